```python
import math
import jax
import jax.numpy as jnp
from jax import lax

D_MODEL = 1024
BATCH = 8
SEQ = 2048
DEPTH = 4
DEC_BATCH = 32
DEC_SEQ = 4
PAST_LEN = 8192
PAGE_SIZE = 128

N_MIXERS = 3
N_SSD = (DEPTH + 2) // 3
N_SWA = (DEPTH + 1) // 3
N_DIFF = DEPTH // 3

D_FF = -((-8 * D_MODEL) // (3 * 256)) * 256

SSM_EXPAND = 2
D_INNER = SSM_EXPAND * D_MODEL
SSM_HEAD_DIM = 64
SSM_HEADS = D_INNER // SSM_HEAD_DIM
SSM_GROUPS = 4
D_STATE = 128
CONV_W = 4
CONV_DIM = D_INNER + 2 * SSM_GROUPS * D_STATE
SSM_IN_DIM = 2 * D_INNER + 2 * SSM_GROUPS * D_STATE + SSM_HEADS
SSD_CHUNK = 128

HEAD_DIM = 64
N_ATTN_HEADS = 12
NUM_BUCKETS = 32
MAX_DISTANCE = 2048

SWA_GROUPS = ((128, 1), (512, 4), (2048, 16))
SWA_HEADS_PER_GROUP = N_ATTN_HEADS // len(SWA_GROUPS)

DIFF_KV_HEADS = 4
DIFF_REP = N_ATTN_HEADS // DIFF_KV_HEADS
DIFF_V_DIM = 2 * HEAD_DIM
DIFF_QKV_DIM = N_ATTN_HEADS * 2 * HEAD_DIM + DIFF_KV_HEADS * 2 * HEAD_DIM + DIFF_KV_HEADS * DIFF_V_DIM
Q_BLOCK = 128

EPS = 1e-6

kernel_name = 'hybrid_ssd_dilated_diff_decoder_step'


def rmsnorm(x, g):
    xf = x.astype(jnp.float32)
    y = xf * lax.rsqrt(jnp.mean(xf * xf, axis=-1, keepdims=True) + EPS)
    return (y * g.astype(jnp.float32)).astype(x.dtype)


def rel_bucket(dist):
    max_exact = NUM_BUCKETS // 2
    d = jnp.maximum(dist, 0)
    df = jnp.maximum(d, 1).astype(jnp.float32)
    large = max_exact + (jnp.log(df / max_exact) / math.log(MAX_DISTANCE / max_exact)
                         * (NUM_BUCKETS - max_exact)).astype(jnp.int32)
    return jnp.where(d < max_exact, d, jnp.minimum(large, NUM_BUCKETS - 1))


def swiglu(h, w_in, w_out):
    g, u = jnp.split(h @ w_in, 2, axis=-1)
    return (jax.nn.silu(g) * u) @ w_out


def segsum(a):
    t = a.shape[-1]
    cs = jnp.cumsum(a, axis=-1)
    diff = cs[..., :, None] - cs[..., None, :]
    return jnp.where(jnp.tril(jnp.ones((t, t), bool)), diff, -jnp.inf)


def ssd_scan(x, dt, a, b, c, h0):
    bt, L, H, P = x.shape
    G, N = b.shape[-2:]
    R = H // G
    q = SSD_CHUNK if L % SSD_CHUNK == 0 else L
    nc = L // q
    xdt = (x * dt[..., None]).reshape(bt, nc, q, G, R, P)
    adt = jnp.transpose((dt * a).reshape(bt, nc, q, G, R), (0, 1, 3, 4, 2))
    b = b.reshape(bt, nc, q, G, N)
    c = c.reshape(bt, nc, q, G, N)
    a_cum = jnp.cumsum(adt, axis=-1)
    cb = jnp.einsum('bcign,bcjgn->bcgij', c, b)
    w = cb[:, :, :, None] * jnp.exp(segsum(adt))
    y_diag = jnp.einsum('bcgrij,bcjgrp->bcigrp', w, xdt)
    decay_states = jnp.exp(a_cum[..., -1:] - a_cum)
    states = jnp.einsum('bcjgn,bcgrj,bcjgrp->bcgrpn', b, decay_states, xdt)
    states = jnp.concatenate([h0.reshape(bt, 1, G, R, P, N), states], axis=1)
    chunk_a = jnp.pad(jnp.transpose(a_cum[..., -1], (0, 2, 3, 1)), ((0, 0), (0, 0), (0, 0), (1, 0)))
    new_states = jnp.einsum('bgrzc,bcgrpn->bzgrpn', jnp.exp(segsum(chunk_a)), states)
    y_off = jnp.einsum('bcign,bcgrpn,bcgri->bcigrp', c, new_states[:, :-1], jnp.exp(a_cum))
    y = (y_diag + y_off).reshape(bt, L, H, P)
    return y, new_states[:, -1].reshape(bt, H, P, N)


def ssd_mixer(h, conv_state, ssm_state, w_in, conv_w, conv_b, dt_bias, a_log, d_skip, norm_g, w_out):
    f32 = jnp.float32
    bt, L, _ = h.shape
    z, xbc, dt = jnp.split(h @ w_in, [D_INNER, D_INNER + CONV_DIM], axis=-1)
    ext = jnp.concatenate([conv_state.astype(xbc.dtype), xbc], axis=1)
    conv = conv_b + sum(ext[:, k:k + L] * conv_w[k] for k in range(CONV_W))
    new_conv = ext[:, L:]
    xs, bs, cs = jnp.split(jax.nn.silu(conv), [D_INNER, D_INNER + SSM_GROUPS * D_STATE], axis=-1)
    xh = xs.reshape(bt, L, SSM_HEADS, SSM_HEAD_DIM).astype(f32)
    dtf = jax.nn.softplus(dt.astype(f32) + dt_bias.astype(f32))
    a = -jnp.exp(a_log.astype(f32))
    y, new_state = ssd_scan(xh, dtf, a,
                            bs.reshape(bt, L, SSM_GROUPS, D_STATE).astype(f32),
                            cs.reshape(bt, L, SSM_GROUPS, D_STATE).astype(f32),
                            ssm_state.astype(f32))
    y = y + xh * d_skip.astype(f32)[:, None]
    gated = (y.reshape(bt, L, D_INNER) * jax.nn.silu(z.astype(f32))).reshape(bt, L, SSM_GROUPS, D_INNER // SSM_GROUPS)
    y = rmsnorm(gated, norm_g.reshape(SSM_GROUPS, D_INNER // SSM_GROUPS)).reshape(bt, L, D_INNER)
    return y.astype(h.dtype) @ w_out, new_conv, new_state.astype(ssm_state.dtype)


def swa_project(h, w_qkv, q_norm, k_norm):
    bt, L, _ = h.shape
    q, k, v = jnp.split(h @ w_qkv, 3, axis=-1)
    shape = (bt, L, N_ATTN_HEADS, HEAD_DIM)
    return rmsnorm(q.reshape(shape), q_norm), rmsnorm(k.reshape(shape), k_norm), v.reshape(shape)


def _to_classes(t, dil, blk, lp):
    bt, L = t.shape[:2]
    rest = t.shape[2:]
    t = jnp.pad(t, [(0, 0), (0, lp - L)] + [(0, 0)] * len(rest))
    t = jnp.moveaxis(t.reshape((bt, lp // dil, dil) + rest), 2, 1)
    return t.reshape((bt, dil, lp // (dil * blk), blk) + rest)


def _from_classes(t, L):
    bt, dil, nb, blk = t.shape[:4]
    rest = t.shape[4:]
    t = jnp.moveaxis(t.reshape((bt, dil, nb * blk) + rest), 1, 2)
    return t.reshape((bt, nb * blk * dil) + rest)[:, :L]


def _with_prev(t):
    prev = jnp.concatenate([jnp.zeros_like(t[:, :, :1]), t[:, :, :-1]], axis=2)
    return jnp.concatenate([prev, t], axis=3)


def dilated_group_prompt(q, k, v, bias_tab, win, dil):
    f32 = jnp.float32
    L = q.shape[1]
    steps = win // dil
    blk = steps
    span = dil * blk
    lp = -(-L // span) * span
    nb = lp // span
    qb = _to_classes(q, dil, blk, lp)
    kk = _with_prev(_to_classes(k, dil, blk, lp))
    vv = _with_prev(_to_classes(v, dil, blk, lp))
    s = jnp.einsum('bcnihd,bcnjhd->bcnhij', qb, kk).astype(f32) * (HEAD_DIM ** -0.5)
    i = jnp.arange(blk)[:, None]
    j = jnp.arange(2 * blk)[None, :]
    step = blk + i - j
    bias = bias_tab[rel_bucket(jnp.clip(step, 0, steps) * dil)].astype(f32)
    s = s + jnp.transpose(bias, (2, 0, 1))
    key_sub = jnp.arange(nb)[:, None, None] * blk + j[None] - blk
    valid = (step >= 0)[None] & (step <= steps)[None] & (key_sub >= 0)
    s = jnp.where(valid[None, None, :, None], s, -jnp.inf)
    lse = jax.nn.logsumexp(s, axis=-1)
    p = jnp.exp(s - lse[..., None])
    o = jnp.einsum('bcnhij,bcnjhd->bcnihd', p, vv.astype(f32))
    return _from_classes(o, L), _from_classes(jnp.swapaxes(lse, -1, -2), L)


def dilated_group_sample(q, k, v, buf, bias_tab, win, dil):
    f32 = jnp.float32
    T = q.shape[1]
    lb = buf.shape[1]
    k_ext = jnp.concatenate([buf[:, :, 0], k.astype(buf.dtype)], axis=1)
    v_ext = jnp.concatenate([buf[:, :, 1], v.astype(buf.dtype)], axis=1)
    steps = win // dil
    n = jnp.arange(steps + 1)
    idx = lb + jnp.arange(T)[:, None] - n[None, :] * dil
    valid = idx >= 0
    idx = jnp.maximum(idx, 0)
    kg = k_ext[:, idx]
    vg = v_ext[:, idx]
    s = jnp.einsum('bthd,btnhd->bthn', q, kg).astype(f32) * (HEAD_DIM ** -0.5)
    s = s + bias_tab[rel_bucket(n * dil)].astype(f32).T
    s = jnp.where(valid[:, None, :], s, -jnp.inf)
    lse = jax.nn.logsumexp(s, axis=-1)
    p = jnp.exp(s - lse[..., None])
    o = jnp.einsum('bthn,btnhd->bthd', p, vg.astype(f32))
    keep = min(win, lb + T)
    new_buf = jnp.concatenate([buf, jnp.stack([k, v], axis=2).astype(buf.dtype)], axis=1)[:, lb + T - keep:]
    return o, lse, new_buf


def swa_combine(outs, lses, w_out):
    alpha = jax.nn.softmax(jnp.stack(lses, axis=0), axis=0)
    o = jnp.concatenate([alpha[g][..., None] * outs[g] for g in range(len(outs))], axis=2)
    bt, L = o.shape[:2]
    return o.reshape(bt, L, -1).astype(w_out.dtype) @ w_out


def swa_prompt(h, w_qkv, q_norm, k_norm, w_out, rel_bias):
    q, k, v = swa_project(h, w_qkv, q_norm, k_norm)
    L = h.shape[1]
    outs, lses, bufs = [], [], []
    for g, (win, dil) in enumerate(SWA_GROUPS):
        sl = slice(g * SWA_HEADS_PER_GROUP, (g + 1) * SWA_HEADS_PER_GROUP)
        o, lse = dilated_group_prompt(q[:, :, sl], k[:, :, sl], v[:, :, sl], rel_bias[:, sl], win, dil)
        outs.append(o)
        lses.append(lse)
        keep = min(win, L)
        bufs.append(jnp.stack([k[:, L - keep:, sl], v[:, L - keep:, sl]], axis=2))
    return swa_combine(outs, lses, w_out), bufs


def swa_sample(h, bufs, w_qkv, q_norm, k_norm, w_out, rel_bias):
    q, k, v = swa_project(h, w_qkv, q_norm, k_norm)
    outs, lses, new_bufs = [], [], []
    for g, (win, dil) in enumerate(SWA_GROUPS):
        sl = slice(g * SWA_HEADS_PER_GROUP, (g + 1) * SWA_HEADS_PER_GROUP)
        o, lse, nbuf = dilated_group_sample(q[:, :, sl], k[:, :, sl], v[:, :, sl], bufs[g], rel_bias[:, sl], win, dil)
        outs.append(o)
        lses.append(lse)
        new_bufs.append(nbuf)
    return swa_combine(outs, lses, w_out), new_bufs


def diff_project(h, w_qkv, q_norm, k_norm):
    bt, L, _ = h.shape
    q, k, v = jnp.split(h @ w_qkv, [N_ATTN_HEADS * 2 * HEAD_DIM, (N_ATTN_HEADS + DIFF_KV_HEADS) * 2 * HEAD_DIM], axis=-1)
    q = rmsnorm(q.reshape(bt, L, DIFF_KV_HEADS, DIFF_REP, 2, HEAD_DIM), q_norm)
    k = rmsnorm(k.reshape(bt, L, DIFF_KV_HEADS, 2, HEAD_DIM), k_norm)
    return q, k, v.reshape(bt, L, DIFF_KV_HEADS, DIFF_V_DIM)


def diff_lambda_value(lam_p, lam_init):
    lp = lam_p.astype(jnp.float32)
    return jnp.exp(jnp.sum(lp[0] * lp[1])) - jnp.exp(jnp.sum(lp[2] * lp[3])) + lam_init


def diff_attend(q, k, v, q_pos, k_pos, lam, rel_bias):
    f32 = jnp.float32
    tq, tk = q.shape[1], k.shape[1]
    s = jnp.einsum('bigrmd,bjgmd->bgrmij', q, k).astype(f32) * (HEAD_DIM ** -0.5)
    dist = q_pos[:, None] - k_pos[None, :]
    bias = rel_bias[rel_bucket(dist)].astype(f32).reshape(tq, tk, DIFF_KV_HEADS, DIFF_REP)
    s = s + jnp.transpose(bias, (2, 3, 0, 1))[:, :, None]
    s = jnp.where(dist >= 0, s, -jnp.inf)
    p = jax.nn.softmax(s, axis=-1)
    a = p[:, :, :, 0] - lam * p[:, :, :, 1]
    return jnp.einsum('bgrij,bjgd->bigrd', a, v.astype(f32))


def diff_output(o, out_norm, lam_init, w_out):
    bt, L = o.shape[:2]
    o = rmsnorm(o.reshape(bt, L, N_ATTN_HEADS, DIFF_V_DIM), out_norm) * (1.0 - lam_init)
    return o.reshape(bt, L, -1).astype(w_out.dtype) @ w_out


def diff_prompt(h, w_qkv, q_norm, k_norm, lam_p, out_norm, w_out, rel_bias, lam_init):
    bt, L, _ = h.shape
    q, k, v = diff_project(h, w_qkv, q_norm, k_norm)
    lam = diff_lambda_value(lam_p, lam_init)
    nb = L // Q_BLOCK
    pos = jnp.arange(L)
    q_blocks = jnp.moveaxis(q.reshape((bt, nb, Q_BLOCK) + q.shape[2:]), 1, 0)
    o = lax.map(lambda a: diff_attend(a[0], k, v, a[1], pos, lam, rel_bias), (q_blocks, pos.reshape(nb, Q_BLOCK)))
    o = jnp.moveaxis(o, 0, 1).reshape(bt, L, DIFF_KV_HEADS, DIFF_REP, DIFF_V_DIM)
    return diff_output(o, out_norm, lam_init, w_out), k, v


def diff_sample(h, cache_k, cache_v, page_table, w_qkv, q_norm, k_norm, lam_p, out_norm, w_out, rel_bias, lam_init):
    bt, T, _ = h.shape
    q, k, v = diff_project(h, w_qkv, q_norm, k_norm)
    lam = diff_lambda_value(lam_p, lam_init)
    past_k = cache_k[page_table]
    past_k = past_k.reshape((bt, -1) + past_k.shape[3:])
    past_v = cache_v[page_table]
    past_v = past_v.reshape((bt, -1) + past_v.shape[3:])
    past = past_k.shape[1]
    k_all = jnp.concatenate([past_k, k.astype(past_k.dtype)], axis=1)
    v_all = jnp.concatenate([past_v, v.astype(past_v.dtype)], axis=1)
    o = diff_attend(q, k_all, v_all, past + jnp.arange(T), jnp.arange(past + T), lam, rel_bias)
    return diff_output(o, out_norm, lam_init, w_out), k, v


def setup_inputs(seed: int = 0) -> dict:
    key = jax.random.key(seed)
    keys = iter(jax.random.split(key, 64))
    f32 = jnp.float32

    def normal(shape, scale=1.0):
        return jax.random.normal(next(keys), shape, f32) * scale

    def gain(shape):
        return 1.0 + normal(shape, 0.02)

    n_pages = PAST_LEN // PAGE_SIZE
    n_used = DEC_BATCH * n_pages
    n_phys = n_used + max(1, n_used // 4)
    page_table = jax.random.permutation(next(keys), n_phys)[:n_used].reshape(DEC_BATCH, n_pages).astype(jnp.int32)
    dt0 = jnp.exp(jax.random.uniform(next(keys), (N_SSD, SSM_HEADS), f32, math.log(1e-3), math.log(1e-1)))
    dt_bias = dt0 + jnp.log(-jnp.expm1(-dt0))
    a_log = jnp.log(jax.random.uniform(next(keys), (N_SSD, SSM_HEADS), f32, 1.0, 16.0))
    swa_w = H_SWA = N_ATTN_HEADS * HEAD_DIM
    return {
        'x_prompt': normal((BATCH, SEQ, D_MODEL)),
        'x_sample': normal((DEC_BATCH, DEC_SEQ, D_MODEL)),
        'state_ssm_conv': normal((N_SSD, DEC_BATCH, CONV_W - 1, CONV_DIM)),
        'state_ssm': normal((N_SSD, DEC_BATCH, SSM_HEADS, SSM_HEAD_DIM, D_STATE), 0.1),
        'cache_swa_kv0': normal((N_SWA, DEC_BATCH, min(SWA_GROUPS[0][0], PAST_LEN), 2, SWA_HEADS_PER_GROUP, HEAD_DIM)),
        'cache_swa_kv1': normal((N_SWA, DEC_BATCH, min(SWA_GROUPS[1][0], PAST_LEN), 2, SWA_HEADS_PER_GROUP, HEAD_DIM)),
        'cache_swa_kv2': normal((N_SWA, DEC_BATCH, min(SWA_GROUPS[2][0], PAST_LEN), 2, SWA_HEADS_PER_GROUP, HEAD_DIM)),
        'cache_diff_k': normal((N_DIFF, n_phys, PAGE_SIZE, DIFF_KV_HEADS, 2, HEAD_DIM)),
        'cache_diff_v': normal((N_DIFF, n_phys, PAGE_SIZE, DIFF_KV_HEADS, DIFF_V_DIM)),
        'page_table': page_table,
        'rel_bias': normal((NUM_BUCKETS, N_ATTN_HEADS), 0.5),
        'norm_mix': gain((DEPTH, D_MODEL)),
        'norm_ffn': gain((DEPTH, D_MODEL)),
        'ffn_w_in': normal((DEPTH, D_MODEL, 2 * D_FF), D_MODEL ** -0.5),
        'ffn_w_out': normal((DEPTH, D_FF, D_MODEL), D_FF ** -0.5),
        'ssm_w_in': normal((N_SSD, D_MODEL, SSM_IN_DIM), D_MODEL ** -0.5),
        'ssm_conv_w': normal((N_SSD, CONV_W, CONV_DIM), CONV_W ** -0.5),
        'ssm_conv_b': normal((N_SSD, CONV_DIM), 0.02),
        'ssm_dt_bias': dt_bias,
        'ssm_a_log': a_log,
        'ssm_d': gain((N_SSD, SSM_HEADS)),
        'ssm_norm': gain((N_SSD, D_INNER)),
        'ssm_w_out': normal((N_SSD, D_INNER, D_MODEL), D_INNER ** -0.5),
        'swa_w_qkv': normal((N_SWA, D_MODEL, 3 * swa_w), D_MODEL ** -0.5),
        'swa_q_norm': gain((N_SWA, HEAD_DIM)),
        'swa_k_norm': gain((N_SWA, HEAD_DIM)),
        'swa_w_out': normal((N_SWA, H_SWA, D_MODEL), H_SWA ** -0.5),
        'diff_w_qkv': normal((N_DIFF, D_MODEL, DIFF_QKV_DIM), D_MODEL ** -0.5),
        'diff_q_norm': gain((N_DIFF, HEAD_DIM)),
        'diff_k_norm': gain((N_DIFF, HEAD_DIM)),
        'diff_lambda': normal((N_DIFF, 4, HEAD_DIM), 0.1),
        'diff_out_norm': gain((N_DIFF, DIFF_V_DIM)),
        'diff_w_out': normal((N_DIFF, N_ATTN_HEADS * DIFF_V_DIM, D_MODEL), (N_ATTN_HEADS * DIFF_V_DIM) ** -0.5),
    }


def reference(x_prompt, x_sample, state_ssm_conv, state_ssm, cache_swa_kv0, cache_swa_kv1, cache_swa_kv2,
              cache_diff_k, cache_diff_v, page_table, rel_bias, norm_mix, norm_ffn, ffn_w_in, ffn_w_out,
              ssm_w_in, ssm_conv_w, ssm_conv_b, ssm_dt_bias, ssm_a_log, ssm_d, ssm_norm, ssm_w_out,
              swa_w_qkv, swa_q_norm, swa_k_norm, swa_w_out,
              diff_w_qkv, diff_q_norm, diff_k_norm, diff_lambda, diff_out_norm, diff_w_out):
    xp, xs = x_prompt, x_sample
    bp = xp.shape[0]
    swa_caches = (cache_swa_kv0, cache_swa_kv1, cache_swa_kv2)
    conv_p, conv_s, ssm_p, ssm_s = [], [], [], []
    swa_p = tuple([] for _ in SWA_GROUPS)
    swa_s = tuple([] for _ in SWA_GROUPS)
    dk_p, dk_s, dv_p, dv_s = [], [], [], []
    i_ssd = i_swa = i_diff = 0
    for layer in range(DEPTH):
        hp = rmsnorm(xp, norm_mix[layer])
        hs = rmsnorm(xs, norm_mix[layer])
        kind = layer % N_MIXERS
        if kind == 0:
            i = i_ssd
            i_ssd += 1
            w = (ssm_w_in[i], ssm_conv_w[i], ssm_conv_b[i], ssm_dt_bias[i], ssm_a_log[i], ssm_d[i], ssm_norm[i], ssm_w_out[i])
            conv0 = jnp.zeros((bp, CONV_W - 1, CONV_DIM), hp.dtype)
            state0 = jnp.zeros((bp, SSM_HEADS, SSM_HEAD_DIM, D_STATE), jnp.float32)
            mp, c_p, s_p = ssd_mixer(hp, conv0, state0, *w)
            ms, c_s, s_s = ssd_mixer(hs, state_ssm_conv[i], state_ssm[i], *w)
            conv_p.append(c_p)
            conv_s.append(c_s)
            ssm_p.append(s_p)
            ssm_s.append(s_s)
        elif kind == 1:
            i = i_swa
            i_swa += 1
            w = (swa_w_qkv[i], swa_q_norm[i], swa_k_norm[i], swa_w_out[i], rel_bias)
            mp, bufs_p = swa_prompt(hp, *w)
            ms, bufs_s = swa_sample(hs, [c[i] for c in swa_caches], *w)
            for g in range(len(SWA_GROUPS)):
                swa_p[g].append(bufs_p[g])
                swa_s[g].append(bufs_s[g])
        else:
            i = i_diff
            i_diff += 1
            lam_init = 0.8 - 0.6 * math.exp(-0.3 * layer)
            w = (diff_w_qkv[i], diff_q_norm[i], diff_k_norm[i], diff_lambda[i], diff_out_norm[i], diff_w_out[i], rel_bias, lam_init)
            mp, k_p, v_p = diff_prompt(hp, *w)
            ms, k_s, v_s = diff_sample(hs, cache_diff_k[i], cache_diff_v[i], page_table, *w)
            dk_p.append(k_p)
            dk_s.append(k_s)
            dv_p.append(v_p)
            dv_s.append(v_s)
        xp = xp + mp.astype(xp.dtype)
        xs = xs + ms.astype(xs.dtype)
        xp = xp + swiglu(rmsnorm(xp, norm_ffn[layer]), ffn_w_in[layer], ffn_w_out[layer]).astype(xp.dtype)
        xs = xs + swiglu(rmsnorm(xs, norm_ffn[layer]), ffn_w_in[layer], ffn_w_out[layer]).astype(xs.dtype)
    ssm_conv_prompt = jnp.stack(conv_p)
    ssm_conv_sample = jnp.stack(conv_s)
    ssm_prompt = jnp.stack(ssm_p)
    ssm_sample = jnp.stack(ssm_s)
    swa_kv0_prompt = jnp.stack(swa_p[0])
    swa_kv0_sample = jnp.stack(swa_s[0])
    swa_kv1_prompt = jnp.stack(swa_p[1])
    swa_kv1_sample = jnp.stack(swa_s[1])
    swa_kv2_prompt = jnp.stack(swa_p[2])
    swa_kv2_sample = jnp.stack(swa_s[2])
    diff_k_prompt = jnp.stack(dk_p)
    diff_k_sample = jnp.stack(dk_s)
    diff_v_prompt = jnp.stack(dv_p)
    diff_v_sample = jnp.stack(dv_s)
    return (xp, xs, ssm_conv_prompt, ssm_conv_sample, ssm_prompt, ssm_sample,
            swa_kv0_prompt, swa_kv0_sample, swa_kv1_prompt, swa_kv1_sample, swa_kv2_prompt, swa_kv2_sample,
            diff_k_prompt, diff_k_sample, diff_v_prompt, diff_v_sample)
```

```python
import functools
import math

import jax
import jax.numpy as jnp
from jax import lax
from jax.experimental import pallas as pl
from jax.experimental.pallas import tpu as pltpu

F32 = jnp.float32
BF16 = jnp.bfloat16

D_MODEL = 1024
DEPTH = 4
D_FF = 2816
D_INNER = 2048
SSM_HEADS = 32
SSM_HEAD_DIM = 64
SSM_GROUPS = 4
D_STATE = 128
CONV_W = 4
CONV_DIM = D_INNER + 2 * SSM_GROUPS * D_STATE
SSD_CHUNK = 128
HEAD_DIM = 64
N_ATTN_HEADS = 12
NUM_BUCKETS = 32
MAX_DISTANCE = 2048
SWA_GROUPS = ((128, 1), (512, 4), (2048, 16))
SWA_HEADS_PER_GROUP = 4
SWA_GROUP_DIM = SWA_HEADS_PER_GROUP * HEAD_DIM
SWA_DIM = N_ATTN_HEADS * HEAD_DIM
SWA_STEPS = 128
DIFF_KV_HEADS = 4
DIFF_REP = 3
DIFF_V_DIM = 128
DIFF_Q_DIM = N_ATTN_HEADS * 2 * HEAD_DIM
DIFF_K_DIM = DIFF_KV_HEADS * 2 * HEAD_DIM
DIFF_V_ALL = DIFF_KV_HEADS * DIFF_V_DIM
PAGE_SIZE = 128
EPS = 1e-6
NEG = -1e30

LANES = 128
FFN_CHUNK = 256
FFN_NCHUNK = D_FF // FFN_CHUNK
VMEM_LIMIT = 56 * 1024 * 1024
PAGES_PER_STEP = 8


def _dot(a, b):
    return jnp.dot(a, b, preferred_element_type=F32)


def _dot_nt(a, b):
    return lax.dot_general(a, b, (((1,), (1,)), ((), ())), preferred_element_type=F32)


def _split3(x):
    hi = x.astype(BF16)
    r = x - hi.astype(F32)
    mid = r.astype(BF16)
    lo = (r - mid.astype(F32)).astype(BF16)
    return hi, mid, lo


def _dot_exact_rhs(a_bf16, x):
    hi, mid, lo = _split3(x)
    return _dot(a_bf16, hi) + _dot(a_bf16, mid) + _dot(a_bf16, lo)


def _dot_exact_lhs(x, a_bf16):
    hi, mid, lo = _split3(x)
    return _dot(hi, a_bf16) + _dot(mid, a_bf16) + _dot(lo, a_bf16)


def _sigmoid(x):
    return 1.0 / (1.0 + jnp.exp(-x))


def _rms(x, g):
    return x * lax.rsqrt(jnp.mean(x * x, axis=-1, keepdims=True) + EPS) * g


def _resident(shape):
    n = len(shape)
    return pl.BlockSpec(shape, lambda *_: (0,) * n, pipeline_mode=pl.Buffered(1))


def _params(sem):
    return pltpu.CompilerParams(dimension_semantics=sem, vmem_limit_bytes=VMEM_LIMIT)


def _norm_proj_kernel(*refs, n_out, head_norm):
    x_ref, g_ref = refs[0], refs[1]
    w_refs = refs[2:2 + n_out]
    n_hn = sum(head_norm)
    hn_refs = refs[2 + n_out:2 + n_out + n_hn]
    pos = 2 + n_out + n_hn
    seg_ref = refs[pos] if n_hn else None
    pos += 1 if n_hn else 0
    o_refs = refs[pos:pos + n_out]
    h = _rms(x_ref[...], g_ref[...]).astype(BF16)
    k = 0
    for i in range(n_out):
        y = _dot(h, w_refs[i][...])
        if head_norm[i]:
            gain = hn_refs[k][...]
            k += 1
            seg = seg_ref[...]
            parts = []
            for c in range(y.shape[1] // seg.shape[0]):
                yc = y[:, c * seg.shape[0]:(c + 1) * seg.shape[0]]
                sq = yc * yc
                hi = sq.astype(BF16)
                lo = (sq - hi.astype(F32)).astype(BF16)
                ms = (_dot(hi, seg) + _dot(lo, seg)) * (1.0 / HEAD_DIM)
                parts.append(yc * lax.rsqrt(ms + EPS))
            y = jnp.concatenate(parts, axis=1) * gain
        o_refs[i][...] = y


def _norm_proj(x, g, weights, head_gains, tm):
    m, d = x.shape
    n_out = len(weights)
    head_norm = tuple(hg is not None for hg in head_gains)
    ins = [x, g.reshape(1, d)] + list(weights)
    specs = [pl.BlockSpec((tm, d), lambda i: (i, 0)), _resident((1, d))]
    specs += [_resident(w.shape) for w in weights]
    for hg in head_gains:
        if hg is not None:
            ins.append(hg)
            specs.append(_resident(hg.shape))
    if any(head_norm):
        r = jnp.arange(2 * LANES)
        seg = (r[:, None] // HEAD_DIM == r[None, :] // HEAD_DIM).astype(BF16)
        ins.append(seg)
        specs.append(_resident(seg.shape))
    return pl.pallas_call(
        functools.partial(_norm_proj_kernel, n_out=n_out, head_norm=head_norm),
        grid=(m // tm,),
        in_specs=specs,
        out_specs=[pl.BlockSpec((tm, w.shape[1]), lambda i: (i, 0)) for w in weights],
        out_shape=[jax.ShapeDtypeStruct((m, w.shape[1]), F32) for w in weights],
        compiler_params=_params(("parallel",)),
        name="norm_proj",
    )(*ins)


def _mix_ffn_kernel(x_ref, a_ref, wo_ref, g_ref, w1_ref, w2_ref, o_ref, acc_ref):
    x1 = x_ref[...] + _dot(a_ref[...], wo_ref[...])
    h = _rms(x1, g_ref[...]).astype(BF16)
    acc_ref[...] = x1

    def body(c, carry):
        gate = _dot(h, w1_ref[0, c])
        up = _dot(h, w1_ref[1, c])
        act = (gate * _sigmoid(gate) * up).astype(BF16)
        acc_ref[...] += _dot(act, w2_ref[c])
        return carry

    lax.fori_loop(0, FFN_NCHUNK, body, 0)
    o_ref[...] = acc_ref[...]


def _mix_ffn(x, a, wo, g, w1, w2, tm):
    m, d = x.shape
    ka = a.shape[1]
    return pl.pallas_call(
        _mix_ffn_kernel,
        grid=(m // tm,),
        in_specs=[pl.BlockSpec((tm, d), lambda i: (i, 0)),
                  pl.BlockSpec((tm, ka), lambda i: (i, 0)),
                  _resident(wo.shape), _resident((1, d)), _resident(w1.shape), _resident(w2.shape)],
        out_specs=pl.BlockSpec((tm, d), lambda i: (i, 0)),
        out_shape=jax.ShapeDtypeStruct((m, d), F32),
        scratch_shapes=[pltpu.VMEM((tm, d), F32)],
        compiler_params=_params(("parallel",)),
        name="mix_ffn",
    )(x, a, wo, g.reshape(1, d), w1, w2)


def _ssd_kernel(z_ref, xbc_ref, dt_ref, cst_ref, sst_ref, cw_ref, cb_ref, dtb_ref, alog_ref,
                dx_ref, ng_ref, e_ref, y_ref, ns_ref, ext_ref, st_ref, yacc_ref, *pad_refs,
                q, n_valid):
    c = pl.program_id(1)
    last = pl.num_programs(1) - 1
    padded = n_valid < q

    @pl.when(c == 0)
    def _():
        ext_ref[...] = jnp.zeros(ext_ref.shape, F32)
        ext_ref[5:8, :] = cst_ref[0]
        st_ref[...] = sst_ref[0]
        if padded:
            for r in pad_refs:
                r[...] = jnp.zeros(r.shape, F32)

    ext_ref[8:8 + n_valid, :] = xbc_ref[0]
    conv = cb_ref[...]
    for k in range(CONV_W):
        conv = conv + ext_ref[5 + k:5 + k + q, :] * cw_ref[k:k + 1, :]
    if not padded:
        ext_ref[5:8, :] = ext_ref[q + 5:q + 8, :]
    act = conv * _sigmoid(conv)
    xs = act[:, :D_INNER]
    bm = act[:, D_INNER:D_INNER + SSM_GROUPS * D_STATE].astype(BF16)
    cm = act[:, D_INNER + SSM_GROUPS * D_STATE:].astype(BF16)

    if padded:
        zpad_ref, dtpad_ref = pad_refs
        zpad_ref[0:n_valid, :] = z_ref[0]
        dtpad_ref[0:n_valid, :] = dt_ref[0]
        z = zpad_ref[...]
        dt_raw = dtpad_ref[...]
    else:
        z = z_ref[0]
        dt_raw = dt_ref[0]

    row = lax.broadcasted_iota(jnp.int32, (q, q), 0)
    col = lax.broadcasted_iota(jnp.int32, (q, q), 1)
    causal = col <= row
    tri = jnp.where(causal, 1.0, 0.0).astype(BF16)
    tri_t = jnp.where(row <= col, 1.0, 0.0).astype(BF16)

    xdt_in = dt_raw + dtb_ref[...]
    dt = jnp.maximum(xdt_in, 0.0) + jnp.log(1.0 + jnp.exp(-jnp.abs(xdt_in)))
    if padded:
        dt = jnp.where(lax.broadcasted_iota(jnp.int32, dt.shape, 0) < n_valid, dt, 0.0)
    adt = dt * (-jnp.exp(alog_ref[...]))
    acum = _dot_exact_rhs(tri, adt)
    acum_t = _dot_exact_lhs(adt.T, tri_t)
    e = e_ref[...]
    dt_x = _dot_exact_lhs(dt, e)
    acum_x = _dot_exact_lhs(acum, e)
    exp_acum_x = jnp.exp(acum_x)
    acum_last_x = acum_x[q - 1:q, :]
    xdt = xs * dt_x
    xdt_b = xdt.astype(BF16)
    xd = xdt * jnp.exp(acum_last_x - acum_x)
    dtot_x = jnp.exp(acum_last_x)
    lane_lo = lax.broadcasted_iota(jnp.int32, (q, LANES), 1) < SSM_HEAD_DIM

    gw = SSM_HEADS // SSM_GROUPS * SSM_HEAD_DIM
    for g in range(SSM_GROUPS):
        cg = cm[:, g * D_STATE:(g + 1) * D_STATE]
        bg = bm[:, g * D_STATE:(g + 1) * D_STATE]
        cb = _dot_nt(cg, bg)
        st_g = st_ref[g * gw:(g + 1) * gw, :]
        yoff = _dot_nt(cg, st_g.astype(BF16))
        yacc_ref[:, g * gw:(g + 1) * gw] = yoff * exp_acum_x[:, g * gw:(g + 1) * gw]
        for pair in range(gw // LANES):
            h0 = g * (gw // SSM_HEAD_DIM) + 2 * pair
            xp = xdt_b[:, h0 * SSM_HEAD_DIM:h0 * SSM_HEAD_DIM + LANES]
            ys = []
            for h in (h0, h0 + 1):
                seg = jnp.broadcast_to(acum[:, h:h + 1], (q, q)) - jnp.broadcast_to(acum_t[h:h + 1, :], (q, q))
                w = cb * jnp.exp(jnp.where(causal, seg, NEG))
                ys.append(_dot(w.astype(BF16), xp))
            yacc_ref[:, h0 * SSM_HEAD_DIM:h0 * SSM_HEAD_DIM + LANES] += jnp.where(lane_lo, ys[0], ys[1])
        xd_t = jnp.concatenate(
            [xd[:, g * gw + t * LANES:g * gw + (t + 1) * LANES].T for t in range(gw // LANES)], axis=0)
        upd = _dot(xd_t.astype(BF16), bg)
        dtot = jnp.concatenate(
            [jnp.broadcast_to(dtot_x[:, g * gw + t * LANES:g * gw + (t + 1) * LANES], (LANES, LANES)).T
             for t in range(gw // LANES)], axis=0)
        st_ref[g * gw:(g + 1) * gw, :] = dtot * st_g + upd

    y = yacc_ref[...] + xs * dx_ref[...]
    gated = y * (z * _sigmoid(z))
    outs = []
    for g in range(SSM_GROUPS):
        gg = gated[:, g * gw:(g + 1) * gw]
        outs.append(gg * lax.rsqrt(jnp.mean(gg * gg, axis=-1, keepdims=True) + EPS))
    yn = jnp.concatenate(outs, axis=1) * ng_ref[...]
    y_ref[0] = yn[0:n_valid].astype(BF16)

    @pl.when(c == last)
    def _():
        ns_ref[0] = st_ref[...]


def _ssd_mixer(z, xbc, dt, conv_state, ssm_state, conv_w, conv_b, dt_bias, a_log, d_skip, norm_g):
    bt, seq, _ = z.shape
    q = SSD_CHUNK
    n_valid = min(q, seq)
    nc = max(1, seq // q)
    pad = lambda v: jnp.pad(v.astype(F32), (0, LANES - SSM_HEADS)).reshape(1, LANES)
    hh = jnp.arange(LANES)[:, None]
    cc = jnp.arange(D_INNER)[None, :] // SSM_HEAD_DIM
    expand = (hh == cc).astype(BF16)
    dx = jnp.repeat(d_skip.astype(F32), SSM_HEAD_DIM).reshape(1, D_INNER)
    sst = ssm_state.reshape(bt, SSM_HEADS * SSM_HEAD_DIM, D_STATE)
    scratch = [pltpu.VMEM((q + 8, CONV_DIM), F32),
               pltpu.VMEM((SSM_HEADS * SSM_HEAD_DIM, D_STATE), F32),
               pltpu.VMEM((q, D_INNER), F32)]
    if n_valid < q:
        scratch += [pltpu.VMEM((q, D_INNER), F32), pltpu.VMEM((q, LANES), F32)]
    y, ns = pl.pallas_call(
        functools.partial(_ssd_kernel, q=q, n_valid=n_valid),
        grid=(bt, nc),
        in_specs=[pl.BlockSpec((1, n_valid, D_INNER), lambda b, c: (b, c, 0)),
                  pl.BlockSpec((1, n_valid, CONV_DIM), lambda b, c: (b, c, 0)),
                  pl.BlockSpec((1, n_valid, LANES), lambda b, c: (b, c, 0)),
                  pl.BlockSpec((1, CONV_W - 1, CONV_DIM), lambda b, c: (b, 0, 0)),
                  pl.BlockSpec((1, SSM_HEADS * SSM_HEAD_DIM, D_STATE), lambda b, c: (b, 0, 0)),
                  _resident((CONV_W, CONV_DIM)), _resident((1, CONV_DIM)),
                  _resident((1, LANES)), _resident((1, LANES)),
                  _resident((1, D_INNER)), _resident((1, D_INNER)), _resident((LANES, D_INNER))],
        out_specs=[pl.BlockSpec((1, n_valid, D_INNER), lambda b, c: (b, c, 0)),
                   pl.BlockSpec((1, SSM_HEADS * SSM_HEAD_DIM, D_STATE), lambda b, c: (b, 0, 0))],
        out_shape=[jax.ShapeDtypeStruct((bt, seq, D_INNER), BF16),
                   jax.ShapeDtypeStruct((bt, SSM_HEADS * SSM_HEAD_DIM, D_STATE), F32)],
        scratch_shapes=scratch,
        compiler_params=_params(("parallel", "arbitrary")),
        name="ssd_mixer",
    )(z, xbc, dt, conv_state, sst, conv_w, conv_b.reshape(1, CONV_DIM), pad(dt_bias), pad(a_log),
      dx, norm_g.reshape(1, D_INNER), expand)
    return y, ns.reshape(bt, SSM_HEADS, SSM_HEAD_DIM, D_STATE)


def _t5_bucket(dist):
    max_exact = NUM_BUCKETS // 2
    d = jnp.maximum(dist, 0)
    df = jnp.maximum(d, 1).astype(F32)
    large = max_exact + (jnp.log(df / max_exact) / math.log(MAX_DISTANCE / max_exact)
                         * (NUM_BUCKETS - max_exact)).astype(jnp.int32)
    return jnp.where(d < max_exact, d, jnp.minimum(large, NUM_BUCKETS - 1))


def _head_lanes(t):
    return jnp.repeat(t, HEAD_DIM, axis=-1)


def _swa_prompt_kernel(q_ref, kp_ref, kc_ref, vp_ref, vc_ref, b_ref, o_ref, l_ref):
    blk = q_ref.shape[1]
    qv = q_ref[0] * (HEAD_DIM ** -0.5)
    kk = jnp.concatenate([kp_ref[0], kc_ref[0]], axis=0).astype(BF16)
    vv = jnp.concatenate([vp_ref[0], vc_ref[0]], axis=0).astype(BF16)
    lane = lax.broadcasted_iota(jnp.int32, (blk, SWA_GROUP_DIM), 1) // HEAD_DIM
    o = jnp.zeros((blk, SWA_GROUP_DIM), F32)
    lse = jnp.zeros((blk, SWA_GROUP_DIM), F32)
    for h in range(SWA_HEADS_PER_GROUP):
        mine = lane == h
        qh = jnp.where(mine, qv, 0.0).astype(BF16)
        s = _dot_nt(qh, kk) + b_ref[0, h]
        m = jnp.max(s, axis=-1, keepdims=True)
        p = jnp.exp(s - m)
        l = jnp.sum(p, axis=-1, keepdims=True)
        oh = _dot(p.astype(BF16), vv) / l
        o = jnp.where(mine, oh, o)
        lse = jnp.where(mine, m + jnp.log(l), lse)
    o_ref[0] = o
    l_ref[0] = lse


def _swa_prompt_group(q, k, v, g, dil, bias, bt, seq):
    nb = seq // (dil * SWA_STEPS)
    ncol = SWA_DIM // SWA_GROUP_DIM
    qv, kv, vv = (t.reshape(bt, seq // dil, dil * SWA_DIM) for t in (q, k, v))
    blk = (1, SWA_STEPS, SWA_GROUP_DIM)
    cur = pl.BlockSpec(blk, lambda b, c, n: (b, n, c * ncol + g))
    prev = pl.BlockSpec(blk, lambda b, c, n: (b, jnp.maximum(n - 1, 0), c * ncol + g))
    o, lse = pl.pallas_call(
        _swa_prompt_kernel,
        grid=(bt, dil, nb),
        in_specs=[cur, prev, cur, prev, cur,
                  pl.BlockSpec((1, SWA_HEADS_PER_GROUP, SWA_STEPS, 2 * SWA_STEPS),
                               lambda b, c, n: (jnp.minimum(n, 1), 0, 0, 0))],
        out_specs=[pl.BlockSpec(blk, lambda b, c, n: (b, n, c)),
                   pl.BlockSpec(blk, lambda b, c, n: (b, n, c))],
        out_shape=[jax.ShapeDtypeStruct((bt, seq // dil, dil * SWA_GROUP_DIM), F32)] * 2,
        compiler_params=_params(("parallel", "parallel", "parallel")),
        name="swa_prompt",
    )(qv, kv, kv, vv, vv, bias)
    return o.reshape(bt * seq, SWA_GROUP_DIM), lse.reshape(bt * seq, SWA_GROUP_DIM)


def _swa_prompt_bias(rel_bias, g, dil):
    i = jnp.arange(SWA_STEPS)[:, None]
    j = jnp.arange(2 * SWA_STEPS)[None, :]
    step = SWA_STEPS + i - j
    tab = rel_bias[:, g * SWA_HEADS_PER_GROUP:(g + 1) * SWA_HEADS_PER_GROUP].astype(F32)
    bias = jnp.transpose(tab[_t5_bucket(jnp.clip(step, 0, SWA_STEPS) * dil)], (2, 0, 1))
    valid = (step >= 0) & (step <= SWA_STEPS)
    first = valid & (j >= SWA_STEPS)
    return jnp.stack([jnp.where(first, bias, NEG), jnp.where(valid, bias, NEG)])


def _swa_combine_kernel(o0, o1, o2, l0, l1, l2, a_ref):
    ls = (l0[...], l1[...], l2[...])
    m = jnp.maximum(jnp.maximum(ls[0], ls[1]), ls[2])
    es = [jnp.exp(l - m) for l in ls]
    inv = 1.0 / (es[0] + es[1] + es[2])
    for g, o in enumerate((o0, o1, o2)):
        a_ref[:, g * SWA_GROUP_DIM:(g + 1) * SWA_GROUP_DIM] = (o[...] * es[g] * inv).astype(BF16)


def _swa_combine(outs, lses, tm):
    m = outs[0].shape[0]
    spec = pl.BlockSpec((tm, SWA_GROUP_DIM), lambda i: (i, 0))
    return pl.pallas_call(
        _swa_combine_kernel,
        grid=(m // tm,),
        in_specs=[spec] * 6,
        out_specs=pl.BlockSpec((tm, SWA_DIM), lambda i: (i, 0)),
        out_shape=jax.ShapeDtypeStruct((m, SWA_DIM), BF16),
        compiler_params=_params(("parallel",)),
        name="swa_combine",
    )(*outs, *lses)


def _swa_sample_kernel(q_ref, kn_ref, vn_ref, c0_ref, c1_ref, c2_ref, b_ref, nb_ref, seg_ref, a_ref, *, t_new):
    seg = seg_ref[...]
    gd = SWA_GROUP_DIM

    def seg_sum(x):
        hi = x.astype(BF16)
        lo = (x - hi.astype(F32)).astype(BF16)
        return _dot(hi, seg) + _dot(lo, seg)

    caches = (c0_ref, c1_ref, c2_ref)
    outs, lses = [], []
    for g, (_, dil) in enumerate(SWA_GROUPS):
        kn = kn_ref[0, :, g * gd:(g + 1) * gd]
        vn = vn_ref[0, :, g * gd:(g + 1) * gd]
        og, lg = [], []
        for t in range(t_new):
            off = 0 if dil == 1 else t * 2 * gd
            kt = caches[g][0, :, off:off + gd]
            vt = caches[g][0, :, off + gd:off + 2 * gd]
            qt = q_ref[0, t:t + 1, g * gd:(g + 1) * gd] * (HEAD_DIM ** -0.5)
            s = seg_sum(kt * qt) + b_ref[g, t]
            sn = seg_sum(kn * qt) + nb_ref[g, t]
            m = jnp.maximum(jnp.max(s, axis=0, keepdims=True), jnp.max(sn, axis=0, keepdims=True))
            p = jnp.exp(s - m)
            pn = jnp.exp(sn - m)
            l = jnp.sum(p, axis=0, keepdims=True) + jnp.sum(pn, axis=0, keepdims=True)
            o = (jnp.sum(p * vt, axis=0, keepdims=True) + jnp.sum(pn * vn, axis=0, keepdims=True)) / l
            og.append(o)
            lg.append(m + jnp.log(l))
        outs.append(jnp.concatenate(og, axis=0))
        lses.append(jnp.concatenate(lg, axis=0))
    m = jnp.maximum(jnp.maximum(lses[0], lses[1]), lses[2])
    es = [jnp.exp(l - m) for l in lses]
    inv = 1.0 / (es[0] + es[1] + es[2])
    for g in range(len(SWA_GROUPS)):
        a_ref[0, :, g * gd:(g + 1) * gd] = (outs[g] * es[g] * inv).astype(BF16)


def _swa_sample_bias(rel_bias, t_new):
    m = jnp.arange(SWA_STEPS)
    t = jnp.arange(t_new)
    tiles, new_tiles = [], []
    for g, (win, dil) in enumerate(SWA_GROUPS):
        tab = rel_bias[:, g * SWA_HEADS_PER_GROUP:(g + 1) * SWA_HEADS_PER_GROUP].astype(F32)
        if dil == 1:
            dist = SWA_STEPS + t[:, None] - m[None, :]
            ok = dist <= SWA_STEPS
            nd = t[:, None] - t[None, :]
            nok = nd >= 0
        else:
            dist = jnp.broadcast_to(((SWA_STEPS - m) * dil)[None, :], (t_new, SWA_STEPS))
            ok = jnp.ones_like(dist, bool)
            nd = jnp.zeros((t_new, t_new), jnp.int32)
            nok = t[:, None] == t[None, :]
        tiles.append(jnp.where(ok[..., None], _head_lanes(tab[_t5_bucket(dist)]), NEG))
        new_tiles.append(jnp.where(nok[..., None], _head_lanes(tab[_t5_bucket(nd)]), NEG))
    return jnp.stack(tiles), jnp.stack(new_tiles)


def _swa_sample(q, k, v, caches, rel_bias, bt, t_new):
    bias, new_bias = _swa_sample_bias(rel_bias, t_new)
    r = jnp.arange(2 * LANES)
    seg = (r[:, None] // HEAD_DIM == r[None, :] // HEAD_DIM).astype(BF16)
    row = 2 * SWA_GROUP_DIM
    views, specs = [], []
    for g, (win, dil) in enumerate(SWA_GROUPS):
        views.append(caches[g].reshape(bt, win // dil, dil * row))
        width = min(dil, t_new) * row
        specs.append(pl.BlockSpec((1, SWA_STEPS, width), lambda b: (b, 0, 0)))
    tok = pl.BlockSpec((1, t_new, SWA_DIM), lambda b: (b, 0, 0))
    a = pl.pallas_call(
        functools.partial(_swa_sample_kernel, t_new=t_new),
        grid=(bt,),
        in_specs=[tok, tok, tok] + specs + [_resident(bias.shape), _resident(new_bias.shape), _resident(seg.shape)],
        out_specs=tok,
        out_shape=jax.ShapeDtypeStruct((bt, t_new, SWA_DIM), BF16),
        compiler_params=_params(("parallel",)),
        name="swa_sample",
    )(q, k, v, *views, bias, new_bias, seg)
    return a.reshape(bt * t_new, SWA_DIM)


def _diff_lambda(lam_ref, lam_init):
    lp = lam_ref[...]
    s1 = jnp.sum(lp[0:1] * lp[1:2], axis=-1, keepdims=True)
    s2 = jnp.sum(lp[2:3] * lp[3:4], axis=-1, keepdims=True)
    return jnp.exp(s1) - jnp.exp(s2) + lam_init


def _diff_prompt_kernel(q_ref, k_ref, v_ref, b_ref, lam_ref, on_ref, o_ref, q6_ref, m_ref, l_ref, acc_ref,
                        *, tq, lam_init):
    i = pl.program_id(2)
    nmap = 2 * DIFF_REP
    lane_lo = lax.broadcasted_iota(jnp.int32, (tq, LANES), 1) < HEAD_DIM
    for r in range(DIFF_REP):
        qr = q_ref[0, :, r * LANES:(r + 1) * LANES] * (HEAD_DIM ** -0.5)
        q6_ref[(2 * r) * tq:(2 * r + 1) * tq, :] = jnp.where(lane_lo, qr, 0.0).astype(BF16)
        q6_ref[(2 * r + 1) * tq:(2 * r + 2) * tq, :] = jnp.where(lane_lo, 0.0, qr).astype(BF16)
    m_ref[...] = jnp.full(m_ref.shape, NEG, F32)
    l_ref[...] = jnp.zeros(l_ref.shape, F32)
    acc_ref[...] = jnp.zeros(acc_ref.shape, F32)

    def body(j, carry):
        start = pl.multiple_of(j * tq, tq)
        kb = k_ref[0, pl.ds(start, tq), :].astype(BF16)
        vb = v_ref[0, pl.ds(start, tq), :].astype(BF16)
        s = _dot_nt(q6_ref[...], kb)
        d = i - j
        bias = jnp.concatenate([b_ref[0, d, r // 2] for r in range(nmap)], axis=0)
        s = s + bias
        m_old = m_ref[...]
        m_new = jnp.maximum(m_old, jnp.max(s, axis=-1, keepdims=True))
        alpha = jnp.exp(m_old - m_new)
        p = jnp.exp(s - m_new)
        l_ref[...] = alpha * l_ref[...] + jnp.sum(p, axis=-1, keepdims=True)
        acc_ref[...] = alpha * acc_ref[...] + _dot(p.astype(BF16), vb)
        m_ref[...] = m_new
        return carry

    lax.fori_loop(0, i + 1, body, 0)
    lam = _diff_lambda(lam_ref, lam_init)
    o = acc_ref[...] / l_ref[...]
    for r in range(DIFF_REP):
        a = o[(2 * r) * tq:(2 * r + 1) * tq] - lam * o[(2 * r + 1) * tq:(2 * r + 2) * tq]
        o_ref[0, :, r * LANES:(r + 1) * LANES] = (_rms(a, on_ref[...]) * (1.0 - lam_init)).astype(BF16)


def _diff_prompt_bias(rel_bias, seq, tq):
    nb = seq // tq
    d = jnp.arange(nb)[:, None, None] * tq + jnp.arange(tq)[None, :, None] - jnp.arange(tq)[None, None, :]
    tab = rel_bias.astype(F32)[_t5_bucket(d)]
    tab = jnp.where((d >= 0)[..., None], tab, NEG)
    tab = jnp.transpose(tab, (3, 0, 1, 2)).reshape(DIFF_KV_HEADS, DIFF_REP, nb, tq, tq)
    return jnp.transpose(tab, (0, 2, 1, 3, 4))


def _diff_prompt(q, k, v, rel_bias, lam_p, out_norm, lam_init, bt, seq):
    tq = 128
    nb = seq // tq
    bias = _diff_prompt_bias(rel_bias, seq, tq)
    qw = DIFF_REP * 2 * HEAD_DIM
    a = pl.pallas_call(
        functools.partial(_diff_prompt_kernel, tq=tq, lam_init=lam_init),
        grid=(bt, DIFF_KV_HEADS, nb),
        in_specs=[pl.BlockSpec((1, tq, qw), lambda b, g, i: (b, i, g)),
                  pl.BlockSpec((1, seq, LANES), lambda b, g, i: (b, 0, g)),
                  pl.BlockSpec((1, seq, LANES), lambda b, g, i: (b, 0, g)),
                  pl.BlockSpec((1, nb, DIFF_REP, tq, tq), lambda b, g, i: (g, 0, 0, 0, 0)),
                  _resident((4, HEAD_DIM)), _resident((1, DIFF_V_DIM))],
        out_specs=pl.BlockSpec((1, tq, qw), lambda b, g, i: (b, i, g)),
        out_shape=jax.ShapeDtypeStruct((bt, seq, DIFF_Q_DIM), BF16),
        scratch_shapes=[pltpu.VMEM((2 * DIFF_REP * tq, LANES), BF16),
                        pltpu.VMEM((2 * DIFF_REP * tq, 1), F32),
                        pltpu.VMEM((2 * DIFF_REP * tq, 1), F32),
                        pltpu.VMEM((2 * DIFF_REP * tq, DIFF_V_DIM), F32)],
        compiler_params=_params(("parallel", "parallel", "arbitrary")),
        name="diff_prompt",
    )(q.reshape(bt, seq, DIFF_Q_DIM), k.reshape(bt, seq, DIFF_K_DIM), v.reshape(bt, seq, DIFF_V_ALL),
      bias, lam_p.astype(F32), out_norm.reshape(1, DIFF_V_DIM).astype(F32))
    return a.reshape(bt * seq, DIFF_Q_DIM)


def _diff_sample_kernel(pt_ref, qb_ref, kn_ref, vn_ref, *refs, t_new, lam_init):
    npg = PAGES_PER_STEP
    k_refs = refs[:npg]
    v_refs = refs[npg:2 * npg]
    b_ref, nb_ref, lam_ref, on_ref, o_ref, m_ref, l_ref, acc_ref = refs[2 * npg:]
    s_idx = pl.program_id(1)
    half = DIFF_KV_HEADS * DIFF_REP * t_new

    @pl.when(s_idx == 0)
    def _():
        m_ref[...] = jnp.full(m_ref.shape, NEG, F32)
        l_ref[...] = jnp.zeros(l_ref.shape, F32)
        acc_ref[...] = jnp.zeros(acc_ref.shape, F32)

    qb = (qb_ref[0] * (HEAD_DIM ** -0.5)).astype(BF16)

    def absorb(kb, vb, bias):
        s = _dot_nt(qb, kb.astype(BF16)) + bias
        m_old = m_ref[...]
        m_new = jnp.maximum(m_old, jnp.max(s, axis=-1, keepdims=True))
        alpha = jnp.exp(m_old - m_new)
        p = jnp.exp(s - m_new)
        l_ref[...] = alpha * l_ref[...] + jnp.sum(p, axis=-1, keepdims=True)
        acc_ref[...] = alpha * acc_ref[...] + _dot(p.astype(BF16), vb.astype(BF16))
        m_ref[...] = m_new

    for pp in range(npg):
        absorb(k_refs[pp][0], v_refs[pp][0], b_ref[pp])

    @pl.when(s_idx == pl.num_programs(1) - 1)
    def _():
        absorb(kn_ref[0], vn_ref[0], nb_ref[...])
        lam = _diff_lambda(lam_ref, lam_init)
        o = acc_ref[...] / l_ref[...]
        a = o[0:half] - lam * o[half:2 * half]
        for g in range(DIFF_KV_HEADS):
            for r in range(DIFF_REP):
                r0 = (g * DIFF_REP + r) * t_new
                blk = a[r0:r0 + t_new, g * DIFF_V_DIM:(g + 1) * DIFF_V_DIM]
                col = (g * DIFF_REP + r) * DIFF_V_DIM
                o_ref[0, :, col:col + DIFF_V_DIM] = (_rms(blk, on_ref[...]) * (1.0 - lam_init)).astype(BF16)


def _diff_sample(q, k, v, cache_k, cache_v, page_table, rel_bias, lam_p, out_norm, lam_init, bt, t_new):
    n_pages = page_table.shape[1]
    past = n_pages * PAGE_SIZE
    n_phys = cache_k.shape[0]
    half = DIFF_KV_HEADS * DIFF_REP * t_new
    q6 = q.reshape(bt, t_new, DIFF_KV_HEADS, DIFF_REP, 2, HEAD_DIM)
    q6 = jnp.transpose(q6, (0, 4, 2, 3, 1, 5))
    eye_m = jnp.eye(2, dtype=F32)
    eye_g = jnp.eye(DIFF_KV_HEADS, dtype=F32)
    qb = jnp.einsum('bmgrtd,mn,gh->bmgrthnd', q6, eye_m, eye_g).reshape(bt, 2 * half, DIFF_K_DIM)
    tab = rel_bias.astype(F32)
    tt = jnp.arange(t_new)
    kpos = jnp.arange(past)
    dist = past + tt[:, None] - kpos[None, :]
    bias = jnp.transpose(tab[_t5_bucket(dist)], (2, 0, 1))
    bias = jnp.tile(bias.reshape(1, half, past), (2, 1, 1)).reshape(2 * half, n_pages, PAGE_SIZE)
    bias = jnp.transpose(bias, (1, 0, 2))
    nd = tt[:, None] - jnp.arange(PAGE_SIZE)[None, :]
    nok = (nd >= 0)
    nbias = jnp.where(nok[None], jnp.transpose(tab[_t5_bucket(nd)], (2, 0, 1)), NEG)
    nbias = jnp.tile(nbias.reshape(1, half, PAGE_SIZE), (2, 1, 1)).reshape(2 * half, PAGE_SIZE)
    padk = jnp.pad(k.reshape(bt, t_new, DIFF_K_DIM), ((0, 0), (0, PAGE_SIZE - t_new), (0, 0)))
    padv = jnp.pad(v.reshape(bt, t_new, DIFF_V_ALL), ((0, 0), (0, PAGE_SIZE - t_new), (0, 0)))
    ck = cache_k.reshape(n_phys, PAGE_SIZE, DIFF_K_DIM)
    cv = cache_v.reshape(n_phys, PAGE_SIZE, DIFF_V_ALL)
    npg = PAGES_PER_STEP

    def page_spec(pp):
        return pl.BlockSpec((1, PAGE_SIZE, DIFF_K_DIM), lambda b, s, pt: (pt[b, s * npg + pp], 0, 0))

    const = lambda shape: pl.BlockSpec(shape, lambda b, s, pt: (0,) * len(shape))
    grid_spec = pltpu.PrefetchScalarGridSpec(
        num_scalar_prefetch=1,
        grid=(bt, n_pages // npg),
        in_specs=[pl.BlockSpec((1, 2 * half, DIFF_K_DIM), lambda b, s, pt: (b, 0, 0)),
                  pl.BlockSpec((1, PAGE_SIZE, DIFF_K_DIM), lambda b, s, pt: (b, 0, 0)),
                  pl.BlockSpec((1, PAGE_SIZE, DIFF_V_ALL), lambda b, s, pt: (b, 0, 0))]
                 + [page_spec(pp) for pp in range(npg)] * 2
                 + [pl.BlockSpec((npg, 2 * half, PAGE_SIZE), lambda b, s, pt: (s, 0, 0)),
                    const((2 * half, PAGE_SIZE)), const((4, HEAD_DIM)), const((1, DIFF_V_DIM))],
        out_specs=pl.BlockSpec((1, t_new, DIFF_Q_DIM), lambda b, s, pt: (b, 0, 0)),
        scratch_shapes=[pltpu.VMEM((2 * half, 1), F32), pltpu.VMEM((2 * half, 1), F32),
                        pltpu.VMEM((2 * half, DIFF_V_ALL), F32)],
    )
    a = pl.pallas_call(
        functools.partial(_diff_sample_kernel, t_new=t_new, lam_init=lam_init),
        grid_spec=grid_spec,
        out_shape=jax.ShapeDtypeStruct((bt, t_new, DIFF_Q_DIM), BF16),
        compiler_params=_params(("parallel", "arbitrary")),
        name="diff_sample",
    )(page_table, qb, padk, padv, *([ck] * npg), *([cv] * npg), bias, nbias,
      lam_p.astype(F32), out_norm.reshape(1, DIFF_V_DIM).astype(F32))
    return a.reshape(bt * t_new, DIFF_Q_DIM)


def _ffn_weights(w_in, w_out):
    w1 = w_in.astype(BF16).reshape(D_MODEL, 2, FFN_NCHUNK, FFN_CHUNK)
    w1 = jnp.transpose(w1, (1, 2, 0, 3))
    w2 = w_out.astype(BF16).reshape(FFN_NCHUNK, FFN_CHUNK, D_MODEL)
    return w1, w2


def kernel(x_prompt, x_sample, state_ssm_conv, state_ssm, cache_swa_kv0, cache_swa_kv1, cache_swa_kv2, cache_diff_k, cache_diff_v, page_table, rel_bias, norm_mix, norm_ffn, ffn_w_in, ffn_w_out, ssm_w_in, ssm_conv_w, ssm_conv_b, ssm_dt_bias, ssm_a_log, ssm_d, ssm_norm, ssm_w_out, swa_w_qkv, swa_q_norm, swa_k_norm, swa_w_out, diff_w_qkv, diff_q_norm, diff_k_norm, diff_lambda, diff_out_norm, diff_w_out):
    bp, seq, d = x_prompt.shape
    bs, t_new, _ = x_sample.shape
    mp, ms = bp * seq, bs * t_new
    tm_p, tm_s = 256, ms
    xp = x_prompt.reshape(mp, d)
    xs = x_sample.reshape(ms, d)
    swa_caches = (cache_swa_kv0, cache_swa_kv1, cache_swa_kv2)
    conv_p, conv_s, ssm_p, ssm_s = [], [], [], []
    swa_p = tuple([] for _ in SWA_GROUPS)
    swa_s = tuple([] for _ in SWA_GROUPS)
    dk_p, dk_s, dv_p, dv_s = [], [], [], []
    i_ssd = i_swa = i_diff = 0
    for layer in range(DEPTH):
        kind = layer % 3
        g_mix = norm_mix[layer].astype(F32)
        w1, w2 = _ffn_weights(ffn_w_in[layer], ffn_w_out[layer])
        if kind == 0:
            i = i_ssd
            i_ssd += 1
            w_in = ssm_w_in[i].astype(BF16)
            weights = [w_in[:, :D_INNER], w_in[:, D_INNER:D_INNER + CONV_DIM],
                       jnp.pad(w_in[:, D_INNER + CONV_DIM:], ((0, 0), (0, LANES - SSM_HEADS)))]
            wo = ssm_w_out[i].astype(BF16)
            mixed = []
            for x, bt, L, tm, cst, sst, convs, states in (
                    (xp, bp, seq, tm_p, jnp.zeros((bp, CONV_W - 1, CONV_DIM), F32),
                     jnp.zeros((bp, SSM_HEADS, SSM_HEAD_DIM, D_STATE), F32), conv_p, ssm_p),
                    (xs, bs, t_new, tm_s, state_ssm_conv[i], state_ssm[i], conv_s, ssm_s)):
                z, xbc, dt = _norm_proj(x, g_mix, weights, [None] * 3, tm)
                xbc3 = xbc.reshape(bt, L, CONV_DIM)
                y, ns = _ssd_mixer(z.reshape(bt, L, D_INNER), xbc3, dt.reshape(bt, L, LANES), cst, sst,
                                   ssm_conv_w[i].astype(F32), ssm_conv_b[i].astype(F32), ssm_dt_bias[i],
                                   ssm_a_log[i], ssm_d[i], ssm_norm[i].astype(F32))
                convs.append(xbc3[:, L - (CONV_W - 1):])
                states.append(ns)
                mixed.append(y.reshape(bt * L, D_INNER))
            ap, as_ = mixed
        elif kind == 1:
            i = i_swa
            i_swa += 1
            w_qkv = swa_w_qkv[i].astype(BF16)
            weights = [w_qkv[:, :SWA_DIM], w_qkv[:, SWA_DIM:2 * SWA_DIM], w_qkv[:, 2 * SWA_DIM:]]
            gains = [jnp.tile(swa_q_norm[i].astype(F32), N_ATTN_HEADS).reshape(1, SWA_DIM),
                     jnp.tile(swa_k_norm[i].astype(F32), N_ATTN_HEADS).reshape(1, SWA_DIM), None]
            wo = swa_w_out[i].astype(BF16)
            q, k, v = _norm_proj(xp, g_mix, weights, gains, tm_p)
            outs, lses = [], []
            for g, (win, dil) in enumerate(SWA_GROUPS):
                o, lse = _swa_prompt_group(q, k, v, g, dil, _swa_prompt_bias(rel_bias, g, dil), bp, seq)
                outs.append(o)
                lses.append(lse)
                keep = min(win, seq)
                sl = slice(g * SWA_GROUP_DIM, (g + 1) * SWA_GROUP_DIM)
                k4 = k.reshape(bp, seq, SWA_DIM)[:, seq - keep:, sl].reshape(bp, keep, SWA_HEADS_PER_GROUP, HEAD_DIM)
                v4 = v.reshape(bp, seq, SWA_DIM)[:, seq - keep:, sl].reshape(bp, keep, SWA_HEADS_PER_GROUP, HEAD_DIM)
                swa_p[g].append(jnp.stack([k4, v4], axis=2))
            ap = _swa_combine(outs, lses, tm_p)
            q, k, v = _norm_proj(xs, g_mix, weights, gains, tm_s)
            q3, k3, v3 = (t.reshape(bs, t_new, SWA_DIM) for t in (q, k, v))
            bufs = [c[i] for c in swa_caches]
            as_ = _swa_sample(q3, k3, v3, bufs, rel_bias, bs, t_new)
            for g, (win, dil) in enumerate(SWA_GROUPS):
                sl = slice(g * SWA_GROUP_DIM, (g + 1) * SWA_GROUP_DIM)
                kv_new = jnp.stack([k3[:, :, sl].reshape(bs, t_new, SWA_HEADS_PER_GROUP, HEAD_DIM),
                                    v3[:, :, sl].reshape(bs, t_new, SWA_HEADS_PER_GROUP, HEAD_DIM)], axis=2)
                lb = bufs[g].shape[1]
                keep = min(win, lb + t_new)
                swa_s[g].append(jnp.concatenate([bufs[g], kv_new.astype(bufs[g].dtype)], axis=1)[:, lb + t_new - keep:])
        else:
            i = i_diff
            i_diff += 1
            lam_init = 0.8 - 0.6 * math.exp(-0.3 * layer)
            w_qkv = diff_w_qkv[i].astype(BF16)
            weights = [w_qkv[:, :DIFF_Q_DIM], w_qkv[:, DIFF_Q_DIM:DIFF_Q_DIM + DIFF_K_DIM],
                       w_qkv[:, DIFF_Q_DIM + DIFF_K_DIM:]]
            gains = [jnp.tile(diff_q_norm[i].astype(F32), DIFF_Q_DIM // HEAD_DIM).reshape(1, DIFF_Q_DIM),
                     jnp.tile(diff_k_norm[i].astype(F32), DIFF_K_DIM // HEAD_DIM).reshape(1, DIFF_K_DIM), None]
            wo = diff_w_out[i].astype(BF16)
            q, k, v = _norm_proj(xp, g_mix, weights, gains, tm_p)
            ap = _diff_prompt(q, k, v, rel_bias, diff_lambda[i], diff_out_norm[i], lam_init, bp, seq)
            dk_p.append(k.reshape(bp, seq, DIFF_KV_HEADS, 2, HEAD_DIM))
            dv_p.append(v.reshape(bp, seq, DIFF_KV_HEADS, DIFF_V_DIM))
            q, k, v = _norm_proj(xs, g_mix, weights, gains, tm_s)
            as_ = _diff_sample(q, k, v, cache_diff_k[i], cache_diff_v[i], page_table, rel_bias, diff_lambda[i],
                               diff_out_norm[i], lam_init, bs, t_new)
            dk_s.append(k.reshape(bs, t_new, DIFF_KV_HEADS, 2, HEAD_DIM))
            dv_s.append(v.reshape(bs, t_new, DIFF_KV_HEADS, DIFF_V_DIM))
        g_ffn = norm_ffn[layer].astype(F32)
        xp = _mix_ffn(xp, ap, wo, g_ffn, w1, w2, tm_p)
        xs = _mix_ffn(xs, as_, wo, g_ffn, w1, w2, tm_s)
    return (xp.reshape(bp, seq, d), xs.reshape(bs, t_new, d),
            jnp.stack(conv_p), jnp.stack(conv_s), jnp.stack(ssm_p), jnp.stack(ssm_s),
            jnp.stack(swa_p[0]), jnp.stack(swa_s[0]), jnp.stack(swa_p[1]), jnp.stack(swa_s[1]),
            jnp.stack(swa_p[2]), jnp.stack(swa_s[2]),
            jnp.stack(dk_p), jnp.stack(dk_s), jnp.stack(dv_p), jnp.stack(dv_s))
```

```python
import functools
import math

import jax
import jax.numpy as jnp
from jax import lax
from jax.experimental import pallas as pl
from jax.experimental.pallas import tpu as pltpu

F32 = jnp.float32
BF16 = jnp.bfloat16

D_MODEL = 1024
DEPTH = 4
D_FF = 2816
D_INNER = 2048
SSM_HEADS = 32
SSM_HEAD_DIM = 64
SSM_GROUPS = 4
D_STATE = 128
CONV_W = 4
CONV_DIM = D_INNER + 2 * SSM_GROUPS * D_STATE
SSD_CHUNK = 128
HEAD_DIM = 64
N_ATTN_HEADS = 12
NUM_BUCKETS = 32
MAX_DISTANCE = 2048
SWA_GROUPS = ((128, 1), (512, 4), (2048, 16))
SWA_HEADS_PER_GROUP = 4
SWA_GROUP_DIM = SWA_HEADS_PER_GROUP * HEAD_DIM
SWA_DIM = N_ATTN_HEADS * HEAD_DIM
SWA_STEPS = 128
DIFF_KV_HEADS = 4
DIFF_REP = 3
DIFF_V_DIM = 128
DIFF_Q_DIM = N_ATTN_HEADS * 2 * HEAD_DIM
DIFF_K_DIM = DIFF_KV_HEADS * 2 * HEAD_DIM
DIFF_V_ALL = DIFF_KV_HEADS * DIFF_V_DIM
PAGE_SIZE = 128
EPS = 1e-6
NEG = -1e30

LANES = 128
FFN_CHUNK = 256
FFN_NCHUNK = D_FF // FFN_CHUNK
VMEM_LIMIT = 56 * 1024 * 1024
PAGES_PER_STEP = 8


def _dot(a, b):
    return jnp.dot(a, b, preferred_element_type=F32)


def _dot_nt(a, b):
    return lax.dot_general(a, b, (((1,), (1,)), ((), ())), preferred_element_type=F32)


def _split3(x):
    hi = x.astype(BF16)
    r = x - hi.astype(F32)
    mid = r.astype(BF16)
    lo = (r - mid.astype(F32)).astype(BF16)
    return hi, mid, lo


def _dot_exact_rhs(a_bf16, x):
    hi, mid, lo = _split3(x)
    return _dot(a_bf16, hi) + _dot(a_bf16, mid) + _dot(a_bf16, lo)


def _dot_exact_lhs(x, a_bf16):
    hi, mid, lo = _split3(x)
    return _dot(hi, a_bf16) + _dot(mid, a_bf16) + _dot(lo, a_bf16)


def _sigmoid(x):
    return 1.0 / (1.0 + jnp.exp(-x))


def _rms(x, g):
    return x * lax.rsqrt(jnp.mean(x * x, axis=-1, keepdims=True) + EPS) * g


def _resident(shape):
    n = len(shape)
    return pl.BlockSpec(shape, lambda *_: (0,) * n, pipeline_mode=pl.Buffered(1))


def _params(sem):
    return pltpu.CompilerParams(dimension_semantics=sem, vmem_limit_bytes=VMEM_LIMIT)


def _norm_proj_kernel(*refs, n_out, head_norm):
    x_ref, g_ref = refs[0], refs[1]
    w_refs = refs[2:2 + n_out]
    n_hn = sum(head_norm)
    hn_refs = refs[2 + n_out:2 + n_out + n_hn]
    pos = 2 + n_out + n_hn
    seg_ref = refs[pos] if n_hn else None
    pos += 1 if n_hn else 0
    o_refs = refs[pos:pos + n_out]
    h = _rms(x_ref[...], g_ref[...]).astype(BF16)
    k = 0
    for i in range(n_out):
        y = _dot(h, w_refs[i][...])
        if head_norm[i]:
            gain = hn_refs[k][...]
            k += 1
            seg = seg_ref[...]
            parts = []
            for c in range(y.shape[1] // seg.shape[0]):
                yc = y[:, c * seg.shape[0]:(c + 1) * seg.shape[0]]
                sq = yc * yc
                hi = sq.astype(BF16)
                lo = (sq - hi.astype(F32)).astype(BF16)
                ms = (_dot(hi, seg) + _dot(lo, seg)) * (1.0 / HEAD_DIM)
                parts.append(yc * lax.rsqrt(ms + EPS))
            y = jnp.concatenate(parts, axis=1) * gain
        o_refs[i][...] = y


def _norm_proj(x, g, weights, head_gains, tm):
    m, d = x.shape
    n_out = len(weights)
    head_norm = tuple(hg is not None for hg in head_gains)
    ins = [x, g.reshape(1, d)] + list(weights)
    specs = [pl.BlockSpec((tm, d), lambda i: (i, 0)), _resident((1, d))]
    specs += [_resident(w.shape) for w in weights]
    for hg in head_gains:
        if hg is not None:
            ins.append(hg)
            specs.append(_resident(hg.shape))
    if any(head_norm):
        r = jnp.arange(2 * LANES)
        seg = (r[:, None] // HEAD_DIM == r[None, :] // HEAD_DIM).astype(BF16)
        ins.append(seg)
        specs.append(_resident(seg.shape))
    return pl.pallas_call(
        functools.partial(_norm_proj_kernel, n_out=n_out, head_norm=head_norm),
        grid=(m // tm,),
        in_specs=specs,
        out_specs=[pl.BlockSpec((tm, w.shape[1]), lambda i: (i, 0)) for w in weights],
        out_shape=[jax.ShapeDtypeStruct((m, w.shape[1]), F32) for w in weights],
        compiler_params=_params(("parallel",)),
        name="norm_proj",
    )(*ins)


def _mix_ffn_kernel(x_ref, a_ref, wo_ref, g_ref, w1_ref, w2_ref, o_ref, acc_ref):
    x1 = x_ref[...] + _dot(a_ref[...], wo_ref[...])
    h = _rms(x1, g_ref[...]).astype(BF16)
    acc_ref[...] = x1

    def body(c, carry):
        gate = _dot(h, w1_ref[0, c])
        up = _dot(h, w1_ref[1, c])
        act = (gate * _sigmoid(gate) * up).astype(BF16)
        acc_ref[...] += _dot(act, w2_ref[c])
        return carry

    lax.fori_loop(0, FFN_NCHUNK, body, 0)
    o_ref[...] = acc_ref[...]


def _mix_ffn(x, a, wo, g, w1, w2, tm):
    m, d = x.shape
    ka = a.shape[1]
    return pl.pallas_call(
        _mix_ffn_kernel,
        grid=(m // tm,),
        in_specs=[pl.BlockSpec((tm, d), lambda i: (i, 0)),
                  pl.BlockSpec((tm, ka), lambda i: (i, 0)),
                  _resident(wo.shape), _resident((1, d)), _resident(w1.shape), _resident(w2.shape)],
        out_specs=pl.BlockSpec((tm, d), lambda i: (i, 0)),
        out_shape=jax.ShapeDtypeStruct((m, d), F32),
        scratch_shapes=[pltpu.VMEM((tm, d), F32)],
        compiler_params=_params(("parallel",)),
        name="mix_ffn",
    )(x, a, wo, g.reshape(1, d), w1, w2)


def _ssd_kernel(z_ref, xbc_ref, dt_ref, cst_ref, sst_ref, cw_ref, cb_ref, dtb_ref, alog_ref,
                dx_ref, ng_ref, e_ref, y_ref, ns_ref, ext_ref, st_ref, yacc_ref, *pad_refs,
                q, n_valid):
    c = pl.program_id(1)
    last = pl.num_programs(1) - 1
    padded = n_valid < q

    @pl.when(c == 0)
    def _():
        ext_ref[...] = jnp.zeros(ext_ref.shape, F32)
        ext_ref[5:8, :] = cst_ref[0]
        st_ref[...] = sst_ref[0]
        if padded:
            for r in pad_refs:
                r[...] = jnp.zeros(r.shape, F32)

    ext_ref[8:8 + n_valid, :] = xbc_ref[0]
    conv = cb_ref[...]
    for k in range(CONV_W):
        conv = conv + ext_ref[5 + k:5 + k + q, :] * cw_ref[k:k + 1, :]
    if not padded:
        ext_ref[5:8, :] = ext_ref[q + 5:q + 8, :]
    act = conv * _sigmoid(conv)
    xs = act[:, :D_INNER]
    bm = act[:, D_INNER:D_INNER + SSM_GROUPS * D_STATE].astype(BF16)
    cm = act[:, D_INNER + SSM_GROUPS * D_STATE:].astype(BF16)

    if padded:
        zpad_ref, dtpad_ref = pad_refs
        zpad_ref[0:n_valid, :] = z_ref[0]
        dtpad_ref[0:n_valid, :] = dt_ref[0]
        z = zpad_ref[...]
        dt_raw = dtpad_ref[...]
    else:
        z = z_ref[0]
        dt_raw = dt_ref[0]

    row = lax.broadcasted_iota(jnp.int32, (q, q), 0)
    col = lax.broadcasted_iota(jnp.int32, (q, q), 1)
    causal = col <= row
    tri = jnp.where(causal, 1.0, 0.0).astype(BF16)
    tri_t = jnp.where(row <= col, 1.0, 0.0).astype(BF16)

    xdt_in = dt_raw + dtb_ref[...]
    dt = jnp.maximum(xdt_in, 0.0) + jnp.log(1.0 + jnp.exp(-jnp.abs(xdt_in)))
    if padded:
        dt = jnp.where(lax.broadcasted_iota(jnp.int32, dt.shape, 0) < n_valid, dt, 0.0)
    adt = dt * (-jnp.exp(alog_ref[...]))
    acum = _dot_exact_rhs(tri, adt)
    acum_t = _dot_exact_lhs(adt.T, tri_t)
    e = e_ref[...]
    dt_x = _dot_exact_lhs(dt, e)
    acum_x = _dot_exact_lhs(acum, e)
    exp_acum_x = jnp.exp(acum_x)
    acum_last_x = acum_x[q - 1:q, :]
    xdt = xs * dt_x
    xdt_b = xdt.astype(BF16)
    xd = xdt * jnp.exp(acum_last_x - acum_x)
    dtot_x = jnp.exp(acum_last_x)
    lane_lo = lax.broadcasted_iota(jnp.int32, (q, LANES), 1) < SSM_HEAD_DIM

    gw = SSM_HEADS // SSM_GROUPS * SSM_HEAD_DIM
    for g in range(SSM_GROUPS):
        cg = cm[:, g * D_STATE:(g + 1) * D_STATE]
        bg = bm[:, g * D_STATE:(g + 1) * D_STATE]
        cb = _dot_nt(cg, bg)
        st_g = st_ref[g * gw:(g + 1) * gw, :]
        yoff = _dot_nt(cg, st_g.astype(BF16))
        yacc_ref[:, g * gw:(g + 1) * gw] = yoff * exp_acum_x[:, g * gw:(g + 1) * gw]
        for pair in range(gw // LANES):
            h0 = g * (gw // SSM_HEAD_DIM) + 2 * pair
            xp = xdt_b[:, h0 * SSM_HEAD_DIM:h0 * SSM_HEAD_DIM + LANES]
            ys = []
            for h in (h0, h0 + 1):
                seg = jnp.broadcast_to(acum[:, h:h + 1], (q, q)) - jnp.broadcast_to(acum_t[h:h + 1, :], (q, q))
                w = cb * jnp.exp(jnp.where(causal, seg, NEG))
                ys.append(_dot(w.astype(BF16), xp))
            yacc_ref[:, h0 * SSM_HEAD_DIM:h0 * SSM_HEAD_DIM + LANES] += jnp.where(lane_lo, ys[0], ys[1])
        xd_t = jnp.concatenate(
            [xd[:, g * gw + t * LANES:g * gw + (t + 1) * LANES].T for t in range(gw // LANES)], axis=0)
        upd = _dot(xd_t.astype(BF16), bg)
        dtot = jnp.concatenate(
            [jnp.broadcast_to(dtot_x[:, g * gw + t * LANES:g * gw + (t + 1) * LANES], (LANES, LANES)).T
             for t in range(gw // LANES)], axis=0)
        st_ref[g * gw:(g + 1) * gw, :] = dtot * st_g + upd

    y = yacc_ref[...] + xs * dx_ref[...]
    gated = y * (z * _sigmoid(z))
    outs = []
    for g in range(SSM_GROUPS):
        gg = gated[:, g * gw:(g + 1) * gw]
        outs.append(gg * lax.rsqrt(jnp.mean(gg * gg, axis=-1, keepdims=True) + EPS))
    yn = jnp.concatenate(outs, axis=1) * ng_ref[...]
    y_ref[0] = yn[0:n_valid].astype(BF16)

    @pl.when(c == last)
    def _():
        ns_ref[0] = st_ref[...]


def _ssd_mixer(z, xbc, dt, conv_state, ssm_state, conv_w, conv_b, dt_bias, a_log, d_skip, norm_g):
    bt, seq, _ = z.shape
    q = SSD_CHUNK
    n_valid = min(q, seq)
    nc = max(1, seq // q)
    pad = lambda v: jnp.pad(v.astype(F32), (0, LANES - SSM_HEADS)).reshape(1, LANES)
    hh = jnp.arange(LANES)[:, None]
    cc = jnp.arange(D_INNER)[None, :] // SSM_HEAD_DIM
    expand = (hh == cc).astype(BF16)
    dx = jnp.repeat(d_skip.astype(F32), SSM_HEAD_DIM).reshape(1, D_INNER)
    sst = ssm_state.reshape(bt, SSM_HEADS * SSM_HEAD_DIM, D_STATE)
    scratch = [pltpu.VMEM((q + 8, CONV_DIM), F32),
               pltpu.VMEM((SSM_HEADS * SSM_HEAD_DIM, D_STATE), F32),
               pltpu.VMEM((q, D_INNER), F32)]
    if n_valid < q:
        scratch += [pltpu.VMEM((q, D_INNER), F32), pltpu.VMEM((q, LANES), F32)]
    y, ns = pl.pallas_call(
        functools.partial(_ssd_kernel, q=q, n_valid=n_valid),
        grid=(bt, nc),
        in_specs=[pl.BlockSpec((1, n_valid, D_INNER), lambda b, c: (b, c, 0)),
                  pl.BlockSpec((1, n_valid, CONV_DIM), lambda b, c: (b, c, 0)),
                  pl.BlockSpec((1, n_valid, LANES), lambda b, c: (b, c, 0)),
                  pl.BlockSpec((1, CONV_W - 1, CONV_DIM), lambda b, c: (b, 0, 0)),
                  pl.BlockSpec((1, SSM_HEADS * SSM_HEAD_DIM, D_STATE), lambda b, c: (b, 0, 0)),
                  _resident((CONV_W, CONV_DIM)), _resident((1, CONV_DIM)),
                  _resident((1, LANES)), _resident((1, LANES)),
                  _resident((1, D_INNER)), _resident((1, D_INNER)), _resident((LANES, D_INNER))],
        out_specs=[pl.BlockSpec((1, n_valid, D_INNER), lambda b, c: (b, c, 0)),
                   pl.BlockSpec((1, SSM_HEADS * SSM_HEAD_DIM, D_STATE), lambda b, c: (b, 0, 0))],
        out_shape=[jax.ShapeDtypeStruct((bt, seq, D_INNER), BF16),
                   jax.ShapeDtypeStruct((bt, SSM_HEADS * SSM_HEAD_DIM, D_STATE), F32)],
        scratch_shapes=scratch,
        compiler_params=_params(("parallel", "arbitrary")),
        name="ssd_mixer",
    )(z, xbc, dt, conv_state, sst, conv_w, conv_b.reshape(1, CONV_DIM), pad(dt_bias), pad(a_log),
      dx, norm_g.reshape(1, D_INNER), expand)
    return y, ns.reshape(bt, SSM_HEADS, SSM_HEAD_DIM, D_STATE)


def _t5_bucket(dist):
    max_exact = NUM_BUCKETS // 2
    d = jnp.maximum(dist, 0)
    df = jnp.maximum(d, 1).astype(F32)
    large = max_exact + (jnp.log(df / max_exact) / math.log(MAX_DISTANCE / max_exact)
                         * (NUM_BUCKETS - max_exact)).astype(jnp.int32)
    return jnp.where(d < max_exact, d, jnp.minimum(large, NUM_BUCKETS - 1))


def _bias_lookup(tab, dist):
    onehot = jax.nn.one_hot(_t5_bucket(dist), NUM_BUCKETS, dtype=F32)
    return jnp.einsum('...b,bh->...h', onehot, tab.astype(F32), precision=lax.Precision.HIGHEST)


def _head_lanes(t):
    return jnp.repeat(t, HEAD_DIM, axis=-1)


def _swa_prompt_kernel(q_ref, kp_ref, kc_ref, vp_ref, vc_ref, b_ref, o_ref, l_ref):
    blk = q_ref.shape[1]
    qv = q_ref[0] * (HEAD_DIM ** -0.5)
    kk = jnp.concatenate([kp_ref[0], kc_ref[0]], axis=0).astype(BF16)
    vv = jnp.concatenate([vp_ref[0], vc_ref[0]], axis=0).astype(BF16)
    lane = lax.broadcasted_iota(jnp.int32, (blk, SWA_GROUP_DIM), 1) // HEAD_DIM
    o = jnp.zeros((blk, SWA_GROUP_DIM), F32)
    lse = jnp.zeros((blk, SWA_GROUP_DIM), F32)
    for h in range(SWA_HEADS_PER_GROUP):
        mine = lane == h
        qh = jnp.where(mine, qv, 0.0).astype(BF16)
        s = _dot_nt(qh, kk) + b_ref[0, h]
        m = jnp.max(s, axis=-1, keepdims=True)
        p = jnp.exp(s - m)
        l = jnp.sum(p, axis=-1, keepdims=True)
        oh = _dot(p.astype(BF16), vv) / l
        o = jnp.where(mine, oh, o)
        lse = jnp.where(mine, m + jnp.log(l), lse)
    o_ref[0] = o
    l_ref[0] = lse


def _swa_prompt_group(q, k, v, g, dil, bias, bt, seq):
    nb = seq // (dil * SWA_STEPS)
    ncol = SWA_DIM // SWA_GROUP_DIM
    qv, kv, vv = (t.reshape(bt, seq // dil, dil * SWA_DIM) for t in (q, k, v))
    blk = (1, SWA_STEPS, SWA_GROUP_DIM)
    cur = pl.BlockSpec(blk, lambda b, c, n: (b, n, c * ncol + g))
    prev = pl.BlockSpec(blk, lambda b, c, n: (b, jnp.maximum(n - 1, 0), c * ncol + g))
    o, lse = pl.pallas_call(
        _swa_prompt_kernel,
        grid=(bt, dil, nb),
        in_specs=[cur, prev, cur, prev, cur,
                  pl.BlockSpec((1, SWA_HEADS_PER_GROUP, SWA_STEPS, 2 * SWA_STEPS),
                               lambda b, c, n: (jnp.minimum(n, 1), 0, 0, 0))],
        out_specs=[pl.BlockSpec(blk, lambda b, c, n: (b, n, c)),
                   pl.BlockSpec(blk, lambda b, c, n: (b, n, c))],
        out_shape=[jax.ShapeDtypeStruct((bt, seq // dil, dil * SWA_GROUP_DIM), F32)] * 2,
        compiler_params=_params(("parallel", "parallel", "parallel")),
        name="swa_prompt",
    )(qv, kv, kv, vv, vv, bias)
    return o.reshape(bt * seq, SWA_GROUP_DIM), lse.reshape(bt * seq, SWA_GROUP_DIM)


def _swa_prompt_bias(rel_bias, g, dil):
    i = jnp.arange(SWA_STEPS)[:, None]
    j = jnp.arange(2 * SWA_STEPS)[None, :]
    step = SWA_STEPS + i - j
    tab = rel_bias[:, g * SWA_HEADS_PER_GROUP:(g + 1) * SWA_HEADS_PER_GROUP].astype(F32)
    bias = jnp.transpose(_bias_lookup(tab, jnp.clip(step, 0, SWA_STEPS) * dil), (2, 0, 1))
    valid = (step >= 0) & (step <= SWA_STEPS)
    first = valid & (j >= SWA_STEPS)
    return jnp.stack([jnp.where(first, bias, NEG), jnp.where(valid, bias, NEG)])


def _swa_combine_kernel(o0, o1, o2, l0, l1, l2, a_ref):
    ls = (l0[...], l1[...], l2[...])
    m = jnp.maximum(jnp.maximum(ls[0], ls[1]), ls[2])
    es = [jnp.exp(l - m) for l in ls]
    inv = 1.0 / (es[0] + es[1] + es[2])
    for g, o in enumerate((o0, o1, o2)):
        a_ref[:, g * SWA_GROUP_DIM:(g + 1) * SWA_GROUP_DIM] = (o[...] * es[g] * inv).astype(BF16)


def _swa_combine(outs, lses, tm):
    m = outs[0].shape[0]
    spec = pl.BlockSpec((tm, SWA_GROUP_DIM), lambda i: (i, 0))
    return pl.pallas_call(
        _swa_combine_kernel,
        grid=(m // tm,),
        in_specs=[spec] * 6,
        out_specs=pl.BlockSpec((tm, SWA_DIM), lambda i: (i, 0)),
        out_shape=jax.ShapeDtypeStruct((m, SWA_DIM), BF16),
        compiler_params=_params(("parallel",)),
        name="swa_combine",
    )(*outs, *lses)


def _swa_sample_kernel(q_ref, kn_ref, vn_ref, c0_ref, c1_ref, c2_ref, b_ref, nb_ref, seg_ref, a_ref, *, t_new):
    seg = seg_ref[...]
    gd = SWA_GROUP_DIM

    def seg_sum(x):
        hi = x.astype(BF16)
        lo = (x - hi.astype(F32)).astype(BF16)
        return _dot(hi, seg) + _dot(lo, seg)

    caches = (c0_ref, c1_ref, c2_ref)
    outs, lses = [], []
    for g, (_, dil) in enumerate(SWA_GROUPS):
        kn = kn_ref[0, :, g * gd:(g + 1) * gd]
        vn = vn_ref[0, :, g * gd:(g + 1) * gd]
        og, lg = [], []
        for t in range(t_new):
            off = 0 if dil == 1 else t * 2 * gd
            kt = caches[g][0, :, off:off + gd]
            vt = caches[g][0, :, off + gd:off + 2 * gd]
            qt = q_ref[0, t:t + 1, g * gd:(g + 1) * gd] * (HEAD_DIM ** -0.5)
            s = seg_sum(kt * qt) + b_ref[g, t]
            sn = seg_sum(kn * qt) + nb_ref[g, t]
            m = jnp.maximum(jnp.max(s, axis=0, keepdims=True), jnp.max(sn, axis=0, keepdims=True))
            p = jnp.exp(s - m)
            pn = jnp.exp(sn - m)
            l = jnp.sum(p, axis=0, keepdims=True) + jnp.sum(pn, axis=0, keepdims=True)
            o = (jnp.sum(p * vt, axis=0, keepdims=True) + jnp.sum(pn * vn, axis=0, keepdims=True)) / l
            og.append(o)
            lg.append(m + jnp.log(l))
        outs.append(jnp.concatenate(og, axis=0))
        lses.append(jnp.concatenate(lg, axis=0))
    m = jnp.maximum(jnp.maximum(lses[0], lses[1]), lses[2])
    es = [jnp.exp(l - m) for l in lses]
    inv = 1.0 / (es[0] + es[1] + es[2])
    for g in range(len(SWA_GROUPS)):
        a_ref[0, :, g * gd:(g + 1) * gd] = (outs[g] * es[g] * inv).astype(BF16)


def _swa_sample_bias(rel_bias, t_new):
    m = jnp.arange(SWA_STEPS)
    t = jnp.arange(t_new)
    tiles, new_tiles = [], []
    for g, (win, dil) in enumerate(SWA_GROUPS):
        tab = rel_bias[:, g * SWA_HEADS_PER_GROUP:(g + 1) * SWA_HEADS_PER_GROUP].astype(F32)
        if dil == 1:
            dist = SWA_STEPS + t[:, None] - m[None, :]
            ok = dist <= SWA_STEPS
            nd = t[:, None] - t[None, :]
            nok = nd >= 0
        else:
            dist = jnp.broadcast_to(((SWA_STEPS - m) * dil)[None, :], (t_new, SWA_STEPS))
            ok = jnp.ones_like(dist, bool)
            nd = jnp.zeros((t_new, t_new), jnp.int32)
            nok = t[:, None] == t[None, :]
        tiles.append(jnp.where(ok[..., None], _head_lanes(_bias_lookup(tab, dist)), NEG))
        new_tiles.append(jnp.where(nok[..., None], _head_lanes(_bias_lookup(tab, nd)), NEG))
    return jnp.stack(tiles), jnp.stack(new_tiles)


def _swa_sample(q, k, v, caches, rel_bias, bt, t_new):
    bias, new_bias = _swa_sample_bias(rel_bias, t_new)
    r = jnp.arange(2 * LANES)
    seg = (r[:, None] // HEAD_DIM == r[None, :] // HEAD_DIM).astype(BF16)
    row = 2 * SWA_GROUP_DIM
    views, specs = [], []
    for g, (win, dil) in enumerate(SWA_GROUPS):
        views.append(caches[g].reshape(bt, win // dil, dil * row))
        width = min(dil, t_new) * row
        specs.append(pl.BlockSpec((1, SWA_STEPS, width), lambda b: (b, 0, 0)))
    tok = pl.BlockSpec((1, t_new, SWA_DIM), lambda b: (b, 0, 0))
    a = pl.pallas_call(
        functools.partial(_swa_sample_kernel, t_new=t_new),
        grid=(bt,),
        in_specs=[tok, tok, tok] + specs + [_resident(bias.shape), _resident(new_bias.shape), _resident(seg.shape)],
        out_specs=tok,
        out_shape=jax.ShapeDtypeStruct((bt, t_new, SWA_DIM), BF16),
        compiler_params=_params(("parallel",)),
        name="swa_sample",
    )(q, k, v, *views, bias, new_bias, seg)
    return a.reshape(bt * t_new, SWA_DIM)


def _diff_lambda(lam_ref, lam_init):
    lp = lam_ref[...]
    s1 = jnp.sum(lp[0:1] * lp[1:2], axis=-1, keepdims=True)
    s2 = jnp.sum(lp[2:3] * lp[3:4], axis=-1, keepdims=True)
    return jnp.exp(s1) - jnp.exp(s2) + lam_init


def _diff_prompt_kernel(q_ref, k_ref, v_ref, b_ref, lam_ref, on_ref, o_ref, q6_ref, m_ref, acc_ref,
                        *, tq, tk, lam_init):
    i = pl.program_id(2)
    ratio = tk // tq
    lane_lo = lax.broadcasted_iota(jnp.int32, (tq, LANES), 1) < HEAD_DIM
    for r in range(DIFF_REP):
        qr = q_ref[0, :, r * LANES:(r + 1) * LANES] * (HEAD_DIM ** -0.5)
        q6_ref[(2 * r) * tq:(2 * r + 1) * tq, :] = jnp.where(lane_lo, qr, 0.0).astype(BF16)
        q6_ref[(2 * r + 1) * tq:(2 * r + 2) * tq, :] = jnp.where(lane_lo, 0.0, qr).astype(BF16)
    m_ref[...] = jnp.full(m_ref.shape, NEG, F32)
    acc_ref[...] = jnp.zeros(acc_ref.shape, F32)
    ones = jnp.ones((tk, LANES), BF16)

    def body(j, carry):
        start = pl.multiple_of(j * tk, tk)
        kb = k_ref[0, pl.ds(start, tk), :].astype(BF16)
        vx = jnp.concatenate([v_ref[0, pl.ds(start, tk), :].astype(BF16), ones], axis=1)
        d0 = i - j * ratio + (ratio - 1)
        for r in range(DIFF_REP):
            rows = slice(2 * r * tq, (2 * r + 2) * tq)
            s = _dot_nt(q6_ref[rows, :], kb)
            bias = jnp.concatenate([b_ref[0, d0 - c, r] for c in range(ratio)], axis=1)
            s = s + jnp.concatenate([bias, bias], axis=0)
            smax = s[:, 0:LANES]
            for c in range(1, tk // LANES):
                smax = jnp.maximum(smax, s[:, c * LANES:(c + 1) * LANES])
            m_old = m_ref[rows, :]
            m_new = jnp.maximum(m_old, jnp.max(smax, axis=-1, keepdims=True))
            alpha = jnp.exp(m_old - m_new)
            p = jnp.concatenate([jnp.exp(s[:, c * LANES:(c + 1) * LANES] - m_new)
                                 for c in range(tk // LANES)], axis=1).astype(BF16)
            acc_ref[rows, :] = jnp.concatenate([alpha, alpha], axis=1) * acc_ref[rows, :] + _dot(p, vx)
            m_ref[rows, :] = m_new
        return carry

    lax.fori_loop(0, i // ratio + 1, body, 0)
    lam = _diff_lambda(lam_ref, lam_init)
    for r in range(DIFF_REP):
        o0 = acc_ref[(2 * r) * tq:(2 * r + 1) * tq, :]
        o1 = acc_ref[(2 * r + 1) * tq:(2 * r + 2) * tq, :]
        a = o0[:, :DIFF_V_DIM] / o0[:, DIFF_V_DIM:] - lam * (o1[:, :DIFF_V_DIM] / o1[:, DIFF_V_DIM:])
        o_ref[0, :, r * LANES:(r + 1) * LANES] = (_rms(a, on_ref[...]) * (1.0 - lam_init)).astype(BF16)


def _diff_prompt_bias(rel_bias, seq, tq, ratio):
    nb = seq // tq
    delta = jnp.arange(-(ratio - 1), nb)
    d = delta[:, None, None] * tq + jnp.arange(tq)[None, :, None] - jnp.arange(tq)[None, None, :]
    tab = jnp.where((d >= 0)[..., None], _bias_lookup(rel_bias, d), NEG)
    tab = jnp.transpose(tab, (3, 0, 1, 2)).reshape(DIFF_KV_HEADS, DIFF_REP, nb + ratio - 1, tq, tq)
    return jnp.transpose(tab, (0, 2, 1, 3, 4))


def _diff_prompt(q, k, v, rel_bias, lam_p, out_norm, lam_init, bt, seq):
    tq, tk = 128, 512
    nb = seq // tq
    ratio = tk // tq
    bias = _diff_prompt_bias(rel_bias, seq, tq, ratio)
    qw = DIFF_REP * 2 * HEAD_DIM
    a = pl.pallas_call(
        functools.partial(_diff_prompt_kernel, tq=tq, tk=tk, lam_init=lam_init),
        grid=(bt, DIFF_KV_HEADS, nb),
        in_specs=[pl.BlockSpec((1, tq, qw), lambda b, g, i: (b, i, g)),
                  pl.BlockSpec((1, seq, LANES), lambda b, g, i: (b, 0, g)),
                  pl.BlockSpec((1, seq, LANES), lambda b, g, i: (b, 0, g)),
                  pl.BlockSpec((1, nb + ratio - 1, DIFF_REP, tq, tq), lambda b, g, i: (g, 0, 0, 0, 0)),
                  _resident((4, HEAD_DIM)), _resident((1, DIFF_V_DIM))],
        out_specs=pl.BlockSpec((1, tq, qw), lambda b, g, i: (b, i, g)),
        out_shape=jax.ShapeDtypeStruct((bt, seq, DIFF_Q_DIM), BF16),
        scratch_shapes=[pltpu.VMEM((2 * DIFF_REP * tq, LANES), BF16),
                        pltpu.VMEM((2 * DIFF_REP * tq, LANES), F32),
                        pltpu.VMEM((2 * DIFF_REP * tq, 2 * DIFF_V_DIM), F32)],
        compiler_params=_params(("parallel", "parallel", "arbitrary")),
        name="diff_prompt",
    )(q.reshape(bt, seq, DIFF_Q_DIM), k.reshape(bt, seq, DIFF_K_DIM), v.reshape(bt, seq, DIFF_V_ALL),
      bias, lam_p.astype(F32), out_norm.reshape(1, DIFF_V_DIM).astype(F32))
    return a.reshape(bt * seq, DIFF_Q_DIM)


def _diff_sample_kernel(pt_ref, qb_ref, kn_ref, vn_ref, *refs, t_new, lam_init):
    npg = PAGES_PER_STEP
    k_refs = refs[:npg]
    v_refs = refs[npg:2 * npg]
    b_ref, nb_ref, lam_ref, on_ref, o_ref, m_ref, acc_ref = refs[2 * npg:]
    s_idx = pl.program_id(1)
    per_map = DIFF_REP * t_new
    per_g = 2 * per_map

    @pl.when(s_idx == 0)
    def _():
        m_ref[...] = jnp.full(m_ref.shape, NEG, F32)
        acc_ref[...] = jnp.zeros(acc_ref.shape, F32)

    qb = (qb_ref[0] * (HEAD_DIM ** -0.5)).astype(BF16)
    ones = jnp.ones((PAGE_SIZE, LANES), BF16)

    def absorb(kt_refs, vv_refs, biases):
        ss = [_dot(qb, kt[0].astype(BF16)) + b for kt, b in zip(kt_refs, biases)]
        smax = ss[0]
        for s in ss[1:]:
            smax = jnp.maximum(smax, s)
        m_old = m_ref[...]
        m_new = jnp.maximum(m_old, jnp.max(smax, axis=-1, keepdims=True))
        alpha = jnp.exp(m_old - m_new)
        ps = [jnp.exp(s - m_new).astype(BF16) for s in ss]
        for g in range(DIFF_KV_HEADS):
            rows = slice(g * per_g, (g + 1) * per_g)
            new = jnp.concatenate([alpha[rows], alpha[rows]], axis=1) * acc_ref[rows, :]
            for p, vv in zip(ps, vv_refs):
                vg = vv[0, pl.ds(g, PAGE_SIZE, stride=DIFF_KV_HEADS), :].astype(BF16)
                new = new + _dot(p[rows], jnp.concatenate([vg, ones], axis=1))
            acc_ref[rows, :] = new
        m_ref[...] = m_new

    absorb(k_refs, v_refs, [b_ref[pp] for pp in range(npg)])

    @pl.when(s_idx == pl.num_programs(1) - 1)
    def _():
        absorb([kn_ref], [vn_ref], [nb_ref[...]])
        lam = _diff_lambda(lam_ref, lam_init)
        acc = acc_ref[...]
        o = acc[:, :DIFF_V_DIM] / acc[:, DIFF_V_DIM:]
        for g in range(DIFF_KV_HEADS):
            for r in range(DIFF_REP):
                r0 = g * per_g + r * t_new
                blk = o[r0:r0 + t_new] - lam * o[r0 + per_map:r0 + per_map + t_new]
                col = (g * DIFF_REP + r) * DIFF_V_DIM
                o_ref[0, :, col:col + DIFF_V_DIM] = (_rms(blk, on_ref[...]) * (1.0 - lam_init)).astype(BF16)


def _diff_sample(q, k, v, cache_k, cache_v, page_table, rel_bias, lam_p, out_norm, lam_init, bt, t_new):
    n_pages = page_table.shape[1]
    past = n_pages * PAGE_SIZE
    n_phys = cache_k.shape[0]
    nrow = DIFF_KV_HEADS * 2 * DIFF_REP * t_new
    q6 = q.reshape(bt, t_new, DIFF_KV_HEADS, DIFF_REP, 2, HEAD_DIM)
    q6 = jnp.transpose(q6, (0, 2, 4, 3, 1, 5))
    eye_m = jnp.eye(2, dtype=F32)
    eye_g = jnp.eye(DIFF_KV_HEADS, dtype=F32)
    qb = jnp.einsum('bgmrtd,mn,gh->bgmrthnd', q6, eye_m, eye_g).reshape(bt, nrow, DIFF_K_DIM)
    tab = rel_bias.astype(F32)
    tt = jnp.arange(t_new)

    def rows_of(b):
        n = b.shape[1]
        b = jnp.transpose(b, (2, 0, 1)).reshape(DIFF_KV_HEADS, 1, DIFF_REP, t_new, n)
        return jnp.broadcast_to(b, (DIFF_KV_HEADS, 2, DIFF_REP, t_new, n)).reshape(nrow, n)

    dist = past + tt[:, None] - jnp.arange(past)[None, :]
    bias = rows_of(_bias_lookup(tab, dist)).reshape(nrow, n_pages, PAGE_SIZE)
    bias = jnp.transpose(bias, (1, 0, 2))
    nd = tt[:, None] - jnp.arange(PAGE_SIZE)[None, :]
    nbias = rows_of(jnp.where((nd >= 0)[..., None], _bias_lookup(tab, nd), NEG))
    kn = jnp.transpose(k.reshape(bt, t_new, DIFF_K_DIM), (0, 2, 1))
    kn = jnp.pad(kn, ((0, 0), (0, 0), (0, PAGE_SIZE - t_new)))
    vn = jnp.pad(v.reshape(bt, t_new * DIFF_KV_HEADS, DIFF_V_DIM),
                 ((0, 0), (0, (PAGE_SIZE - t_new) * DIFF_KV_HEADS), (0, 0)))
    ck = jnp.transpose(cache_k, (0, 2, 3, 4, 1)).reshape(n_phys, DIFF_K_DIM, PAGE_SIZE)
    cv = cache_v.reshape(n_phys, PAGE_SIZE * DIFF_KV_HEADS, DIFF_V_DIM)
    npg = PAGES_PER_STEP

    def page_spec(pp):
        return pl.BlockSpec((1, DIFF_K_DIM, PAGE_SIZE), lambda b, s, pt: (pt[b, s * npg + pp], 0, 0))

    const = lambda shape: pl.BlockSpec(shape, lambda b, s, pt: (0,) * len(shape))
    grid_spec = pltpu.PrefetchScalarGridSpec(
        num_scalar_prefetch=1,
        grid=(bt, n_pages // npg),
        in_specs=[pl.BlockSpec((1, nrow, DIFF_K_DIM), lambda b, s, pt: (b, 0, 0)),
                  pl.BlockSpec((1, DIFF_K_DIM, PAGE_SIZE), lambda b, s, pt: (b, 0, 0)),
                  pl.BlockSpec((1, DIFF_K_DIM, PAGE_SIZE), lambda b, s, pt: (b, 0, 0))]
                 + [page_spec(pp) for pp in range(npg)] * 2
                 + [pl.BlockSpec((npg, nrow, PAGE_SIZE), lambda b, s, pt: (s, 0, 0)),
                    const((nrow, PAGE_SIZE)), const((4, HEAD_DIM)), const((1, DIFF_V_DIM))],
        out_specs=pl.BlockSpec((1, t_new, DIFF_Q_DIM), lambda b, s, pt: (b, 0, 0)),
        scratch_shapes=[pltpu.VMEM((nrow, LANES), F32), pltpu.VMEM((nrow, 2 * DIFF_V_DIM), F32)],
    )
    a = pl.pallas_call(
        functools.partial(_diff_sample_kernel, t_new=t_new, lam_init=lam_init),
        grid_spec=grid_spec,
        out_shape=jax.ShapeDtypeStruct((bt, t_new, DIFF_Q_DIM), BF16),
        compiler_params=_params(("parallel", "arbitrary")),
        name="diff_sample",
    )(page_table, qb, kn, vn, *([ck] * npg), *([cv] * npg), bias, nbias,
      lam_p.astype(F32), out_norm.reshape(1, DIFF_V_DIM).astype(F32))
    return a.reshape(bt * t_new, DIFF_Q_DIM)


def _ffn_weights(w_in, w_out):
    w1 = w_in.astype(BF16).reshape(D_MODEL, 2, FFN_NCHUNK, FFN_CHUNK)
    w1 = jnp.transpose(w1, (1, 2, 0, 3))
    w2 = w_out.astype(BF16).reshape(FFN_NCHUNK, FFN_CHUNK, D_MODEL)
    return w1, w2


def kernel(x_prompt, x_sample, state_ssm_conv, state_ssm, cache_swa_kv0, cache_swa_kv1, cache_swa_kv2, cache_diff_k, cache_diff_v, page_table, rel_bias, norm_mix, norm_ffn, ffn_w_in, ffn_w_out, ssm_w_in, ssm_conv_w, ssm_conv_b, ssm_dt_bias, ssm_a_log, ssm_d, ssm_norm, ssm_w_out, swa_w_qkv, swa_q_norm, swa_k_norm, swa_w_out, diff_w_qkv, diff_q_norm, diff_k_norm, diff_lambda, diff_out_norm, diff_w_out):
    bp, seq, d = x_prompt.shape
    bs, t_new, _ = x_sample.shape
    mp, ms = bp * seq, bs * t_new
    tm_p, tm_s = 256, ms
    xp = x_prompt.reshape(mp, d)
    xs = x_sample.reshape(ms, d)
    swa_caches = (cache_swa_kv0, cache_swa_kv1, cache_swa_kv2)
    conv_p, conv_s, ssm_p, ssm_s = [], [], [], []
    swa_p = tuple([] for _ in SWA_GROUPS)
    swa_s = tuple([] for _ in SWA_GROUPS)
    dk_p, dk_s, dv_p, dv_s = [], [], [], []
    i_ssd = i_swa = i_diff = 0
    for layer in range(DEPTH):
        kind = layer % 3
        g_mix = norm_mix[layer].astype(F32)
        w1, w2 = _ffn_weights(ffn_w_in[layer], ffn_w_out[layer])
        if kind == 0:
            i = i_ssd
            i_ssd += 1
            w_in = ssm_w_in[i].astype(BF16)
            weights = [w_in[:, :D_INNER], w_in[:, D_INNER:D_INNER + CONV_DIM],
                       jnp.pad(w_in[:, D_INNER + CONV_DIM:], ((0, 0), (0, LANES - SSM_HEADS)))]
            wo = ssm_w_out[i].astype(BF16)
            mixed = []
            for x, bt, L, tm, cst, sst, convs, states in (
                    (xp, bp, seq, tm_p, jnp.zeros((bp, CONV_W - 1, CONV_DIM), F32),
                     jnp.zeros((bp, SSM_HEADS, SSM_HEAD_DIM, D_STATE), F32), conv_p, ssm_p),
                    (xs, bs, t_new, tm_s, state_ssm_conv[i], state_ssm[i], conv_s, ssm_s)):
                z, xbc, dt = _norm_proj(x, g_mix, weights, [None] * 3, tm)
                xbc3 = xbc.reshape(bt, L, CONV_DIM)
                y, ns = _ssd_mixer(z.reshape(bt, L, D_INNER), xbc3, dt.reshape(bt, L, LANES), cst, sst,
                                   ssm_conv_w[i].astype(F32), ssm_conv_b[i].astype(F32), ssm_dt_bias[i],
                                   ssm_a_log[i], ssm_d[i], ssm_norm[i].astype(F32))
                convs.append(xbc3[:, L - (CONV_W - 1):])
                states.append(ns)
                mixed.append(y.reshape(bt * L, D_INNER))
            ap, as_ = mixed
        elif kind == 1:
            i = i_swa
            i_swa += 1
            w_qkv = swa_w_qkv[i].astype(BF16)
            weights = [w_qkv[:, :SWA_DIM], w_qkv[:, SWA_DIM:2 * SWA_DIM], w_qkv[:, 2 * SWA_DIM:]]
            gains = [jnp.tile(swa_q_norm[i].astype(F32), N_ATTN_HEADS).reshape(1, SWA_DIM),
                     jnp.tile(swa_k_norm[i].astype(F32), N_ATTN_HEADS).reshape(1, SWA_DIM), None]
            wo = swa_w_out[i].astype(BF16)
            q, k, v = _norm_proj(xp, g_mix, weights, gains, tm_p)
            outs, lses = [], []
            for g, (win, dil) in enumerate(SWA_GROUPS):
                o, lse = _swa_prompt_group(q, k, v, g, dil, _swa_prompt_bias(rel_bias, g, dil), bp, seq)
                outs.append(o)
                lses.append(lse)
                keep = min(win, seq)
                sl = slice(g * SWA_GROUP_DIM, (g + 1) * SWA_GROUP_DIM)
                k4 = k.reshape(bp, seq, SWA_DIM)[:, seq - keep:, sl].reshape(bp, keep, SWA_HEADS_PER_GROUP, HEAD_DIM)
                v4 = v.reshape(bp, seq, SWA_DIM)[:, seq - keep:, sl].reshape(bp, keep, SWA_HEADS_PER_GROUP, HEAD_DIM)
                swa_p[g].append(jnp.stack([k4, v4], axis=2))
            ap = _swa_combine(outs, lses, tm_p)
            q, k, v = _norm_proj(xs, g_mix, weights, gains, tm_s)
            q3, k3, v3 = (t.reshape(bs, t_new, SWA_DIM) for t in (q, k, v))
            bufs = [c[i] for c in swa_caches]
            as_ = _swa_sample(q3, k3, v3, bufs, rel_bias, bs, t_new)
            for g, (win, dil) in enumerate(SWA_GROUPS):
                sl = slice(g * SWA_GROUP_DIM, (g + 1) * SWA_GROUP_DIM)
                kv_new = jnp.stack([k3[:, :, sl].reshape(bs, t_new, SWA_HEADS_PER_GROUP, HEAD_DIM),
                                    v3[:, :, sl].reshape(bs, t_new, SWA_HEADS_PER_GROUP, HEAD_DIM)], axis=2)
                lb = bufs[g].shape[1]
                keep = min(win, lb + t_new)
                swa_s[g].append(jnp.concatenate([bufs[g], kv_new.astype(bufs[g].dtype)], axis=1)[:, lb + t_new - keep:])
        else:
            i = i_diff
            i_diff += 1
            lam_init = 0.8 - 0.6 * math.exp(-0.3 * layer)
            w_qkv = diff_w_qkv[i].astype(BF16)
            weights = [w_qkv[:, :DIFF_Q_DIM], w_qkv[:, DIFF_Q_DIM:DIFF_Q_DIM + DIFF_K_DIM],
                       w_qkv[:, DIFF_Q_DIM + DIFF_K_DIM:]]
            gains = [jnp.tile(diff_q_norm[i].astype(F32), DIFF_Q_DIM // HEAD_DIM).reshape(1, DIFF_Q_DIM),
                     jnp.tile(diff_k_norm[i].astype(F32), DIFF_K_DIM // HEAD_DIM).reshape(1, DIFF_K_DIM), None]
            wo = diff_w_out[i].astype(BF16)
            q, k, v = _norm_proj(xp, g_mix, weights, gains, tm_p)
            ap = _diff_prompt(q, k, v, rel_bias, diff_lambda[i], diff_out_norm[i], lam_init, bp, seq)
            dk_p.append(k.reshape(bp, seq, DIFF_KV_HEADS, 2, HEAD_DIM))
            dv_p.append(v.reshape(bp, seq, DIFF_KV_HEADS, DIFF_V_DIM))
            q, k, v = _norm_proj(xs, g_mix, weights, gains, tm_s)
            as_ = _diff_sample(q, k, v, cache_diff_k[i], cache_diff_v[i], page_table, rel_bias, diff_lambda[i],
                               diff_out_norm[i], lam_init, bs, t_new)
            dk_s.append(k.reshape(bs, t_new, DIFF_KV_HEADS, 2, HEAD_DIM))
            dv_s.append(v.reshape(bs, t_new, DIFF_KV_HEADS, DIFF_V_DIM))
        g_ffn = norm_ffn[layer].astype(F32)
        xp = _mix_ffn(xp, ap, wo, g_ffn, w1, w2, tm_p)
        xs = _mix_ffn(xs, as_, wo, g_ffn, w1, w2, tm_s)
    return (xp.reshape(bp, seq, d), xs.reshape(bs, t_new, d),
            jnp.stack(conv_p), jnp.stack(conv_s), jnp.stack(ssm_p), jnp.stack(ssm_s),
            jnp.stack(swa_p[0]), jnp.stack(swa_s[0]), jnp.stack(swa_p[1]), jnp.stack(swa_s[1]),
            jnp.stack(swa_p[2]), jnp.stack(swa_s[2]),
            jnp.stack(dk_p), jnp.stack(dk_s), jnp.stack(dv_p), jnp.stack(dv_s))
```

```python
import functools
import math

import jax
import jax.numpy as jnp
from jax import lax
from jax.experimental import pallas as pl
from jax.experimental.pallas import tpu as pltpu

F32 = jnp.float32
BF16 = jnp.bfloat16

D_MODEL = 1024
DEPTH = 4
D_FF = 2816
D_INNER = 2048
SSM_HEADS = 32
SSM_HEAD_DIM = 64
SSM_GROUPS = 4
D_STATE = 128
CONV_W = 4
CONV_DIM = D_INNER + 2 * SSM_GROUPS * D_STATE
SSD_CHUNK = 128
HEAD_DIM = 64
N_ATTN_HEADS = 12
NUM_BUCKETS = 32
MAX_DISTANCE = 2048
SWA_GROUPS = ((128, 1), (512, 4), (2048, 16))
SWA_HEADS_PER_GROUP = 4
SWA_GROUP_DIM = SWA_HEADS_PER_GROUP * HEAD_DIM
SWA_DIM = N_ATTN_HEADS * HEAD_DIM
SWA_STEPS = 128
DIFF_KV_HEADS = 4
DIFF_REP = 3
DIFF_V_DIM = 128
DIFF_Q_DIM = N_ATTN_HEADS * 2 * HEAD_DIM
DIFF_K_DIM = DIFF_KV_HEADS * 2 * HEAD_DIM
DIFF_V_ALL = DIFF_KV_HEADS * DIFF_V_DIM
PAGE_SIZE = 128
EPS = 1e-6
NEG = -1e30

LANES = 128
VMEM_LIMIT = 56 * 1024 * 1024
PAGES_PER_STEP = 8


def _dot(a, b):
    return jnp.dot(a, b, preferred_element_type=F32)


def _dot_nt(a, b):
    return lax.dot_general(a, b, (((1,), (1,)), ((), ())), preferred_element_type=F32)


def _split3(x):
    hi = x.astype(BF16)
    r = x - hi.astype(F32)
    mid = r.astype(BF16)
    lo = (r - mid.astype(F32)).astype(BF16)
    return hi, mid, lo


def _dot_exact_rhs(a_bf16, x):
    hi, mid, lo = _split3(x)
    return _dot(a_bf16, hi) + _dot(a_bf16, mid) + _dot(a_bf16, lo)


def _dot_exact_lhs(x, a_bf16):
    hi, mid, lo = _split3(x)
    return _dot(hi, a_bf16) + _dot(mid, a_bf16) + _dot(lo, a_bf16)


def _sigmoid(x):
    return 1.0 / (1.0 + jnp.exp(-x))


def _rms(x, g):
    return x * lax.rsqrt(jnp.mean(x * x, axis=-1, keepdims=True) + EPS) * g


def _resident(shape):
    n = len(shape)
    return pl.BlockSpec(shape, lambda *_: (0,) * n, pipeline_mode=pl.Buffered(1))


def _params(sem):
    return pltpu.CompilerParams(dimension_semantics=sem, vmem_limit_bytes=VMEM_LIMIT)


def _norm_proj_kernel(*refs, n_out, head_norm, emit):
    x_ref, g_ref = refs[0], refs[1]
    w_refs = refs[2:2 + n_out]
    n_hn = sum(head_norm)
    hn_refs = refs[2 + n_out:2 + n_out + n_hn]
    pos = 2 + n_out + n_hn
    seg_ref = refs[pos] if n_hn else None
    pos += 1 if n_hn else 0
    o_refs = list(refs[pos:])
    h = _rms(x_ref[...], g_ref[...]).astype(BF16)
    k = 0
    for i in range(n_out):
        y = _dot(h, w_refs[i][...])
        if head_norm[i]:
            gain = hn_refs[k][...]
            k += 1
            seg = seg_ref[...]
            parts = []
            for c in range(y.shape[1] // seg.shape[0]):
                yc = y[:, c * seg.shape[0]:(c + 1) * seg.shape[0]]
                sq = yc * yc
                hi = sq.astype(BF16)
                lo = (sq - hi.astype(F32)).astype(BF16)
                ms = (_dot(hi, seg) + _dot(lo, seg)) * (1.0 / HEAD_DIM)
                parts.append(yc * lax.rsqrt(ms + EPS))
            y = jnp.concatenate(parts, axis=1) * gain
        if "n" in emit[i]:
            o_refs.pop(0)[...] = y
        if "t" in emit[i]:
            o_refs.pop(0)[0] = y.T


def _norm_proj(x, g, weights, head_gains, tm, emit=None, seq=None):
    m, d = x.shape
    n_out = len(weights)
    emit = tuple(emit or ("n",) * n_out)
    head_norm = tuple(hg is not None for hg in head_gains)
    ins = [x, g.reshape(1, d)] + list(weights)
    specs = [pl.BlockSpec((tm, d), lambda i: (i, 0)), _resident((1, d))]
    specs += [_resident(w.shape) for w in weights]
    for hg in head_gains:
        if hg is not None:
            ins.append(hg)
            specs.append(_resident(hg.shape))
    if any(head_norm):
        r = jnp.arange(2 * LANES)
        seg = (r[:, None] // HEAD_DIM == r[None, :] // HEAD_DIM).astype(BF16)
        ins.append(seg)
        specs.append(_resident(seg.shape))
    out_specs, out_shape = [], []
    for w, e in zip(weights, emit):
        n = w.shape[1]
        if "n" in e:
            out_specs.append(pl.BlockSpec((tm, n), lambda i: (i, 0)))
            out_shape.append(jax.ShapeDtypeStruct((m, n), F32))
        if "t" in e:
            per_seq = seq // tm
            out_specs.append(pl.BlockSpec((1, n, tm), lambda i: (i // per_seq, 0, i % per_seq)))
            out_shape.append(jax.ShapeDtypeStruct((m // seq, n, seq), F32))
    return pl.pallas_call(
        functools.partial(_norm_proj_kernel, n_out=n_out, head_norm=head_norm, emit=emit),
        grid=(m // tm,),
        in_specs=specs,
        out_specs=out_specs,
        out_shape=out_shape,
        compiler_params=_params(("parallel",)),
        name="norm_proj",
    )(*ins)


def _mix_ffn_kernel(x_ref, a_ref, wo_ref, g_ref, w1_ref, w2_ref, o_ref):
    x1 = x_ref[...] + _dot(a_ref[...], wo_ref[...])
    h = _rms(x1, g_ref[...]).astype(BF16)
    gu = _dot(h, w1_ref[...])
    gate = gu[:, :D_FF]
    act = (gate * _sigmoid(gate) * gu[:, D_FF:]).astype(BF16)
    o_ref[...] = x1 + _dot(act, w2_ref[...])


def _mix_ffn(x, a, wo, g, w1, w2, tm):
    m, d = x.shape
    ka = a.shape[1]
    return pl.pallas_call(
        _mix_ffn_kernel,
        grid=(m // tm,),
        in_specs=[pl.BlockSpec((tm, d), lambda i: (i, 0)),
                  pl.BlockSpec((tm, ka), lambda i: (i, 0)),
                  _resident(wo.shape), _resident((1, d)), _resident(w1.shape), _resident(w2.shape)],
        out_specs=pl.BlockSpec((tm, d), lambda i: (i, 0)),
        out_shape=jax.ShapeDtypeStruct((m, d), F32),
        compiler_params=_params(("parallel",)),
        name="mix_ffn",
    )(x, a, wo, g.reshape(1, d), w1, w2)


def _ssd_kernel(z_ref, xbc_ref, dt_ref, cst_ref, sst_ref, cw_ref, cb_ref, dtb_ref, alog_ref,
                dx_ref, ng_ref, e_ref, y_ref, ns_ref, ext_ref, st_ref, yacc_ref, *pad_refs,
                q, n_valid):
    c = pl.program_id(1)
    last = pl.num_programs(1) - 1
    padded = n_valid < q

    @pl.when(c == 0)
    def _():
        ext_ref[...] = jnp.zeros(ext_ref.shape, F32)
        ext_ref[5:8, :] = cst_ref[0]
        st_ref[...] = sst_ref[0]
        if padded:
            for r in pad_refs:
                r[...] = jnp.zeros(r.shape, F32)

    ext_ref[8:8 + n_valid, :] = xbc_ref[0]
    conv = cb_ref[...]
    for k in range(CONV_W):
        conv = conv + ext_ref[5 + k:5 + k + q, :] * cw_ref[k:k + 1, :]
    if not padded:
        ext_ref[5:8, :] = ext_ref[q + 5:q + 8, :]
    act = conv * _sigmoid(conv)
    xs = act[:, :D_INNER]
    bm = act[:, D_INNER:D_INNER + SSM_GROUPS * D_STATE].astype(BF16)
    cm = act[:, D_INNER + SSM_GROUPS * D_STATE:].astype(BF16)

    if padded:
        zpad_ref, dtpad_ref = pad_refs
        zpad_ref[0:n_valid, :] = z_ref[0]
        dtpad_ref[0:n_valid, :] = dt_ref[0]
        z = zpad_ref[...]
        dt_raw = dtpad_ref[...]
    else:
        z = z_ref[0]
        dt_raw = dt_ref[0]

    row = lax.broadcasted_iota(jnp.int32, (q, q), 0)
    col = lax.broadcasted_iota(jnp.int32, (q, q), 1)
    causal = col <= row
    tri = jnp.where(causal, 1.0, 0.0).astype(BF16)
    tri_t = jnp.where(row <= col, 1.0, 0.0).astype(BF16)

    xdt_in = dt_raw + dtb_ref[...]
    dt = jnp.maximum(xdt_in, 0.0) + jnp.log(1.0 + jnp.exp(-jnp.abs(xdt_in)))
    if padded:
        dt = jnp.where(lax.broadcasted_iota(jnp.int32, dt.shape, 0) < n_valid, dt, 0.0)
    adt = dt * (-jnp.exp(alog_ref[...]))
    acum = _dot_exact_rhs(tri, adt)
    acum_t = _dot_exact_lhs(adt.T, tri_t)
    e = e_ref[...]
    dt_x = _dot_exact_lhs(dt, e)
    acum_x = _dot_exact_lhs(acum, e)
    exp_acum_x = jnp.exp(acum_x)
    acum_last_x = acum_x[q - 1:q, :]
    xdt = xs * dt_x
    xdt_b = xdt.astype(BF16)
    xd = xdt * jnp.exp(acum_last_x - acum_x)
    dtot_x = jnp.exp(acum_last_x)
    lane_lo = lax.broadcasted_iota(jnp.int32, (q, LANES), 1) < SSM_HEAD_DIM

    gw = SSM_HEADS // SSM_GROUPS * SSM_HEAD_DIM
    for g in range(SSM_GROUPS):
        cg = cm[:, g * D_STATE:(g + 1) * D_STATE]
        bg = bm[:, g * D_STATE:(g + 1) * D_STATE]
        cb = _dot_nt(cg, bg)
        st_g = st_ref[g * gw:(g + 1) * gw, :]
        yoff = _dot_nt(cg, st_g.astype(BF16))
        yacc_ref[:, g * gw:(g + 1) * gw] = yoff * exp_acum_x[:, g * gw:(g + 1) * gw]
        for pair in range(gw // LANES):
            h0 = g * (gw // SSM_HEAD_DIM) + 2 * pair
            xp = xdt_b[:, h0 * SSM_HEAD_DIM:h0 * SSM_HEAD_DIM + LANES]
            ys = []
            for h in (h0, h0 + 1):
                seg = jnp.broadcast_to(acum[:, h:h + 1], (q, q)) - jnp.broadcast_to(acum_t[h:h + 1, :], (q, q))
                w = cb * jnp.exp(jnp.where(causal, seg, NEG))
                ys.append(_dot(w.astype(BF16), xp))
            yacc_ref[:, h0 * SSM_HEAD_DIM:h0 * SSM_HEAD_DIM + LANES] += jnp.where(lane_lo, ys[0], ys[1])
        xd_t = jnp.concatenate(
            [xd[:, g * gw + t * LANES:g * gw + (t + 1) * LANES].T for t in range(gw // LANES)], axis=0)
        upd = _dot(xd_t.astype(BF16), bg)
        dtot = jnp.concatenate(
            [jnp.broadcast_to(dtot_x[:, g * gw + t * LANES:g * gw + (t + 1) * LANES], (LANES, LANES)).T
             for t in range(gw // LANES)], axis=0)
        st_ref[g * gw:(g + 1) * gw, :] = dtot * st_g + upd

    y = yacc_ref[...] + xs * dx_ref[...]
    gated = y * (z * _sigmoid(z))
    outs = []
    for g in range(SSM_GROUPS):
        gg = gated[:, g * gw:(g + 1) * gw]
        outs.append(gg * lax.rsqrt(jnp.mean(gg * gg, axis=-1, keepdims=True) + EPS))
    yn = jnp.concatenate(outs, axis=1) * ng_ref[...]
    y_ref[0] = yn[0:n_valid].astype(BF16)

    @pl.when(c == last)
    def _():
        ns_ref[0] = st_ref[...]


def _ssd_mixer(z, xbc, dt, conv_state, ssm_state, conv_w, conv_b, dt_bias, a_log, d_skip, norm_g):
    bt, seq, _ = z.shape
    q = SSD_CHUNK
    n_valid = min(q, seq)
    nc = max(1, seq // q)
    pad = lambda v: jnp.pad(v.astype(F32), (0, LANES - SSM_HEADS)).reshape(1, LANES)
    hh = jnp.arange(LANES)[:, None]
    cc = jnp.arange(D_INNER)[None, :] // SSM_HEAD_DIM
    expand = (hh == cc).astype(BF16)
    dx = jnp.repeat(d_skip.astype(F32), SSM_HEAD_DIM).reshape(1, D_INNER)
    sst = ssm_state.reshape(bt, SSM_HEADS * SSM_HEAD_DIM, D_STATE)
    scratch = [pltpu.VMEM((q + 8, CONV_DIM), F32),
               pltpu.VMEM((SSM_HEADS * SSM_HEAD_DIM, D_STATE), F32),
               pltpu.VMEM((q, D_INNER), F32)]
    if n_valid < q:
        scratch += [pltpu.VMEM((q, D_INNER), F32), pltpu.VMEM((q, LANES), F32)]
    y, ns = pl.pallas_call(
        functools.partial(_ssd_kernel, q=q, n_valid=n_valid),
        grid=(bt, nc),
        in_specs=[pl.BlockSpec((1, n_valid, D_INNER), lambda b, c: (b, c, 0)),
                  pl.BlockSpec((1, n_valid, CONV_DIM), lambda b, c: (b, c, 0)),
                  pl.BlockSpec((1, n_valid, LANES), lambda b, c: (b, c, 0)),
                  pl.BlockSpec((1, CONV_W - 1, CONV_DIM), lambda b, c: (b, 0, 0)),
                  pl.BlockSpec((1, SSM_HEADS * SSM_HEAD_DIM, D_STATE), lambda b, c: (b, 0, 0)),
                  _resident((CONV_W, CONV_DIM)), _resident((1, CONV_DIM)),
                  _resident((1, LANES)), _resident((1, LANES)),
                  _resident((1, D_INNER)), _resident((1, D_INNER)), _resident((LANES, D_INNER))],
        out_specs=[pl.BlockSpec((1, n_valid, D_INNER), lambda b, c: (b, c, 0)),
                   pl.BlockSpec((1, SSM_HEADS * SSM_HEAD_DIM, D_STATE), lambda b, c: (b, 0, 0))],
        out_shape=[jax.ShapeDtypeStruct((bt, seq, D_INNER), BF16),
                   jax.ShapeDtypeStruct((bt, SSM_HEADS * SSM_HEAD_DIM, D_STATE), F32)],
        scratch_shapes=scratch,
        compiler_params=_params(("parallel", "arbitrary")),
        name="ssd_mixer",
    )(z, xbc, dt, conv_state, sst, conv_w, conv_b.reshape(1, CONV_DIM), pad(dt_bias), pad(a_log),
      dx, norm_g.reshape(1, D_INNER), expand)
    return y, ns.reshape(bt, SSM_HEADS, SSM_HEAD_DIM, D_STATE)


def _t5_bucket(dist):
    max_exact = NUM_BUCKETS // 2
    d = jnp.maximum(dist, 0)
    df = jnp.maximum(d, 1).astype(F32)
    large = max_exact + (jnp.log(df / max_exact) / math.log(MAX_DISTANCE / max_exact)
                         * (NUM_BUCKETS - max_exact)).astype(jnp.int32)
    return jnp.where(d < max_exact, d, jnp.minimum(large, NUM_BUCKETS - 1))


def _bias_lookup(tab, dist):
    onehot = jax.nn.one_hot(_t5_bucket(dist), NUM_BUCKETS, dtype=F32)
    return jnp.einsum('...b,bh->...h', onehot, tab.astype(F32), precision=lax.Precision.HIGHEST)


def _head_lanes(t):
    return jnp.repeat(t, HEAD_DIM, axis=-1)


def _swa_prompt_kernel(q_ref, kp_ref, kc_ref, vp_ref, vc_ref, b_ref, o_ref, l_ref):
    blk = q_ref.shape[1]
    qv = q_ref[0] * (HEAD_DIM ** -0.5)
    kk = jnp.concatenate([kp_ref[0], kc_ref[0]], axis=0).astype(BF16)
    vv = jnp.concatenate([vp_ref[0], vc_ref[0]], axis=0).astype(BF16)
    lane = lax.broadcasted_iota(jnp.int32, (blk, SWA_GROUP_DIM), 1) // HEAD_DIM
    o = jnp.zeros((blk, SWA_GROUP_DIM), F32)
    lse = jnp.zeros((blk, SWA_GROUP_DIM), F32)
    for h in range(SWA_HEADS_PER_GROUP):
        mine = lane == h
        qh = jnp.where(mine, qv, 0.0).astype(BF16)
        s = _dot_nt(qh, kk) + b_ref[0, h]
        m = jnp.max(s, axis=-1, keepdims=True)
        p = jnp.exp(s - m)
        l = jnp.sum(p, axis=-1, keepdims=True)
        oh = _dot(p.astype(BF16), vv) / l
        o = jnp.where(mine, oh, o)
        lse = jnp.where(mine, m + jnp.log(l), lse)
    o_ref[0] = o
    l_ref[0] = lse


def _swa_prompt_group(q, k, v, g, dil, bias, bt, seq):
    nb = seq // (dil * SWA_STEPS)
    ncol = SWA_DIM // SWA_GROUP_DIM
    qv, kv, vv = (t.reshape(bt, seq // dil, dil * SWA_DIM) for t in (q, k, v))
    blk = (1, SWA_STEPS, SWA_GROUP_DIM)
    cur = pl.BlockSpec(blk, lambda b, c, n: (b, n, c * ncol + g))
    prev = pl.BlockSpec(blk, lambda b, c, n: (b, jnp.maximum(n - 1, 0), c * ncol + g))
    o, lse = pl.pallas_call(
        _swa_prompt_kernel,
        grid=(bt, dil, nb),
        in_specs=[cur, prev, cur, prev, cur,
                  pl.BlockSpec((1, SWA_HEADS_PER_GROUP, SWA_STEPS, 2 * SWA_STEPS),
                               lambda b, c, n: (jnp.minimum(n, 1), 0, 0, 0))],
        out_specs=[pl.BlockSpec(blk, lambda b, c, n: (b, n, c)),
                   pl.BlockSpec(blk, lambda b, c, n: (b, n, c))],
        out_shape=[jax.ShapeDtypeStruct((bt, seq // dil, dil * SWA_GROUP_DIM), F32)] * 2,
        compiler_params=_params(("parallel", "parallel", "parallel")),
        name="swa_prompt",
    )(qv, kv, kv, vv, vv, bias)
    return o.reshape(bt * seq, SWA_GROUP_DIM), lse.reshape(bt * seq, SWA_GROUP_DIM)


def _swa_prompt_bias(rel_bias, g, dil):
    i = jnp.arange(SWA_STEPS)[:, None]
    j = jnp.arange(2 * SWA_STEPS)[None, :]
    step = SWA_STEPS + i - j
    tab = rel_bias[:, g * SWA_HEADS_PER_GROUP:(g + 1) * SWA_HEADS_PER_GROUP].astype(F32)
    bias = jnp.transpose(_bias_lookup(tab, jnp.clip(step, 0, SWA_STEPS) * dil), (2, 0, 1))
    valid = (step >= 0) & (step <= SWA_STEPS)
    first = valid & (j >= SWA_STEPS)
    return jnp.stack([jnp.where(first, bias, NEG), jnp.where(valid, bias, NEG)])


def _swa_combine_kernel(o0, o1, o2, l0, l1, l2, a_ref):
    ls = (l0[...], l1[...], l2[...])
    m = jnp.maximum(jnp.maximum(ls[0], ls[1]), ls[2])
    es = [jnp.exp(l - m) for l in ls]
    inv = 1.0 / (es[0] + es[1] + es[2])
    for g, o in enumerate((o0, o1, o2)):
        a_ref[:, g * SWA_GROUP_DIM:(g + 1) * SWA_GROUP_DIM] = (o[...] * es[g] * inv).astype(BF16)


def _swa_combine(outs, lses, tm):
    m = outs[0].shape[0]
    spec = pl.BlockSpec((tm, SWA_GROUP_DIM), lambda i: (i, 0))
    return pl.pallas_call(
        _swa_combine_kernel,
        grid=(m // tm,),
        in_specs=[spec] * 6,
        out_specs=pl.BlockSpec((tm, SWA_DIM), lambda i: (i, 0)),
        out_shape=jax.ShapeDtypeStruct((m, SWA_DIM), BF16),
        compiler_params=_params(("parallel",)),
        name="swa_combine",
    )(*outs, *lses)


def _swa_sample_kernel(q_ref, kn_ref, c0_ref, c1_ref, c2_ref, b0_ref, b1_ref, b2_ref, nb_ref,
                       a_ref, o0_ref, o1_ref, o2_ref, *, t_new):
    gd = SWA_GROUP_DIM
    nh = SWA_HEADS_PER_GROUP
    lane_head = lax.broadcasted_iota(jnp.int32, (t_new, gd), 1) // HEAD_DIM
    caches = (c0_ref, c1_ref, c2_ref)
    biases = (b0_ref, b1_ref, b2_ref)
    new_caches = (o0_ref, o1_ref, o2_ref)
    outs, lses = [], []
    for g in range(len(SWA_GROUPS)):
        c_ref = caches[g]
        win = c_ref.shape[2]
        qg = q_ref[0, :, g * gd:(g + 1) * gd] * (HEAD_DIM ** -0.5)
        qs = jnp.concatenate([jnp.where(lane_head == h, qg, 0.0) for h in range(nh)], axis=0).astype(BF16)
        new = kn_ref[0, g]
        s = _dot(qs, c_ref[0, 0:gd, :].astype(BF16)) + biases[g][...]
        sn = _dot(qs, new[0:gd].astype(BF16)) + nb_ref[g]
        m = jnp.maximum(jnp.max(s, axis=-1, keepdims=True), jnp.max(sn, axis=-1, keepdims=True))
        p = jnp.exp(s - m)
        pn = jnp.exp(sn - m)
        l = jnp.sum(p, axis=-1, keepdims=True) + jnp.sum(pn, axis=-1, keepdims=True)
        of = (_dot_nt(p.astype(BF16), c_ref[0, gd:2 * gd, :].astype(BF16))
              + _dot_nt(pn.astype(BF16), new[gd:2 * gd].astype(BF16))) / l
        lf = jnp.broadcast_to(m + jnp.log(l), (nh * t_new, gd))
        o = jnp.zeros((t_new, gd), F32)
        lse = jnp.zeros((t_new, gd), F32)
        for h in range(nh):
            o = jnp.where(lane_head == h, of[h * t_new:(h + 1) * t_new], o)
            lse = jnp.where(lane_head == h, lf[h * t_new:(h + 1) * t_new], lse)
        outs.append(o)
        lses.append(lse)
        rolled = pltpu.roll(c_ref[0], win - t_new, axis=1)
        if win > LANES:
            new_caches[g][0, :, 0:win - LANES] = rolled[:, 0:win - LANES]
        keep = lax.broadcasted_iota(jnp.int32, (2 * gd, LANES), 1) < LANES - t_new
        new_caches[g][0, :, win - LANES:win] = jnp.where(keep, rolled[:, win - LANES:win], new)
    m = jnp.maximum(jnp.maximum(lses[0], lses[1]), lses[2])
    es = [jnp.exp(l - m) for l in lses]
    inv = 1.0 / (es[0] + es[1] + es[2])
    for g in range(len(SWA_GROUPS)):
        a_ref[0, :, g * gd:(g + 1) * gd] = (outs[g] * es[g] * inv).astype(BF16)


def _swa_sample_bias(rel_bias, t_new):
    t = jnp.arange(t_new)
    biases, new_biases = [], []
    for g, (win, dil) in enumerate(SWA_GROUPS):
        tab = rel_bias[:, g * SWA_HEADS_PER_GROUP:(g + 1) * SWA_HEADS_PER_GROUP].astype(F32)
        back = t[:, None] - jnp.arange(win)[None, :]
        ok = (back <= 0) & (back % dil == 0)
        b = jnp.where(ok[..., None], _bias_lookup(tab, win + back), NEG)
        biases.append(jnp.transpose(b, (2, 0, 1)).reshape(SWA_HEADS_PER_GROUP * t_new, win))
        j = jnp.arange(LANES)[None, :] - (LANES - t_new)
        nd = t[:, None] - j
        nok = (j >= 0) & (nd >= 0) & (nd % dil == 0)
        nb = jnp.where(nok[..., None], _bias_lookup(tab, nd), NEG)
        new_biases.append(jnp.transpose(nb, (2, 0, 1)).reshape(SWA_HEADS_PER_GROUP * t_new, LANES))
    return biases, jnp.stack(new_biases)


def _swa_sample(q, k, v, caches, rel_bias, bt, t_new):
    biases, new_bias = _swa_sample_bias(rel_bias, t_new)
    gd = SWA_GROUP_DIM
    tiles = []
    for g in range(len(SWA_GROUPS)):
        kv = jnp.concatenate([k[:, :, g * gd:(g + 1) * gd], v[:, :, g * gd:(g + 1) * gd]], axis=2)
        tiles.append(jnp.pad(jnp.transpose(kv, (0, 2, 1)), ((0, 0), (0, 0), (LANES - t_new, 0))))
    kn = jnp.stack(tiles, axis=1)
    views = [jnp.transpose(c, (0, 2, 3, 4, 1)).reshape(bt, 2 * gd, c.shape[1]) for c in caches]
    cspecs = [pl.BlockSpec((1, 2 * gd, c.shape[2]), lambda b: (b, 0, 0)) for c in views]
    tok = pl.BlockSpec((1, t_new, SWA_DIM), lambda b: (b, 0, 0))
    res = pl.pallas_call(
        functools.partial(_swa_sample_kernel, t_new=t_new),
        grid=(bt,),
        in_specs=[tok, pl.BlockSpec((1, len(SWA_GROUPS), 2 * gd, LANES), lambda b: (b, 0, 0, 0))] + cspecs
                 + [_resident(b.shape) for b in biases] + [_resident(new_bias.shape)],
        out_specs=[tok] + cspecs,
        out_shape=[jax.ShapeDtypeStruct((bt, t_new, SWA_DIM), BF16)]
                  + [jax.ShapeDtypeStruct(c.shape, F32) for c in views],
        compiler_params=_params(("parallel",)),
        name="swa_sample",
    )(q, kn, *views, *biases, new_bias)
    new_caches = [jnp.transpose(c.reshape(bt, 2, SWA_HEADS_PER_GROUP, HEAD_DIM, c.shape[2]), (0, 4, 1, 2, 3))
                  for c in res[1:]]
    return res[0].reshape(bt * t_new, SWA_DIM), new_caches


def _diff_lambda(lam_ref, lam_init):
    lp = lam_ref[...]
    s1 = jnp.sum(lp[0:1] * lp[1:2], axis=-1, keepdims=True)
    s2 = jnp.sum(lp[2:3] * lp[3:4], axis=-1, keepdims=True)
    return jnp.exp(s1) - jnp.exp(s2) + lam_init


def _diff_prompt_kernel(q_ref, k_ref, v_ref, b_ref, lam_ref, on_ref, o_ref, q6_ref, m_ref, acc_ref,
                        *, tq, tk, lam_init):
    i = pl.program_id(2)
    ratio = tk // tq
    lane_lo = lax.broadcasted_iota(jnp.int32, (tq, LANES), 1) < HEAD_DIM
    for r in range(DIFF_REP):
        qr = q_ref[0, :, r * LANES:(r + 1) * LANES] * (HEAD_DIM ** -0.5)
        q6_ref[(2 * r) * tq:(2 * r + 1) * tq, :] = jnp.where(lane_lo, qr, 0.0).astype(BF16)
        q6_ref[(2 * r + 1) * tq:(2 * r + 2) * tq, :] = jnp.where(lane_lo, 0.0, qr).astype(BF16)
    m_ref[...] = jnp.full(m_ref.shape, NEG, F32)
    acc_ref[...] = jnp.zeros(acc_ref.shape, F32)
    ones = jnp.ones((tk, LANES), BF16)

    def body(j, carry):
        start = pl.multiple_of(j * tk, tk)
        kb = k_ref[0, pl.ds(start, tk), :].astype(BF16)
        vx = jnp.concatenate([v_ref[0, pl.ds(start, tk), :].astype(BF16), ones], axis=1)
        d0 = i - j * ratio + (ratio - 1)
        for r in range(DIFF_REP):
            rows = slice(2 * r * tq, (2 * r + 2) * tq)
            s = _dot_nt(q6_ref[rows, :], kb)
            bias = jnp.concatenate([b_ref[0, d0 - c, r] for c in range(ratio)], axis=1)
            s = s + jnp.concatenate([bias, bias], axis=0)
            smax = s[:, 0:LANES]
            for c in range(1, tk // LANES):
                smax = jnp.maximum(smax, s[:, c * LANES:(c + 1) * LANES])
            m_old = m_ref[rows, :]
            m_new = jnp.maximum(m_old, jnp.max(smax, axis=-1, keepdims=True))
            alpha = jnp.exp(m_old - m_new)
            p = jnp.concatenate([jnp.exp(s[:, c * LANES:(c + 1) * LANES] - m_new)
                                 for c in range(tk // LANES)], axis=1).astype(BF16)
            acc_ref[rows, :] = jnp.concatenate([alpha, alpha], axis=1) * acc_ref[rows, :] + _dot(p, vx)
            m_ref[rows, :] = m_new
        return carry

    lax.fori_loop(0, i // ratio + 1, body, 0)
    lam = _diff_lambda(lam_ref, lam_init)
    for r in range(DIFF_REP):
        o0 = acc_ref[(2 * r) * tq:(2 * r + 1) * tq, :]
        o1 = acc_ref[(2 * r + 1) * tq:(2 * r + 2) * tq, :]
        a = o0[:, :DIFF_V_DIM] / o0[:, DIFF_V_DIM:] - lam * (o1[:, :DIFF_V_DIM] / o1[:, DIFF_V_DIM:])
        o_ref[0, :, r * LANES:(r + 1) * LANES] = (_rms(a, on_ref[...]) * (1.0 - lam_init)).astype(BF16)


def _diff_prompt_bias(rel_bias, seq, tq, ratio):
    nb = seq // tq
    delta = jnp.arange(-(ratio - 1), nb)
    d = delta[:, None, None] * tq + jnp.arange(tq)[None, :, None] - jnp.arange(tq)[None, None, :]
    tab = jnp.where((d >= 0)[..., None], _bias_lookup(rel_bias, d), NEG)
    tab = jnp.transpose(tab, (3, 0, 1, 2)).reshape(DIFF_KV_HEADS, DIFF_REP, nb + ratio - 1, tq, tq)
    return jnp.transpose(tab, (0, 2, 1, 3, 4))


def _diff_prompt(q, k, v, rel_bias, lam_p, out_norm, lam_init, bt, seq):
    tq, tk = 128, 512
    nb = seq // tq
    ratio = tk // tq
    bias = _diff_prompt_bias(rel_bias, seq, tq, ratio)
    qw = DIFF_REP * 2 * HEAD_DIM
    a = pl.pallas_call(
        functools.partial(_diff_prompt_kernel, tq=tq, tk=tk, lam_init=lam_init),
        grid=(bt, DIFF_KV_HEADS, nb),
        in_specs=[pl.BlockSpec((1, tq, qw), lambda b, g, i: (b, i, g)),
                  pl.BlockSpec((1, seq, LANES), lambda b, g, i: (b, 0, g)),
                  pl.BlockSpec((1, seq, LANES), lambda b, g, i: (b, 0, g)),
                  pl.BlockSpec((1, nb + ratio - 1, DIFF_REP, tq, tq), lambda b, g, i: (g, 0, 0, 0, 0)),
                  _resident((4, HEAD_DIM)), _resident((1, DIFF_V_DIM))],
        out_specs=pl.BlockSpec((1, tq, qw), lambda b, g, i: (b, i, g)),
        out_shape=jax.ShapeDtypeStruct((bt, seq, DIFF_Q_DIM), BF16),
        scratch_shapes=[pltpu.VMEM((2 * DIFF_REP * tq, LANES), BF16),
                        pltpu.VMEM((2 * DIFF_REP * tq, LANES), F32),
                        pltpu.VMEM((2 * DIFF_REP * tq, 2 * DIFF_V_DIM), F32)],
        compiler_params=_params(("parallel", "parallel", "arbitrary")),
        name="diff_prompt",
    )(q.reshape(bt, seq, DIFF_Q_DIM), k.reshape(bt, seq, DIFF_K_DIM), v.reshape(bt, seq, DIFF_V_ALL),
      bias, lam_p.astype(F32), out_norm.reshape(1, DIFF_V_DIM).astype(F32))
    return a.reshape(bt * seq, DIFF_Q_DIM)


def _diff_sample_kernel(pt_ref, qb_ref, kn_ref, vn_ref, *refs, t_new, lam_init):
    npg = PAGES_PER_STEP
    k_refs = refs[:npg]
    v_refs = refs[npg:2 * npg]
    b_ref, nb_ref, lam_ref, on_ref, o_ref, m_ref, l_ref, acc_ref = refs[2 * npg:]
    s_idx = pl.program_id(1)
    per_map = DIFF_REP * t_new
    per_g = 2 * per_map

    @pl.when(s_idx == 0)
    def _():
        m_ref[...] = jnp.full(m_ref.shape, NEG, F32)
        l_ref[...] = jnp.zeros(l_ref.shape, F32)
        acc_ref[...] = jnp.zeros(acc_ref.shape, F32)

    qb = (qb_ref[0] * (HEAD_DIM ** -0.5)).astype(BF16)

    def absorb(kt_refs, vv_refs, biases):
        ss = [_dot(qb, kt[0].astype(BF16)) + b for kt, b in zip(kt_refs, biases)]
        smax = ss[0]
        for s in ss[1:]:
            smax = jnp.maximum(smax, s)
        m_old = m_ref[...]
        m_new = jnp.maximum(m_old, jnp.max(smax, axis=-1, keepdims=True))
        alpha = jnp.exp(m_old - m_new)
        ps = [jnp.exp(s - m_new) for s in ss]
        psum = ps[0]
        for p in ps[1:]:
            psum = psum + p
        l_ref[...] = alpha * l_ref[...] + jnp.sum(psum, axis=-1, keepdims=True)
        ps = [p.astype(BF16) for p in ps]
        for g in range(DIFF_KV_HEADS):
            rows = slice(g * per_g, (g + 1) * per_g)
            new = alpha[rows] * acc_ref[rows, :]
            for p, vv in zip(ps, vv_refs):
                vg = vv[0, pl.ds(g, PAGE_SIZE, stride=DIFF_KV_HEADS), :].astype(BF16)
                new = new + _dot(p[rows], vg)
            acc_ref[rows, :] = new
        m_ref[...] = m_new

    absorb(k_refs, v_refs, [b_ref[pp] for pp in range(npg)])

    @pl.when(s_idx == pl.num_programs(1) - 1)
    def _():
        absorb([kn_ref], [vn_ref], [nb_ref[...]])
        lam = _diff_lambda(lam_ref, lam_init)
        o = acc_ref[...] / l_ref[...]
        for g in range(DIFF_KV_HEADS):
            for r in range(DIFF_REP):
                r0 = g * per_g + r * t_new
                blk = o[r0:r0 + t_new] - lam * o[r0 + per_map:r0 + per_map + t_new]
                col = (g * DIFF_REP + r) * DIFF_V_DIM
                o_ref[0, :, col:col + DIFF_V_DIM] = (_rms(blk, on_ref[...]) * (1.0 - lam_init)).astype(BF16)


def _diff_sample(q, k, v, cache_k, cache_v, page_table, rel_bias, lam_p, out_norm, lam_init, bt, t_new):
    n_pages = page_table.shape[1]
    past = n_pages * PAGE_SIZE
    n_phys = cache_k.shape[0]
    nrow = DIFF_KV_HEADS * 2 * DIFF_REP * t_new
    q6 = q.reshape(bt, t_new, DIFF_KV_HEADS, DIFF_REP, 2, HEAD_DIM)
    q6 = jnp.transpose(q6, (0, 2, 4, 3, 1, 5))
    eye_m = jnp.eye(2, dtype=F32)
    eye_g = jnp.eye(DIFF_KV_HEADS, dtype=F32)
    qb = jnp.einsum('bgmrtd,mn,gh->bgmrthnd', q6, eye_m, eye_g).reshape(bt, nrow, DIFF_K_DIM)
    tab = rel_bias.astype(F32)
    tt = jnp.arange(t_new)

    def rows_of(b):
        n = b.shape[1]
        b = jnp.transpose(b, (2, 0, 1)).reshape(DIFF_KV_HEADS, 1, DIFF_REP, t_new, n)
        return jnp.broadcast_to(b, (DIFF_KV_HEADS, 2, DIFF_REP, t_new, n)).reshape(nrow, n)

    dist = past + tt[:, None] - jnp.arange(past)[None, :]
    bias = rows_of(_bias_lookup(tab, dist)).reshape(nrow, n_pages, PAGE_SIZE)
    bias = jnp.transpose(bias, (1, 0, 2))
    nd = tt[:, None] - jnp.arange(PAGE_SIZE)[None, :]
    nbias = rows_of(jnp.where((nd >= 0)[..., None], _bias_lookup(tab, nd), NEG))
    kn = jnp.transpose(k.reshape(bt, t_new, DIFF_K_DIM), (0, 2, 1))
    kn = jnp.pad(kn, ((0, 0), (0, 0), (0, PAGE_SIZE - t_new)))
    vn = jnp.pad(v.reshape(bt, t_new * DIFF_KV_HEADS, DIFF_V_DIM),
                 ((0, 0), (0, (PAGE_SIZE - t_new) * DIFF_KV_HEADS), (0, 0)))
    ck = jnp.transpose(cache_k, (0, 2, 3, 4, 1)).reshape(n_phys, DIFF_K_DIM, PAGE_SIZE)
    cv = cache_v.reshape(n_phys, PAGE_SIZE * DIFF_KV_HEADS, DIFF_V_DIM)
    npg = PAGES_PER_STEP

    def page_spec(pp):
        return pl.BlockSpec((1, DIFF_K_DIM, PAGE_SIZE), lambda b, s, pt: (pt[b, s * npg + pp], 0, 0))

    const = lambda shape: pl.BlockSpec(shape, lambda b, s, pt: (0,) * len(shape))
    grid_spec = pltpu.PrefetchScalarGridSpec(
        num_scalar_prefetch=1,
        grid=(bt, n_pages // npg),
        in_specs=[pl.BlockSpec((1, nrow, DIFF_K_DIM), lambda b, s, pt: (b, 0, 0)),
                  pl.BlockSpec((1, DIFF_K_DIM, PAGE_SIZE), lambda b, s, pt: (b, 0, 0)),
                  pl.BlockSpec((1, DIFF_K_DIM, PAGE_SIZE), lambda b, s, pt: (b, 0, 0))]
                 + [page_spec(pp) for pp in range(npg)] * 2
                 + [pl.BlockSpec((npg, nrow, PAGE_SIZE), lambda b, s, pt: (s, 0, 0)),
                    const((nrow, PAGE_SIZE)), const((4, HEAD_DIM)), const((1, DIFF_V_DIM))],
        out_specs=pl.BlockSpec((1, t_new, DIFF_Q_DIM), lambda b, s, pt: (b, 0, 0)),
        scratch_shapes=[pltpu.VMEM((nrow, LANES), F32), pltpu.VMEM((nrow, LANES), F32),
                        pltpu.VMEM((nrow, DIFF_V_DIM), F32)],
    )
    a = pl.pallas_call(
        functools.partial(_diff_sample_kernel, t_new=t_new, lam_init=lam_init),
        grid_spec=grid_spec,
        out_shape=jax.ShapeDtypeStruct((bt, t_new, DIFF_Q_DIM), BF16),
        compiler_params=_params(("parallel", "arbitrary")),
        name="diff_sample",
    )(page_table, qb, kn, vn, *([ck] * npg), *([cv] * npg), bias, nbias,
      lam_p.astype(F32), out_norm.reshape(1, DIFF_V_DIM).astype(F32))
    return a.reshape(bt * t_new, DIFF_Q_DIM)


def _ffn_weights(w_in, w_out):
    return w_in.astype(BF16), w_out.astype(BF16)


def kernel(x_prompt, x_sample, state_ssm_conv, state_ssm, cache_swa_kv0, cache_swa_kv1, cache_swa_kv2, cache_diff_k, cache_diff_v, page_table, rel_bias, norm_mix, norm_ffn, ffn_w_in, ffn_w_out, ssm_w_in, ssm_conv_w, ssm_conv_b, ssm_dt_bias, ssm_a_log, ssm_d, ssm_norm, ssm_w_out, swa_w_qkv, swa_q_norm, swa_k_norm, swa_w_out, diff_w_qkv, diff_q_norm, diff_k_norm, diff_lambda, diff_out_norm, diff_w_out):
    bp, seq, d = x_prompt.shape
    bs, t_new, _ = x_sample.shape
    mp, ms = bp * seq, bs * t_new
    tm_p, tm_s = 256, ms
    xp = x_prompt.reshape(mp, d)
    xs = x_sample.reshape(ms, d)
    swa_caches = (cache_swa_kv0, cache_swa_kv1, cache_swa_kv2)
    conv_p, conv_s, ssm_p, ssm_s = [], [], [], []
    swa_p = tuple([] for _ in SWA_GROUPS)
    swa_s = tuple([] for _ in SWA_GROUPS)
    dk_p, dk_s, dv_p, dv_s = [], [], [], []
    i_ssd = i_swa = i_diff = 0
    for layer in range(DEPTH):
        kind = layer % 3
        g_mix = norm_mix[layer].astype(F32)
        w1, w2 = _ffn_weights(ffn_w_in[layer], ffn_w_out[layer])
        if kind == 0:
            i = i_ssd
            i_ssd += 1
            w_in = ssm_w_in[i].astype(BF16)
            weights = [w_in[:, :D_INNER], w_in[:, D_INNER:D_INNER + CONV_DIM],
                       jnp.pad(w_in[:, D_INNER + CONV_DIM:], ((0, 0), (0, LANES - SSM_HEADS)))]
            wo = ssm_w_out[i].astype(BF16)
            mixed = []
            for x, bt, L, tm, cst, sst, convs, states in (
                    (xp, bp, seq, tm_p, jnp.zeros((bp, CONV_W - 1, CONV_DIM), F32),
                     jnp.zeros((bp, SSM_HEADS, SSM_HEAD_DIM, D_STATE), F32), conv_p, ssm_p),
                    (xs, bs, t_new, tm_s, state_ssm_conv[i], state_ssm[i], conv_s, ssm_s)):
                z, xbc, dt = _norm_proj(x, g_mix, weights, [None] * 3, tm)
                xbc3 = xbc.reshape(bt, L, CONV_DIM)
                y, ns = _ssd_mixer(z.reshape(bt, L, D_INNER), xbc3, dt.reshape(bt, L, LANES), cst, sst,
                                   ssm_conv_w[i].astype(F32), ssm_conv_b[i].astype(F32), ssm_dt_bias[i],
                                   ssm_a_log[i], ssm_d[i], ssm_norm[i].astype(F32))
                convs.append(xbc3[:, L - (CONV_W - 1):])
                states.append(ns)
                mixed.append(y.reshape(bt * L, D_INNER))
            ap, as_ = mixed
        elif kind == 1:
            i = i_swa
            i_swa += 1
            w_qkv = swa_w_qkv[i].astype(BF16)
            weights = [w_qkv[:, :SWA_DIM], w_qkv[:, SWA_DIM:2 * SWA_DIM], w_qkv[:, 2 * SWA_DIM:]]
            gains = [jnp.tile(swa_q_norm[i].astype(F32), N_ATTN_HEADS).reshape(1, SWA_DIM),
                     jnp.tile(swa_k_norm[i].astype(F32), N_ATTN_HEADS).reshape(1, SWA_DIM), None]
            wo = swa_w_out[i].astype(BF16)
            q, k, kt, v, vt = _norm_proj(xp, g_mix, weights, gains, tm_p, emit=("n", "nt", "nt"), seq=seq)
            outs, lses = [], []
            for g, (win, dil) in enumerate(SWA_GROUPS):
                o, lse = _swa_prompt_group(q, k, v, g, dil, _swa_prompt_bias(rel_bias, g, dil), bp, seq)
                outs.append(o)
                lses.append(lse)
                keep = min(win, seq)
                sl = slice(g * SWA_GROUP_DIM, (g + 1) * SWA_GROUP_DIM)
                kv_t = jnp.stack([kt[:, sl, seq - keep:], vt[:, sl, seq - keep:]], axis=1)
                kv_t = kv_t.reshape(bp, 2, SWA_HEADS_PER_GROUP, HEAD_DIM, keep)
                swa_p[g].append(jnp.transpose(kv_t, (0, 4, 1, 2, 3)))
            ap = _swa_combine(outs, lses, tm_p)
            q, k, v = _norm_proj(xs, g_mix, weights, gains, tm_s)
            q3, k3, v3 = (t.reshape(bs, t_new, SWA_DIM) for t in (q, k, v))
            as_, new_bufs = _swa_sample(q3, k3, v3, [c[i] for c in swa_caches], rel_bias, bs, t_new)
            for g in range(len(SWA_GROUPS)):
                swa_s[g].append(new_bufs[g])
        else:
            i = i_diff
            i_diff += 1
            lam_init = 0.8 - 0.6 * math.exp(-0.3 * layer)
            w_qkv = diff_w_qkv[i].astype(BF16)
            weights = [w_qkv[:, :DIFF_Q_DIM], w_qkv[:, DIFF_Q_DIM:DIFF_Q_DIM + DIFF_K_DIM],
                       w_qkv[:, DIFF_Q_DIM + DIFF_K_DIM:]]
            gains = [jnp.tile(diff_q_norm[i].astype(F32), DIFF_Q_DIM // HEAD_DIM).reshape(1, DIFF_Q_DIM),
                     jnp.tile(diff_k_norm[i].astype(F32), DIFF_K_DIM // HEAD_DIM).reshape(1, DIFF_K_DIM), None]
            wo = diff_w_out[i].astype(BF16)
            q, k, kt, v = _norm_proj(xp, g_mix, weights, gains, tm_p, emit=("n", "nt", "n"), seq=seq)
            ap = _diff_prompt(q, k, v, rel_bias, diff_lambda[i], diff_out_norm[i], lam_init, bp, seq)
            dk_p.append(jnp.transpose(kt.reshape(bp, DIFF_KV_HEADS, 2, HEAD_DIM, seq), (0, 4, 1, 2, 3)))
            dv_p.append(v.reshape(bp, seq, DIFF_KV_HEADS, DIFF_V_DIM))
            q, k, v = _norm_proj(xs, g_mix, weights, gains, tm_s)
            as_ = _diff_sample(q, k, v, cache_diff_k[i], cache_diff_v[i], page_table, rel_bias, diff_lambda[i],
                               diff_out_norm[i], lam_init, bs, t_new)
            dk_s.append(k.reshape(bs, t_new, DIFF_KV_HEADS, 2, HEAD_DIM))
            dv_s.append(v.reshape(bs, t_new, DIFF_KV_HEADS, DIFF_V_DIM))
        g_ffn = norm_ffn[layer].astype(F32)
        xp = _mix_ffn(xp, ap, wo, g_ffn, w1, w2, tm_p)
        xs = _mix_ffn(xs, as_, wo, g_ffn, w1, w2, tm_s)
    return (xp.reshape(bp, seq, d), xs.reshape(bs, t_new, d),
            jnp.stack(conv_p), jnp.stack(conv_s), jnp.stack(ssm_p), jnp.stack(ssm_s),
            jnp.stack(swa_p[0]), jnp.stack(swa_s[0]), jnp.stack(swa_p[1]), jnp.stack(swa_s[1]),
            jnp.stack(swa_p[2]), jnp.stack(swa_s[2]),
            jnp.stack(dk_p), jnp.stack(dk_s), jnp.stack(dv_p), jnp.stack(dv_s))
```

```python
import functools
import math

import jax
import jax.numpy as jnp
from jax import lax
from jax.experimental import pallas as pl
from jax.experimental.pallas import tpu as pltpu

F32 = jnp.float32
BF16 = jnp.bfloat16

D_MODEL = 1024
DEPTH = 4
D_FF = 2816
D_INNER = 2048
SSM_HEADS = 32
SSM_HEAD_DIM = 64
SSM_GROUPS = 4
D_STATE = 128
CONV_W = 4
CONV_DIM = D_INNER + 2 * SSM_GROUPS * D_STATE
SSD_CHUNK = 128
HEAD_DIM = 64
N_ATTN_HEADS = 12
NUM_BUCKETS = 32
MAX_DISTANCE = 2048
SWA_GROUPS = ((128, 1), (512, 4), (2048, 16))
SWA_HEADS_PER_GROUP = 4
SWA_GROUP_DIM = SWA_HEADS_PER_GROUP * HEAD_DIM
SWA_DIM = N_ATTN_HEADS * HEAD_DIM
SWA_STEPS = 128
DIFF_KV_HEADS = 4
DIFF_REP = 3
DIFF_V_DIM = 128
DIFF_Q_DIM = N_ATTN_HEADS * 2 * HEAD_DIM
DIFF_K_DIM = DIFF_KV_HEADS * 2 * HEAD_DIM
DIFF_V_ALL = DIFF_KV_HEADS * DIFF_V_DIM
PAGE_SIZE = 128
EPS = 1e-6
NEG = -1e30

LANES = 128
VMEM_LIMIT = 56 * 1024 * 1024
PAGES_PER_STEP = 8


def _dot(a, b):
    return jnp.dot(a, b, preferred_element_type=F32)


def _dot_nt(a, b):
    return lax.dot_general(a, b, (((1,), (1,)), ((), ())), preferred_element_type=F32)


def _split3(x):
    hi = x.astype(BF16)
    r = x - hi.astype(F32)
    mid = r.astype(BF16)
    lo = (r - mid.astype(F32)).astype(BF16)
    return hi, mid, lo


def _dot_exact_rhs(a_bf16, x):
    hi, mid, lo = _split3(x)
    return _dot(a_bf16, hi) + _dot(a_bf16, mid) + _dot(a_bf16, lo)


def _dot_exact_lhs(x, a_bf16):
    hi, mid, lo = _split3(x)
    return _dot(hi, a_bf16) + _dot(mid, a_bf16) + _dot(lo, a_bf16)


def _sigmoid(x):
    return 1.0 / (1.0 + jnp.exp(-x))


def _rms(x, g):
    return x * lax.rsqrt(jnp.mean(x * x, axis=-1, keepdims=True) + EPS) * g


def _resident(shape):
    n = len(shape)
    return pl.BlockSpec(shape, lambda *_: (0,) * n, pipeline_mode=pl.Buffered(1))


def _params(sem):
    return pltpu.CompilerParams(dimension_semantics=sem, vmem_limit_bytes=VMEM_LIMIT)


def _norm_proj_kernel(*refs, n_out, head_norm, emit):
    x_ref, g_ref = refs[0], refs[1]
    w_refs = refs[2:2 + n_out]
    n_hn = sum(head_norm)
    hn_refs = refs[2 + n_out:2 + n_out + n_hn]
    pos = 2 + n_out + n_hn
    seg_ref = refs[pos] if n_hn else None
    pos += 1 if n_hn else 0
    o_refs = list(refs[pos:])
    h = _rms(x_ref[...], g_ref[...]).astype(BF16)
    k = 0
    for i in range(n_out):
        y = _dot(h, w_refs[i][...])
        if head_norm[i]:
            gain = hn_refs[k][...]
            k += 1
            seg = seg_ref[...]
            parts = []
            for c in range(y.shape[1] // seg.shape[0]):
                yc = y[:, c * seg.shape[0]:(c + 1) * seg.shape[0]]
                sq = yc * yc
                hi = sq.astype(BF16)
                lo = (sq - hi.astype(F32)).astype(BF16)
                ms = (_dot(hi, seg) + _dot(lo, seg)) * (1.0 / HEAD_DIM)
                parts.append(yc * lax.rsqrt(ms + EPS))
            y = jnp.concatenate(parts, axis=1) * gain
        if "n" in emit[i]:
            o_refs.pop(0)[...] = y
        if "t" in emit[i]:
            o_refs.pop(0)[0] = y.T
        if "b" in emit[i]:
            ref = o_refs.pop(0)
            for cb in range(y.shape[1] // LANES):
                ref[0, cb] = y[:, cb * LANES:(cb + 1) * LANES]


def _norm_proj(x, g, weights, head_gains, tm, emit=None, seq=None):
    m, d = x.shape
    n_out = len(weights)
    emit = tuple(emit or ("n",) * n_out)
    head_norm = tuple(hg is not None for hg in head_gains)
    ins = [x, g.reshape(1, d)] + list(weights)
    specs = [pl.BlockSpec((tm, d), lambda i: (i, 0)), _resident((1, d))]
    specs += [_resident(w.shape) for w in weights]
    for hg in head_gains:
        if hg is not None:
            ins.append(hg)
            specs.append(_resident(hg.shape))
    if any(head_norm):
        r = jnp.arange(2 * LANES)
        seg = (r[:, None] // HEAD_DIM == r[None, :] // HEAD_DIM).astype(BF16)
        ins.append(seg)
        specs.append(_resident(seg.shape))
    per_seq = seq // tm if seq else 1
    out_specs, out_shape = [], []
    for w, e in zip(weights, emit):
        n = w.shape[1]
        if "n" in e:
            out_specs.append(pl.BlockSpec((tm, n), lambda i: (i, 0)))
            out_shape.append(jax.ShapeDtypeStruct((m, n), F32))
        if "t" in e:
            out_specs.append(pl.BlockSpec((1, n, tm), lambda i: (i // per_seq, 0, i % per_seq)))
            out_shape.append(jax.ShapeDtypeStruct((m // seq, n, seq), F32))
        if "b" in e:
            out_specs.append(pl.BlockSpec((1, n // LANES, tm, LANES), lambda i: (i // per_seq, 0, i % per_seq, 0)))
            out_shape.append(jax.ShapeDtypeStruct((m // seq, n // LANES, seq, LANES), F32))
    return pl.pallas_call(
        functools.partial(_norm_proj_kernel, n_out=n_out, head_norm=head_norm, emit=emit),
        grid=(m // tm,),
        in_specs=specs,
        out_specs=out_specs,
        out_shape=out_shape,
        compiler_params=_params(("parallel",)),
        name="norm_proj",
    )(*ins)


def _mix_ffn_kernel(x_ref, a_ref, wo_ref, g_ref, w1_ref, w2_ref, o_ref):
    x1 = x_ref[...] + _dot(a_ref[...], wo_ref[...])
    h = _rms(x1, g_ref[...]).astype(BF16)
    gu = _dot(h, w1_ref[...])
    gate = gu[:, :D_FF]
    act = (gate * _sigmoid(gate) * gu[:, D_FF:]).astype(BF16)
    o_ref[...] = x1 + _dot(act, w2_ref[...])


def _mix_ffn(x, a, wo, g, w1, w2, tm):
    m, d = x.shape
    ka = a.shape[1]
    return pl.pallas_call(
        _mix_ffn_kernel,
        grid=(m // tm,),
        in_specs=[pl.BlockSpec((tm, d), lambda i: (i, 0)),
                  pl.BlockSpec((tm, ka), lambda i: (i, 0)),
                  _resident(wo.shape), _resident((1, d)), _resident(w1.shape), _resident(w2.shape)],
        out_specs=pl.BlockSpec((tm, d), lambda i: (i, 0)),
        out_shape=jax.ShapeDtypeStruct((m, d), F32),
        compiler_params=_params(("parallel",)),
        name="mix_ffn",
    )(x, a, wo, g.reshape(1, d), w1, w2)


def _ssd_kernel(z_ref, xbc_ref, dt_ref, cst_ref, sst_ref, cw_ref, cb_ref, dtb_ref, alog_ref,
                dx_ref, ng_ref, e_ref, y_ref, ns_ref, ext_ref, st_ref, yacc_ref, *pad_refs,
                q, n_valid):
    c = pl.program_id(1)
    last = pl.num_programs(1) - 1
    padded = n_valid < q

    @pl.when(c == 0)
    def _():
        ext_ref[...] = jnp.zeros(ext_ref.shape, F32)
        ext_ref[5:8, :] = cst_ref[0]
        st_ref[...] = sst_ref[0]
        if padded:
            for r in pad_refs:
                r[...] = jnp.zeros(r.shape, F32)

    ext_ref[8:8 + n_valid, :] = xbc_ref[0]
    conv = cb_ref[...]
    for k in range(CONV_W):
        conv = conv + ext_ref[5 + k:5 + k + q, :] * cw_ref[k:k + 1, :]
    if not padded:
        ext_ref[5:8, :] = ext_ref[q + 5:q + 8, :]
    act = conv * _sigmoid(conv)
    xs = act[:, :D_INNER]
    bm = act[:, D_INNER:D_INNER + SSM_GROUPS * D_STATE].astype(BF16)
    cm = act[:, D_INNER + SSM_GROUPS * D_STATE:].astype(BF16)

    if padded:
        zpad_ref, dtpad_ref = pad_refs
        zpad_ref[0:n_valid, :] = z_ref[0]
        dtpad_ref[0:n_valid, :] = dt_ref[0]
        z = zpad_ref[...]
        dt_raw = dtpad_ref[...]
    else:
        z = z_ref[0]
        dt_raw = dt_ref[0]

    row = lax.broadcasted_iota(jnp.int32, (q, q), 0)
    col = lax.broadcasted_iota(jnp.int32, (q, q), 1)
    causal = col <= row
    tri = jnp.where(causal, 1.0, 0.0).astype(BF16)
    tri_t = jnp.where(row <= col, 1.0, 0.0).astype(BF16)

    xdt_in = dt_raw + dtb_ref[...]
    dt = jnp.maximum(xdt_in, 0.0) + jnp.log(1.0 + jnp.exp(-jnp.abs(xdt_in)))
    if padded:
        dt = jnp.where(lax.broadcasted_iota(jnp.int32, dt.shape, 0) < n_valid, dt, 0.0)
    adt = dt * (-jnp.exp(alog_ref[...]))
    acum = _dot_exact_rhs(tri, adt)
    acum_t = _dot_exact_lhs(adt.T, tri_t)
    e = e_ref[...]
    dt_x = _dot_exact_lhs(dt, e)
    acum_x = _dot_exact_lhs(acum, e)
    exp_acum_x = jnp.exp(acum_x)
    acum_last_x = acum_x[q - 1:q, :]
    xdt = xs * dt_x
    xdt_b = xdt.astype(BF16)
    xd = xdt * jnp.exp(acum_last_x - acum_x)
    dtot_x = jnp.exp(acum_last_x)
    lane_lo = lax.broadcasted_iota(jnp.int32, (q, LANES), 1) < SSM_HEAD_DIM

    gw = SSM_HEADS // SSM_GROUPS * SSM_HEAD_DIM
    for g in range(SSM_GROUPS):
        cg = cm[:, g * D_STATE:(g + 1) * D_STATE]
        bg = bm[:, g * D_STATE:(g + 1) * D_STATE]
        cb = _dot_nt(cg, bg)
        st_g = st_ref[g * gw:(g + 1) * gw, :]
        yoff = _dot_nt(cg, st_g.astype(BF16))
        yacc_ref[:, g * gw:(g + 1) * gw] = yoff * exp_acum_x[:, g * gw:(g + 1) * gw]
        for pair in range(gw // LANES):
            h0 = g * (gw // SSM_HEAD_DIM) + 2 * pair
            xp = xdt_b[:, h0 * SSM_HEAD_DIM:h0 * SSM_HEAD_DIM + LANES]
            ys = []
            for h in (h0, h0 + 1):
                seg = jnp.broadcast_to(acum[:, h:h + 1], (q, q)) - jnp.broadcast_to(acum_t[h:h + 1, :], (q, q))
                w = cb * jnp.exp(jnp.where(causal, seg, NEG))
                ys.append(_dot(w.astype(BF16), xp))
            yacc_ref[:, h0 * SSM_HEAD_DIM:h0 * SSM_HEAD_DIM + LANES] += jnp.where(lane_lo, ys[0], ys[1])
        xd_t = jnp.concatenate(
            [xd[:, g * gw + t * LANES:g * gw + (t + 1) * LANES].T for t in range(gw // LANES)], axis=0)
        upd = _dot(xd_t.astype(BF16), bg)
        dtot = jnp.concatenate(
            [jnp.broadcast_to(dtot_x[:, g * gw + t * LANES:g * gw + (t + 1) * LANES], (LANES, LANES)).T
             for t in range(gw // LANES)], axis=0)
        st_ref[g * gw:(g + 1) * gw, :] = dtot * st_g + upd

    y = yacc_ref[...] + xs * dx_ref[...]
    gated = y * (z * _sigmoid(z))
    outs = []
    for g in range(SSM_GROUPS):
        gg = gated[:, g * gw:(g + 1) * gw]
        outs.append(gg * lax.rsqrt(jnp.mean(gg * gg, axis=-1, keepdims=True) + EPS))
    yn = jnp.concatenate(outs, axis=1) * ng_ref[...]
    y_ref[0] = yn[0:n_valid].astype(BF16)

    @pl.when(c == last)
    def _():
        ns_ref[0] = st_ref[...]


def _ssd_mixer(z, xbc, dt, conv_state, ssm_state, conv_w, conv_b, dt_bias, a_log, d_skip, norm_g):
    bt, seq, _ = z.shape
    q = SSD_CHUNK
    n_valid = min(q, seq)
    nc = max(1, seq // q)
    pad = lambda v: jnp.pad(v.astype(F32), (0, LANES - SSM_HEADS)).reshape(1, LANES)
    hh = jnp.arange(LANES)[:, None]
    cc = jnp.arange(D_INNER)[None, :] // SSM_HEAD_DIM
    expand = (hh == cc).astype(BF16)
    dx = jnp.repeat(d_skip.astype(F32), SSM_HEAD_DIM).reshape(1, D_INNER)
    sst = ssm_state.reshape(bt, SSM_HEADS * SSM_HEAD_DIM, D_STATE)
    scratch = [pltpu.VMEM((q + 8, CONV_DIM), F32),
               pltpu.VMEM((SSM_HEADS * SSM_HEAD_DIM, D_STATE), F32),
               pltpu.VMEM((q, D_INNER), F32)]
    if n_valid < q:
        scratch += [pltpu.VMEM((q, D_INNER), F32), pltpu.VMEM((q, LANES), F32)]
    y, ns = pl.pallas_call(
        functools.partial(_ssd_kernel, q=q, n_valid=n_valid),
        grid=(bt, nc),
        in_specs=[pl.BlockSpec((1, n_valid, D_INNER), lambda b, c: (b, c, 0)),
                  pl.BlockSpec((1, n_valid, CONV_DIM), lambda b, c: (b, c, 0)),
                  pl.BlockSpec((1, n_valid, LANES), lambda b, c: (b, c, 0)),
                  pl.BlockSpec((1, CONV_W - 1, CONV_DIM), lambda b, c: (b, 0, 0)),
                  pl.BlockSpec((1, SSM_HEADS * SSM_HEAD_DIM, D_STATE), lambda b, c: (b, 0, 0)),
                  _resident((CONV_W, CONV_DIM)), _resident((1, CONV_DIM)),
                  _resident((1, LANES)), _resident((1, LANES)),
                  _resident((1, D_INNER)), _resident((1, D_INNER)), _resident((LANES, D_INNER))],
        out_specs=[pl.BlockSpec((1, n_valid, D_INNER), lambda b, c: (b, c, 0)),
                   pl.BlockSpec((1, SSM_HEADS * SSM_HEAD_DIM, D_STATE), lambda b, c: (b, 0, 0))],
        out_shape=[jax.ShapeDtypeStruct((bt, seq, D_INNER), BF16),
                   jax.ShapeDtypeStruct((bt, SSM_HEADS * SSM_HEAD_DIM, D_STATE), F32)],
        scratch_shapes=scratch,
        compiler_params=_params(("parallel", "arbitrary")),
        name="ssd_mixer",
    )(z, xbc, dt, conv_state, sst, conv_w, conv_b.reshape(1, CONV_DIM), pad(dt_bias), pad(a_log),
      dx, norm_g.reshape(1, D_INNER), expand)
    return y, ns.reshape(bt, SSM_HEADS, SSM_HEAD_DIM, D_STATE)


def _t5_bucket(dist):
    max_exact = NUM_BUCKETS // 2
    d = jnp.maximum(dist, 0)
    df = jnp.maximum(d, 1).astype(F32)
    large = max_exact + (jnp.log(df / max_exact) / math.log(MAX_DISTANCE / max_exact)
                         * (NUM_BUCKETS - max_exact)).astype(jnp.int32)
    return jnp.where(d < max_exact, d, jnp.minimum(large, NUM_BUCKETS - 1))


def _bias_lookup(tab, dist):
    onehot = jax.nn.one_hot(_t5_bucket(dist), NUM_BUCKETS, dtype=F32)
    return jnp.einsum('...b,bh->...h', onehot, tab.astype(F32), precision=lax.Precision.HIGHEST)


def _swa_prompt_bias(rel_bias, g, dil):
    i = jnp.arange(SWA_STEPS)[:, None]
    j = jnp.arange(2 * SWA_STEPS)[None, :]
    step = SWA_STEPS + i - j
    tab = rel_bias[:, g * SWA_HEADS_PER_GROUP:(g + 1) * SWA_HEADS_PER_GROUP].astype(F32)
    bias = jnp.transpose(_bias_lookup(tab, jnp.clip(step, 0, SWA_STEPS) * dil), (2, 0, 1))
    valid = (step >= 0) & (step <= SWA_STEPS)
    first = valid & (j >= SWA_STEPS)
    both = jnp.stack([jnp.where(first, bias, NEG), jnp.where(valid, bias, NEG)])
    return both.reshape(2, SWA_HEADS_PER_GROUP * SWA_STEPS, 2 * SWA_STEPS)


def _swa_fused_kernel(q_ref, k_ref, v_ref, b_ref, a_ref, qs_ref, ks_ref, vs_ref, oc_ref, lc_ref, ot_ref, lt_ref,
                      *, seq):
    g = pl.program_id(1)
    gd = SWA_GROUP_DIM
    blk = SWA_STEPS
    n_units = seq // blk
    lane_head = lax.broadcasted_iota(jnp.int32, (blk, gd), 1) // HEAD_DIM

    def group_body(gi, dil):
        rows = seq // dil
        nb = rows // blk
        ks_ref[0:blk, :] = jnp.zeros((blk, gd), BF16)
        vs_ref[0:blk, :] = jnp.zeros((blk, gd), BF16)
        for c in range(dil):
            src = pl.ds(c, rows, stride=dil) if dil > 1 else pl.ds(0, rows)
            dst = slice(blk + c * rows, blk + (c + 1) * rows)
            for half in range(gd // LANES):
                lanes = slice(half * LANES, (half + 1) * LANES)
                qs_ref[dst, lanes] = (q_ref[0, half, src, :] * (HEAD_DIM ** -0.5)).astype(BF16)
                ks_ref[dst, lanes] = k_ref[0, half, src, :].astype(BF16)
                vs_ref[dst, lanes] = v_ref[0, half, src, :].astype(BF16)

        def unit(u, carry):
            start = pl.multiple_of(u * blk, blk)
            first = lax.rem(u, nb) == 0
            qv = qs_ref[pl.ds(blk + start, blk), :]
            kk = ks_ref[pl.ds(start, 2 * blk), :]
            vv = vs_ref[pl.ds(start, 2 * blk), :]
            sel = jnp.where(first, 0, 1)
            nh = SWA_HEADS_PER_GROUP
            qh = jnp.concatenate([jnp.where(lane_head == h, qv, jnp.zeros_like(qv)) for h in range(nh)], axis=0)
            s = _dot_nt(qh, kk) + b_ref[gi, sel]
            m = jnp.max(s, axis=-1, keepdims=True)
            p = jnp.exp(s - m)
            l = jnp.sum(p, axis=-1, keepdims=True)
            of = _dot(p.astype(BF16), vv) / l
            lf = jnp.broadcast_to(m + jnp.log(l), (nh * blk, gd))
            o, lse = of[0:blk], lf[0:blk]
            for h in range(1, nh):
                o = jnp.where(lane_head == h, of[h * blk:(h + 1) * blk], o)
                lse = jnp.where(lane_head == h, lf[h * blk:(h + 1) * blk], lse)
            oc_ref[pl.ds(start, blk), :] = o
            lc_ref[pl.ds(start, blk), :] = lse
            return carry

        lax.fori_loop(0, n_units, unit, 0, unroll=2)
        for c in range(dil):
            dst = pl.ds(c, rows, stride=dil) if dil > 1 else pl.ds(0, rows)
            for half in range(gd // LANES):
                lanes = slice(half * LANES, (half + 1) * LANES)
                cb = gi * (gd // LANES) + half
                ot_ref[cb, dst, :] = oc_ref[c * rows:(c + 1) * rows, lanes]
                lt_ref[cb, dst, :] = lc_ref[c * rows:(c + 1) * rows, lanes]

    for gi, (_, dil) in enumerate(SWA_GROUPS):
        pl.when(g == gi)(functools.partial(group_body, gi, dil))

    @pl.when(g == len(SWA_GROUPS) - 1)
    def _():
        def tile(t, carry):
            r = pl.ds(pl.multiple_of(t * blk, blk), blk)
            per = gd // LANES
            for half in range(per):
                ls = [lt_ref[gi * per + half, r, :] for gi in range(len(SWA_GROUPS))]
                m = jnp.maximum(jnp.maximum(ls[0], ls[1]), ls[2])
                es = [jnp.exp(l - m) for l in ls]
                inv = 1.0 / (es[0] + es[1] + es[2])
                for gi in range(len(SWA_GROUPS)):
                    cb = gi * per + half
                    a_ref[0, r, cb * LANES:(cb + 1) * LANES] = (ot_ref[cb, r, :] * es[gi] * inv).astype(BF16)
            return carry

        lax.fori_loop(0, n_units, tile, 0)


def _swa_prompt(q, k, v, rel_bias, bt, seq):
    bias = jnp.stack([_swa_prompt_bias(rel_bias, g, dil) for g, (_, dil) in enumerate(SWA_GROUPS)])
    gd = SWA_GROUP_DIM
    tok = pl.BlockSpec((1, gd // LANES, seq, LANES), lambda b, g: (b, g, 0, 0))
    a = pl.pallas_call(
        functools.partial(_swa_fused_kernel, seq=seq),
        grid=(bt, len(SWA_GROUPS)),
        in_specs=[tok, tok, tok, _resident(bias.shape)],
        out_specs=pl.BlockSpec((1, seq, SWA_DIM), lambda b, g: (b, 0, 0)),
        out_shape=jax.ShapeDtypeStruct((bt, seq, SWA_DIM), BF16),
        scratch_shapes=[pltpu.VMEM((seq + SWA_STEPS, gd), BF16)] * 3
                       + [pltpu.VMEM((seq, gd), F32)] * 2 + [pltpu.VMEM((SWA_DIM // LANES, seq, LANES), F32)] * 2,
        compiler_params=_params(("parallel", "arbitrary")),
        name="swa_prompt",
    )(q, k, v, bias)
    return a.reshape(bt * seq, SWA_DIM)


def _swa_sample_kernel(q_ref, kn_ref, c0_ref, c1_ref, c2_ref, b0_ref, b1_ref, b2_ref, nb_ref,
                       a_ref, o0_ref, o1_ref, o2_ref, *, t_new):
    gd = SWA_GROUP_DIM
    nh = SWA_HEADS_PER_GROUP
    lane_head = lax.broadcasted_iota(jnp.int32, (t_new, gd), 1) // HEAD_DIM
    caches = (c0_ref, c1_ref, c2_ref)
    biases = (b0_ref, b1_ref, b2_ref)
    new_caches = (o0_ref, o1_ref, o2_ref)
    outs, lses = [], []
    for g in range(len(SWA_GROUPS)):
        c_ref = caches[g]
        win = c_ref.shape[2]
        qg = q_ref[0, :, g * gd:(g + 1) * gd] * (HEAD_DIM ** -0.5)
        qs = jnp.concatenate([jnp.where(lane_head == h, qg, 0.0) for h in range(nh)], axis=0).astype(BF16)
        new = kn_ref[0, g]
        s = _dot(qs, c_ref[0, 0:gd, :].astype(BF16)) + biases[g][...]
        sn = _dot(qs, new[0:gd].astype(BF16)) + nb_ref[g]
        m = jnp.maximum(jnp.max(s, axis=-1, keepdims=True), jnp.max(sn, axis=-1, keepdims=True))
        p = jnp.exp(s - m)
        pn = jnp.exp(sn - m)
        l = jnp.sum(p, axis=-1, keepdims=True) + jnp.sum(pn, axis=-1, keepdims=True)
        of = (_dot_nt(p.astype(BF16), c_ref[0, gd:2 * gd, :].astype(BF16))
              + _dot_nt(pn.astype(BF16), new[gd:2 * gd].astype(BF16))) / l
        lf = jnp.broadcast_to(m + jnp.log(l), (nh * t_new, gd))
        o = jnp.zeros((t_new, gd), F32)
        lse = jnp.zeros((t_new, gd), F32)
        for h in range(nh):
            o = jnp.where(lane_head == h, of[h * t_new:(h + 1) * t_new], o)
            lse = jnp.where(lane_head == h, lf[h * t_new:(h + 1) * t_new], lse)
        outs.append(o)
        lses.append(lse)
        rolled = pltpu.roll(c_ref[0], win - t_new, axis=1)
        if win > LANES:
            new_caches[g][0, :, 0:win - LANES] = rolled[:, 0:win - LANES]
        keep = lax.broadcasted_iota(jnp.int32, (2 * gd, LANES), 1) < LANES - t_new
        new_caches[g][0, :, win - LANES:win] = jnp.where(keep, rolled[:, win - LANES:win], new)
    m = jnp.maximum(jnp.maximum(lses[0], lses[1]), lses[2])
    es = [jnp.exp(l - m) for l in lses]
    inv = 1.0 / (es[0] + es[1] + es[2])
    for g in range(len(SWA_GROUPS)):
        a_ref[0, :, g * gd:(g + 1) * gd] = (outs[g] * es[g] * inv).astype(BF16)


def _swa_sample_bias(rel_bias, t_new):
    t = jnp.arange(t_new)
    biases, new_biases = [], []
    for g, (win, dil) in enumerate(SWA_GROUPS):
        tab = rel_bias[:, g * SWA_HEADS_PER_GROUP:(g + 1) * SWA_HEADS_PER_GROUP].astype(F32)
        back = t[:, None] - jnp.arange(win)[None, :]
        ok = (back <= 0) & (back % dil == 0)
        b = jnp.where(ok[..., None], _bias_lookup(tab, win + back), NEG)
        biases.append(jnp.transpose(b, (2, 0, 1)).reshape(SWA_HEADS_PER_GROUP * t_new, win))
        j = jnp.arange(LANES)[None, :] - (LANES - t_new)
        nd = t[:, None] - j
        nok = (j >= 0) & (nd >= 0) & (nd % dil == 0)
        nb = jnp.where(nok[..., None], _bias_lookup(tab, nd), NEG)
        new_biases.append(jnp.transpose(nb, (2, 0, 1)).reshape(SWA_HEADS_PER_GROUP * t_new, LANES))
    return biases, jnp.stack(new_biases)


def _swa_sample(q, k, v, caches, rel_bias, bt, t_new):
    biases, new_bias = _swa_sample_bias(rel_bias, t_new)
    gd = SWA_GROUP_DIM
    tiles = []
    for g in range(len(SWA_GROUPS)):
        kv = jnp.concatenate([k[:, :, g * gd:(g + 1) * gd], v[:, :, g * gd:(g + 1) * gd]], axis=2)
        tiles.append(jnp.pad(jnp.transpose(kv, (0, 2, 1)), ((0, 0), (0, 0), (LANES - t_new, 0))))
    kn = jnp.stack(tiles, axis=1)
    views = [jnp.transpose(c, (0, 2, 3, 4, 1)).reshape(bt, 2 * gd, c.shape[1]) for c in caches]
    cspecs = [pl.BlockSpec((1, 2 * gd, c.shape[2]), lambda b: (b, 0, 0)) for c in views]
    tok = pl.BlockSpec((1, t_new, SWA_DIM), lambda b: (b, 0, 0))
    res = pl.pallas_call(
        functools.partial(_swa_sample_kernel, t_new=t_new),
        grid=(bt,),
        in_specs=[tok, pl.BlockSpec((1, len(SWA_GROUPS), 2 * gd, LANES), lambda b: (b, 0, 0, 0))] + cspecs
                 + [_resident(b.shape) for b in biases] + [_resident(new_bias.shape)],
        out_specs=[tok] + cspecs,
        out_shape=[jax.ShapeDtypeStruct((bt, t_new, SWA_DIM), BF16)]
                  + [jax.ShapeDtypeStruct(c.shape, F32) for c in views],
        compiler_params=_params(("parallel",)),
        name="swa_sample",
    )(q, kn, *views, *biases, new_bias)
    new_caches = [jnp.transpose(c.reshape(bt, 2, SWA_HEADS_PER_GROUP, HEAD_DIM, c.shape[2]), (0, 4, 1, 2, 3))
                  for c in res[1:]]
    return res[0].reshape(bt * t_new, SWA_DIM), new_caches


def _diff_lambda(lam_ref, lam_init):
    lp = lam_ref[...]
    s1 = jnp.sum(lp[0:1] * lp[1:2], axis=-1, keepdims=True)
    s2 = jnp.sum(lp[2:3] * lp[3:4], axis=-1, keepdims=True)
    return jnp.exp(s1) - jnp.exp(s2) + lam_init


def _diff_prompt_kernel(q_ref, k_ref, v_ref, b_ref, lam_ref, on_ref, o_ref, q6_ref, m_ref, acc_ref,
                        *, tq, tk, lam_init):
    i = pl.program_id(2)
    nsub = tk // LANES
    qsub = tq // LANES
    lane_lo = lax.broadcasted_iota(jnp.int32, (tq, LANES), 1) < HEAD_DIM
    for r in range(DIFF_REP):
        qr = q_ref[0, :, r * LANES:(r + 1) * LANES] * (HEAD_DIM ** -0.5)
        q6_ref[(2 * r) * tq:(2 * r + 1) * tq, :] = jnp.where(lane_lo, qr, 0.0).astype(BF16)
        q6_ref[(2 * r + 1) * tq:(2 * r + 2) * tq, :] = jnp.where(lane_lo, 0.0, qr).astype(BF16)
    m_ref[...] = jnp.full(m_ref.shape, NEG, F32)
    acc_ref[...] = jnp.zeros(acc_ref.shape, F32)
    ones = jnp.ones((tk, LANES), BF16)

    def body(j, carry):
        start = pl.multiple_of(j * tk, tk)
        kb = k_ref[0, pl.ds(start, tk), :].astype(BF16)
        vx = jnp.concatenate([v_ref[0, pl.ds(start, tk), :].astype(BF16), ones], axis=1)
        d0 = i * qsub - j * nsub + (nsub - 1)
        for r in range(DIFF_REP):
            rows = slice(2 * r * tq, (2 * r + 2) * tq)
            s = _dot_nt(q6_ref[rows, :], kb)
            bias = jnp.concatenate([b_ref[0, d0 - c, r] for c in range(nsub)], axis=1)
            s = s + jnp.concatenate([bias, bias], axis=0)
            smax = s[:, 0:LANES]
            for c in range(1, tk // LANES):
                smax = jnp.maximum(smax, s[:, c * LANES:(c + 1) * LANES])
            m_old = m_ref[rows, :]
            m_new = jnp.maximum(m_old, jnp.max(smax, axis=-1, keepdims=True))
            alpha = jnp.exp(m_old - m_new)
            p = jnp.concatenate([jnp.exp(s[:, c * LANES:(c + 1) * LANES] - m_new)
                                 for c in range(tk // LANES)], axis=1).astype(BF16)
            acc_ref[rows, :] = jnp.concatenate([alpha, alpha], axis=1) * acc_ref[rows, :] + _dot(p, vx)
            m_ref[rows, :] = m_new
        return carry

    lax.fori_loop(0, ((i + 1) * tq - 1) // tk + 1, body, 0)
    lam = _diff_lambda(lam_ref, lam_init)
    for r in range(DIFF_REP):
        o0 = acc_ref[(2 * r) * tq:(2 * r + 1) * tq, :]
        o1 = acc_ref[(2 * r + 1) * tq:(2 * r + 2) * tq, :]
        a = o0[:, :DIFF_V_DIM] / o0[:, DIFF_V_DIM:] - lam * (o1[:, :DIFF_V_DIM] / o1[:, DIFF_V_DIM:])
        o_ref[0, :, r * LANES:(r + 1) * LANES] = (_rms(a, on_ref[...]) * (1.0 - lam_init)).astype(BF16)


def _diff_prompt_bias(rel_bias, seq, tq, tk):
    nd = seq // LANES + tk // LANES - 1
    delta = jnp.arange(nd) - (tk // LANES - 1)
    d = delta[:, None, None] * LANES + jnp.arange(tq)[None, :, None] - jnp.arange(LANES)[None, None, :]
    tab = jnp.where((d >= 0)[..., None], _bias_lookup(rel_bias, d), NEG)
    tab = jnp.transpose(tab, (3, 0, 1, 2)).reshape(DIFF_KV_HEADS, DIFF_REP, nd, tq, LANES)
    return jnp.transpose(tab, (0, 2, 1, 3, 4))


def _diff_prompt(q, k, v, rel_bias, lam_p, out_norm, lam_init, bt, seq):
    tq, tk = 256, 512
    nb = seq // tq
    bias = _diff_prompt_bias(rel_bias, seq, tq, tk)
    qw = DIFF_REP * 2 * HEAD_DIM
    a = pl.pallas_call(
        functools.partial(_diff_prompt_kernel, tq=tq, tk=tk, lam_init=lam_init),
        grid=(bt, DIFF_KV_HEADS, nb),
        in_specs=[pl.BlockSpec((1, tq, qw), lambda b, g, i: (b, i, g)),
                  pl.BlockSpec((1, seq, LANES), lambda b, g, i: (b, 0, g)),
                  pl.BlockSpec((1, seq, LANES), lambda b, g, i: (b, 0, g)),
                  pl.BlockSpec((1,) + bias.shape[1:], lambda b, g, i: (g, 0, 0, 0, 0)),
                  _resident((4, HEAD_DIM)), _resident((1, DIFF_V_DIM))],
        out_specs=pl.BlockSpec((1, tq, qw), lambda b, g, i: (b, i, g)),
        out_shape=jax.ShapeDtypeStruct((bt, seq, DIFF_Q_DIM), BF16),
        scratch_shapes=[pltpu.VMEM((2 * DIFF_REP * tq, LANES), BF16),
                        pltpu.VMEM((2 * DIFF_REP * tq, LANES), F32),
                        pltpu.VMEM((2 * DIFF_REP * tq, 2 * DIFF_V_DIM), F32)],
        compiler_params=_params(("parallel", "parallel", "arbitrary")),
        name="diff_prompt",
    )(q.reshape(bt, seq, DIFF_Q_DIM), k.reshape(bt, seq, DIFF_K_DIM), v.reshape(bt, seq, DIFF_V_ALL),
      bias, lam_p.astype(F32), out_norm.reshape(1, DIFF_V_DIM).astype(F32))
    return a.reshape(bt * seq, DIFF_Q_DIM)


def _diff_sample_kernel(pt_ref, qb_ref, kn_ref, vn_ref, *refs, t_new, lam_init):
    npg = PAGES_PER_STEP
    k_refs = refs[:npg]
    v_refs = refs[npg:2 * npg]
    b_ref, nb_ref, lam_ref, on_ref, o_ref, m_ref, l_ref, acc_ref = refs[2 * npg:]
    s_idx = pl.program_id(1)
    per_map = DIFF_REP * t_new
    per_g = 2 * per_map

    @pl.when(s_idx == 0)
    def _():
        m_ref[...] = jnp.full(m_ref.shape, NEG, F32)
        l_ref[...] = jnp.zeros(l_ref.shape, F32)
        acc_ref[...] = jnp.zeros(acc_ref.shape, F32)

    qb = (qb_ref[0] * (HEAD_DIM ** -0.5)).astype(BF16)

    def absorb(kt_refs, vv_refs, biases):
        ss = [_dot(qb, kt[0].astype(BF16)) + b for kt, b in zip(kt_refs, biases)]
        smax = ss[0]
        for s in ss[1:]:
            smax = jnp.maximum(smax, s)
        m_old = m_ref[...]
        m_new = jnp.maximum(m_old, jnp.max(smax, axis=-1, keepdims=True))
        alpha = jnp.exp(m_old - m_new)
        ps = [jnp.exp(s - m_new) for s in ss]
        psum = ps[0]
        for p in ps[1:]:
            psum = psum + p
        l_ref[...] = alpha * l_ref[...] + jnp.sum(psum, axis=-1, keepdims=True)
        ps = [p.astype(BF16) for p in ps]
        for g in range(DIFF_KV_HEADS):
            rows = slice(g * per_g, (g + 1) * per_g)
            new = alpha[rows] * acc_ref[rows, :]
            for p, vv in zip(ps, vv_refs):
                vg = vv[0, pl.ds(g, PAGE_SIZE, stride=DIFF_KV_HEADS), :].astype(BF16)
                new = new + _dot(p[rows], vg)
            acc_ref[rows, :] = new
        m_ref[...] = m_new

    absorb(k_refs, v_refs, [b_ref[pp] for pp in range(npg)])

    @pl.when(s_idx == pl.num_programs(1) - 1)
    def _():
        absorb([kn_ref], [vn_ref], [nb_ref[...]])
        lam = _diff_lambda(lam_ref, lam_init)
        o = acc_ref[...] / l_ref[...]
        for g in range(DIFF_KV_HEADS):
            for r in range(DIFF_REP):
                r0 = g * per_g + r * t_new
                blk = o[r0:r0 + t_new] - lam * o[r0 + per_map:r0 + per_map + t_new]
                col = (g * DIFF_REP + r) * DIFF_V_DIM
                o_ref[0, :, col:col + DIFF_V_DIM] = (_rms(blk, on_ref[...]) * (1.0 - lam_init)).astype(BF16)


def _diff_sample(q, k, v, cache_k, cache_v, page_table, rel_bias, lam_p, out_norm, lam_init, bt, t_new):
    n_pages = page_table.shape[1]
    past = n_pages * PAGE_SIZE
    n_phys = cache_k.shape[0]
    nrow = DIFF_KV_HEADS * 2 * DIFF_REP * t_new
    q6 = q.reshape(bt, t_new, DIFF_KV_HEADS, DIFF_REP, 2, HEAD_DIM)
    q6 = jnp.transpose(q6, (0, 2, 4, 3, 1, 5))
    eye_m = jnp.eye(2, dtype=F32)
    eye_g = jnp.eye(DIFF_KV_HEADS, dtype=F32)
    qb = jnp.einsum('bgmrtd,mn,gh->bgmrthnd', q6, eye_m, eye_g).reshape(bt, nrow, DIFF_K_DIM)
    tab = rel_bias.astype(F32)
    tt = jnp.arange(t_new)

    def rows_of(b):
        n = b.shape[1]
        b = jnp.transpose(b, (2, 0, 1)).reshape(DIFF_KV_HEADS, 1, DIFF_REP, t_new, n)
        return jnp.broadcast_to(b, (DIFF_KV_HEADS, 2, DIFF_REP, t_new, n)).reshape(nrow, n)

    dist = past + tt[:, None] - jnp.arange(past)[None, :]
    bias = rows_of(_bias_lookup(tab, dist)).reshape(nrow, n_pages, PAGE_SIZE)
    bias = jnp.transpose(bias, (1, 0, 2))
    nd = tt[:, None] - jnp.arange(PAGE_SIZE)[None, :]
    nbias = rows_of(jnp.where((nd >= 0)[..., None], _bias_lookup(tab, nd), NEG))
    kn = jnp.transpose(k.reshape(bt, t_new, DIFF_K_DIM), (0, 2, 1))
    kn = jnp.pad(kn, ((0, 0), (0, 0), (0, PAGE_SIZE - t_new)))
    vn = jnp.pad(v.reshape(bt, t_new * DIFF_KV_HEADS, DIFF_V_DIM),
                 ((0, 0), (0, (PAGE_SIZE - t_new) * DIFF_KV_HEADS), (0, 0)))
    ck = jnp.transpose(cache_k, (0, 2, 3, 4, 1)).reshape(n_phys, DIFF_K_DIM, PAGE_SIZE)
    cv = cache_v.reshape(n_phys, PAGE_SIZE * DIFF_KV_HEADS, DIFF_V_DIM)
    npg = PAGES_PER_STEP

    def page_spec(pp):
        return pl.BlockSpec((1, DIFF_K_DIM, PAGE_SIZE), lambda b, s, pt: (pt[b, s * npg + pp], 0, 0))

    const = lambda shape: pl.BlockSpec(shape, lambda b, s, pt: (0,) * len(shape))
    grid_spec = pltpu.PrefetchScalarGridSpec(
        num_scalar_prefetch=1,
        grid=(bt, n_pages // npg),
        in_specs=[pl.BlockSpec((1, nrow, DIFF_K_DIM), lambda b, s, pt: (b, 0, 0)),
                  pl.BlockSpec((1, DIFF_K_DIM, PAGE_SIZE), lambda b, s, pt: (b, 0, 0)),
                  pl.BlockSpec((1, DIFF_K_DIM, PAGE_SIZE), lambda b, s, pt: (b, 0, 0))]
                 + [page_spec(pp) for pp in range(npg)] * 2
                 + [pl.BlockSpec((npg, nrow, PAGE_SIZE), lambda b, s, pt: (s, 0, 0)),
                    const((nrow, PAGE_SIZE)), const((4, HEAD_DIM)), const((1, DIFF_V_DIM))],
        out_specs=pl.BlockSpec((1, t_new, DIFF_Q_DIM), lambda b, s, pt: (b, 0, 0)),
        scratch_shapes=[pltpu.VMEM((nrow, LANES), F32), pltpu.VMEM((nrow, LANES), F32),
                        pltpu.VMEM((nrow, DIFF_V_DIM), F32)],
    )
    a = pl.pallas_call(
        functools.partial(_diff_sample_kernel, t_new=t_new, lam_init=lam_init),
        grid_spec=grid_spec,
        out_shape=jax.ShapeDtypeStruct((bt, t_new, DIFF_Q_DIM), BF16),
        compiler_params=_params(("parallel", "arbitrary")),
        name="diff_sample",
    )(page_table, qb, kn, vn, *([ck] * npg), *([cv] * npg), bias, nbias,
      lam_p.astype(F32), out_norm.reshape(1, DIFF_V_DIM).astype(F32))
    return a.reshape(bt * t_new, DIFF_Q_DIM)


def _ffn_weights(w_in, w_out):
    return w_in.astype(BF16), w_out.astype(BF16)


def kernel(x_prompt, x_sample, state_ssm_conv, state_ssm, cache_swa_kv0, cache_swa_kv1, cache_swa_kv2, cache_diff_k, cache_diff_v, page_table, rel_bias, norm_mix, norm_ffn, ffn_w_in, ffn_w_out, ssm_w_in, ssm_conv_w, ssm_conv_b, ssm_dt_bias, ssm_a_log, ssm_d, ssm_norm, ssm_w_out, swa_w_qkv, swa_q_norm, swa_k_norm, swa_w_out, diff_w_qkv, diff_q_norm, diff_k_norm, diff_lambda, diff_out_norm, diff_w_out):
    bp, seq, d = x_prompt.shape
    bs, t_new, _ = x_sample.shape
    mp, ms = bp * seq, bs * t_new
    tm_p, tm_s = 256, ms
    xp = x_prompt.reshape(mp, d)
    xs = x_sample.reshape(ms, d)
    swa_caches = (cache_swa_kv0, cache_swa_kv1, cache_swa_kv2)
    conv_p, conv_s, ssm_p, ssm_s = [], [], [], []
    swa_p = tuple([] for _ in SWA_GROUPS)
    swa_s = tuple([] for _ in SWA_GROUPS)
    dk_p, dk_s, dv_p, dv_s = [], [], [], []
    i_ssd = i_swa = i_diff = 0
    for layer in range(DEPTH):
        kind = layer % 3
        g_mix = norm_mix[layer].astype(F32)
        w1, w2 = _ffn_weights(ffn_w_in[layer], ffn_w_out[layer])
        if kind == 0:
            i = i_ssd
            i_ssd += 1
            w_in = ssm_w_in[i].astype(BF16)
            weights = [w_in[:, :D_INNER], w_in[:, D_INNER:D_INNER + CONV_DIM],
                       jnp.pad(w_in[:, D_INNER + CONV_DIM:], ((0, 0), (0, LANES - SSM_HEADS)))]
            wo = ssm_w_out[i].astype(BF16)
            cw, cb = ssm_conv_w[i].astype(F32), ssm_conv_b[i].astype(F32)
            ssd_args = (cw, cb, ssm_dt_bias[i], ssm_a_log[i], ssm_d[i], ssm_norm[i].astype(F32))
            z, xbc, dt = _norm_proj(xp, g_mix, weights, [None] * 3, tm_p)
            xbc3 = xbc.reshape(bp, seq, CONV_DIM)
            y, ns = _ssd_mixer(z.reshape(bp, seq, D_INNER), xbc3, dt.reshape(bp, seq, LANES),
                               jnp.zeros((bp, CONV_W - 1, CONV_DIM), F32),
                               jnp.zeros((bp, SSM_HEADS, SSM_HEAD_DIM, D_STATE), F32), *ssd_args)
            conv_p.append(xbc3[:, seq - (CONV_W - 1):])
            ssm_p.append(ns)
            ap = y.reshape(mp, D_INNER)
            z, xbc, dt = _norm_proj(xs, g_mix, weights, [None] * 3, tm_s)
            xbc3 = xbc.reshape(bs, t_new, CONV_DIM)
            y, ns = _ssd_mixer(z.reshape(bs, t_new, D_INNER), xbc3, dt.reshape(bs, t_new, LANES),
                               state_ssm_conv[i], state_ssm[i], *ssd_args)
            conv_s.append(xbc3[:, t_new - (CONV_W - 1):])
            ssm_s.append(ns)
            as_ = y.reshape(ms, D_INNER)
        elif kind == 1:
            i = i_swa
            i_swa += 1
            w_qkv = swa_w_qkv[i].astype(BF16)
            weights = [w_qkv[:, :SWA_DIM], w_qkv[:, SWA_DIM:2 * SWA_DIM], w_qkv[:, 2 * SWA_DIM:]]
            gains = [jnp.tile(swa_q_norm[i].astype(F32), N_ATTN_HEADS).reshape(1, SWA_DIM),
                     jnp.tile(swa_k_norm[i].astype(F32), N_ATTN_HEADS).reshape(1, SWA_DIM), None]
            wo = swa_w_out[i].astype(BF16)
            q, kt, k, vt, v = _norm_proj(xp, g_mix, weights, gains, tm_p, emit=("b", "tb", "tb"), seq=seq)
            ap = _swa_prompt(q, k, v, rel_bias, bp, seq)
            for g, (win, dil) in enumerate(SWA_GROUPS):
                keep = min(win, seq)
                sl = slice(g * SWA_GROUP_DIM, (g + 1) * SWA_GROUP_DIM)
                kv_t = jnp.stack([kt[:, sl, seq - keep:], vt[:, sl, seq - keep:]], axis=1)
                kv_t = kv_t.reshape(bp, 2, SWA_HEADS_PER_GROUP, HEAD_DIM, keep)
                swa_p[g].append(jnp.transpose(kv_t, (0, 4, 1, 2, 3)))
            q, k, v = _norm_proj(xs, g_mix, weights, gains, tm_s)
            q3, k3, v3 = (t.reshape(bs, t_new, SWA_DIM) for t in (q, k, v))
            as_, new_bufs = _swa_sample(q3, k3, v3, [c[i] for c in swa_caches], rel_bias, bs, t_new)
            for g in range(len(SWA_GROUPS)):
                swa_s[g].append(new_bufs[g])
        else:
            i = i_diff
            i_diff += 1
            lam_init = 0.8 - 0.6 * math.exp(-0.3 * layer)
            w_qkv = diff_w_qkv[i].astype(BF16)
            weights = [w_qkv[:, :DIFF_Q_DIM], w_qkv[:, DIFF_Q_DIM:DIFF_Q_DIM + DIFF_K_DIM],
                       w_qkv[:, DIFF_Q_DIM + DIFF_K_DIM:]]
            gains = [jnp.tile(diff_q_norm[i].astype(F32), DIFF_Q_DIM // HEAD_DIM).reshape(1, DIFF_Q_DIM),
                     jnp.tile(diff_k_norm[i].astype(F32), DIFF_K_DIM // HEAD_DIM).reshape(1, DIFF_K_DIM), None]
            wo = diff_w_out[i].astype(BF16)
            q, k, kt, v = _norm_proj(xp, g_mix, weights, gains, tm_p, emit=("n", "nt", "n"), seq=seq)
            ap = _diff_prompt(q, k, v, rel_bias, diff_lambda[i], diff_out_norm[i], lam_init, bp, seq)
            dk_p.append(jnp.transpose(kt.reshape(bp, DIFF_KV_HEADS, 2, HEAD_DIM, seq), (0, 4, 1, 2, 3)))
            dv_p.append(v.reshape(bp, seq, DIFF_KV_HEADS, DIFF_V_DIM))
            q, k, v = _norm_proj(xs, g_mix, weights, gains, tm_s)
            as_ = _diff_sample(q, k, v, cache_diff_k[i], cache_diff_v[i], page_table, rel_bias, diff_lambda[i],
                               diff_out_norm[i], lam_init, bs, t_new)
            dk_s.append(k.reshape(bs, t_new, DIFF_KV_HEADS, 2, HEAD_DIM))
            dv_s.append(v.reshape(bs, t_new, DIFF_KV_HEADS, DIFF_V_DIM))
        g_ffn = norm_ffn[layer].astype(F32)
        xp = _mix_ffn(xp, ap, wo, g_ffn, w1, w2, 2 * tm_p)
        xs = _mix_ffn(xs, as_, wo, g_ffn, w1, w2, tm_s)
    return (xp.reshape(bp, seq, d), xs.reshape(bs, t_new, d),
            jnp.stack(conv_p), jnp.stack(conv_s), jnp.stack(ssm_p), jnp.stack(ssm_s),
            jnp.stack(swa_p[0]), jnp.stack(swa_s[0]), jnp.stack(swa_p[1]), jnp.stack(swa_s[1]),
            jnp.stack(swa_p[2]), jnp.stack(swa_s[2]),
            jnp.stack(dk_p), jnp.stack(dk_s), jnp.stack(dv_p), jnp.stack(dv_s))
```

```python
import functools
import math

import jax
import jax.numpy as jnp
from jax import lax
from jax.experimental import pallas as pl
from jax.experimental.pallas import tpu as pltpu

F32 = jnp.float32
BF16 = jnp.bfloat16

D_MODEL = 1024
DEPTH = 4
D_FF = 2816
D_INNER = 2048
SSM_HEADS = 32
SSM_HEAD_DIM = 64
SSM_GROUPS = 4
D_STATE = 128
CONV_W = 4
CONV_DIM = D_INNER + 2 * SSM_GROUPS * D_STATE
SSD_CHUNK = 128
SHORT_CHUNK = 16
HEAD_DIM = 64
N_ATTN_HEADS = 12
NUM_BUCKETS = 32
MAX_DISTANCE = 2048
SWA_GROUPS = ((128, 1), (512, 4), (2048, 16))
SWA_HEADS_PER_GROUP = 4
SWA_GROUP_DIM = SWA_HEADS_PER_GROUP * HEAD_DIM
SWA_DIM = N_ATTN_HEADS * HEAD_DIM
SWA_STEPS = 128
DIFF_KV_HEADS = 4
DIFF_REP = 3
DIFF_V_DIM = 128
DIFF_Q_DIM = N_ATTN_HEADS * 2 * HEAD_DIM
DIFF_K_DIM = DIFF_KV_HEADS * 2 * HEAD_DIM
DIFF_V_ALL = DIFF_KV_HEADS * DIFF_V_DIM
PAGE_SIZE = 128
EPS = 1e-6
NEG = -1e30

LANES = 128
VMEM_LIMIT = 56 * 1024 * 1024
PAGES_PER_STEP = 8


def _dot(a, b):
    return jnp.dot(a, b, preferred_element_type=F32)


def _dot_nt(a, b):
    return lax.dot_general(a, b, (((1,), (1,)), ((), ())), preferred_element_type=F32)


def _split3(x):
    hi = x.astype(BF16)
    r = x - hi.astype(F32)
    mid = r.astype(BF16)
    lo = (r - mid.astype(F32)).astype(BF16)
    return hi, mid, lo


def _dot_exact_rhs(a3_bf16, x):
    return _dot(a3_bf16, jnp.concatenate(_split3(x), axis=0))


def _dot_exact_lhs(x, a3_bf16):
    return _dot(jnp.concatenate(_split3(x), axis=1), a3_bf16)


def _sigmoid(x):
    return 1.0 / (1.0 + jnp.exp(-x))


def _rms(x, g):
    return x * lax.rsqrt(jnp.mean(x * x, axis=-1, keepdims=True) + EPS) * g


def _resident(shape):
    n = len(shape)
    return pl.BlockSpec(shape, lambda *_: (0,) * n, pipeline_mode=pl.Buffered(1))


def _params(sem):
    return pltpu.CompilerParams(dimension_semantics=sem, vmem_limit_bytes=VMEM_LIMIT)


def _norm_proj_kernel(*refs, n_out, head_norm, emit):
    x_ref, g_ref = refs[0], refs[1]
    w_refs = refs[2:2 + n_out]
    n_hn = sum(head_norm)
    hn_refs = refs[2 + n_out:2 + n_out + n_hn]
    pos = 2 + n_out + n_hn
    seg_ref = refs[pos] if n_hn else None
    pos += 1 if n_hn else 0
    o_refs = list(refs[pos:])
    h = _rms(x_ref[...], g_ref[...]).astype(BF16)
    k = 0
    for i in range(n_out):
        y = _dot(h, w_refs[i][...])
        if head_norm[i]:
            gain = hn_refs[k][...]
            k += 1
            seg = seg_ref[...]
            parts = []
            for c in range(y.shape[1] // seg.shape[0]):
                yc = y[:, c * seg.shape[0]:(c + 1) * seg.shape[0]]
                sq = yc * yc
                hi = sq.astype(BF16)
                lo = (sq - hi.astype(F32)).astype(BF16)
                ms = (_dot(hi, seg) + _dot(lo, seg)) * (1.0 / HEAD_DIM)
                parts.append(yc * lax.rsqrt(ms + EPS))
            y = jnp.concatenate(parts, axis=1) * gain
        if "n" in emit[i]:
            o_refs.pop(0)[...] = y
        if "t" in emit[i]:
            o_refs.pop(0)[0] = y.T
        if "b" in emit[i]:
            ref = o_refs.pop(0)
            for cb in range(y.shape[1] // LANES):
                ref[0, cb] = y[:, cb * LANES:(cb + 1) * LANES]


def _norm_proj(x, g, weights, head_gains, tm, emit=None, seq=None):
    m, d = x.shape
    n_out = len(weights)
    emit = tuple(emit or ("n",) * n_out)
    head_norm = tuple(hg is not None for hg in head_gains)
    ins = [x, g.reshape(1, d)] + list(weights)
    specs = [pl.BlockSpec((tm, d), lambda i: (i, 0)), _resident((1, d))]
    specs += [_resident(w.shape) for w in weights]
    for hg in head_gains:
        if hg is not None:
            ins.append(hg)
            specs.append(_resident(hg.shape))
    if any(head_norm):
        r = jnp.arange(2 * LANES)
        seg = (r[:, None] // HEAD_DIM == r[None, :] // HEAD_DIM).astype(BF16)
        ins.append(seg)
        specs.append(_resident(seg.shape))
    per_seq = seq // tm if seq else 1
    out_specs, out_shape = [], []
    for w, e in zip(weights, emit):
        n = w.shape[1]
        if "n" in e:
            out_specs.append(pl.BlockSpec((tm, n), lambda i: (i, 0)))
            out_shape.append(jax.ShapeDtypeStruct((m, n), F32))
        if "t" in e:
            out_specs.append(pl.BlockSpec((1, n, tm), lambda i: (i // per_seq, 0, i % per_seq)))
            out_shape.append(jax.ShapeDtypeStruct((m // seq, n, seq), F32))
        if "b" in e:
            out_specs.append(pl.BlockSpec((1, n // LANES, tm, LANES), lambda i: (i // per_seq, 0, i % per_seq, 0)))
            out_shape.append(jax.ShapeDtypeStruct((m // seq, n // LANES, seq, LANES), F32))
    return pl.pallas_call(
        functools.partial(_norm_proj_kernel, n_out=n_out, head_norm=head_norm, emit=emit),
        grid=(m // tm,),
        in_specs=specs,
        out_specs=out_specs,
        out_shape=out_shape,
        compiler_params=_params(("parallel",)),
        name="norm_proj",
    )(*ins)


def _mix_ffn_kernel(x_ref, a_ref, wo_ref, g_ref, w1_ref, w2_ref, o_ref):
    x1 = x_ref[...] + _dot(a_ref[...], wo_ref[...])
    h = _rms(x1, g_ref[...]).astype(BF16)
    gu = _dot(h, w1_ref[0])
    gate = gu[:, :D_FF]
    act = (gate * _sigmoid(gate) * gu[:, D_FF:]).astype(BF16)
    o_ref[...] = x1 + _dot(act, w2_ref[0])


def _mix_ffn(x, a, wo, g, w1, w2, layer, tm):
    m, d = x.shape
    ka = a.shape[1]
    slab = lambda w: pl.BlockSpec((1,) + w.shape[1:], lambda i: (layer, 0, 0), pipeline_mode=pl.Buffered(1))
    return pl.pallas_call(
        _mix_ffn_kernel,
        grid=(m // tm,),
        in_specs=[pl.BlockSpec((tm, d), lambda i: (i, 0)),
                  pl.BlockSpec((tm, ka), lambda i: (i, 0)),
                  _resident(wo.shape), _resident((1, d)), slab(w1), slab(w2)],
        out_specs=pl.BlockSpec((tm, d), lambda i: (i, 0)),
        out_shape=jax.ShapeDtypeStruct((m, d), F32),
        compiler_params=_params(("parallel",)),
        name="mix_ffn",
    )(x, a, wo, g.reshape(1, d), w1, w2)


def _ssd_kernel(z_ref, xbc_ref, dt_ref, cst_ref, sst_ref, cw_ref, cb_ref, dtb_ref, alog_ref,
                dx_ref, ng_ref, e_ref, y_ref, ns_ref, ext_ref, st_ref, yacc_ref, *pad_refs,
                q, n_valid):
    c = pl.program_id(1)
    last = pl.num_programs(1) - 1
    padded = n_valid < q

    @pl.when(c == 0)
    def _():
        ext_ref[...] = jnp.zeros(ext_ref.shape, F32)
        ext_ref[5:8, :] = cst_ref[0]
        st_ref[...] = sst_ref[0, 0]
        if padded:
            for r in pad_refs:
                r[...] = jnp.zeros(r.shape, F32)

    ext_ref[8:8 + n_valid, :] = xbc_ref[0]
    conv = cb_ref[...]
    for k in range(CONV_W):
        conv = conv + ext_ref[5 + k:5 + k + q, :] * cw_ref[k:k + 1, :]
    if not padded:
        ext_ref[5:8, :] = ext_ref[q + 5:q + 8, :]
    act = conv * _sigmoid(conv)
    xs = act[:, :D_INNER]
    bm = act[:, D_INNER:D_INNER + SSM_GROUPS * D_STATE].astype(BF16)
    cm = act[:, D_INNER + SSM_GROUPS * D_STATE:].astype(BF16)

    if padded:
        zpad_ref, dtpad_ref = pad_refs
        zpad_ref[0:n_valid, :] = z_ref[0]
        dtpad_ref[0:n_valid, :] = dt_ref[0]
        z = zpad_ref[...]
        dt_raw = dtpad_ref[...]
    else:
        z = z_ref[0]
        dt_raw = dt_ref[0]

    row = lax.broadcasted_iota(jnp.int32, (q, q), 0)
    col = lax.broadcasted_iota(jnp.int32, (q, q), 1)
    causal = col <= row
    tri = jnp.where(causal, 1.0, 0.0).astype(BF16)
    tri_t = jnp.where(row <= col, 1.0, 0.0).astype(BF16)
    tri3 = jnp.concatenate([tri] * 3, axis=1)
    tri_t3 = jnp.concatenate([tri_t] * 3, axis=0)

    xdt_in = dt_raw + dtb_ref[...]
    dt = jnp.maximum(xdt_in, 0.0) + jnp.log(1.0 + jnp.exp(-jnp.abs(xdt_in)))
    if padded:
        dt = jnp.where(lax.broadcasted_iota(jnp.int32, dt.shape, 0) < n_valid, dt, 0.0)
    adt = dt * (-jnp.exp(alog_ref[...]))
    acum = _dot_exact_rhs(tri3, adt)
    acum_t = _dot_exact_lhs(adt.T, tri_t3)
    both_x = _dot_exact_lhs(jnp.concatenate([dt, acum], axis=0), e_ref[...])
    dt_x, acum_x = both_x[0:q], both_x[q:2 * q]
    exp_acum_x = jnp.exp(acum_x)
    acum_last_x = acum_x[q - 1:q, :]
    xdt = xs * dt_x
    xdt_b = xdt.astype(BF16)
    xd = xdt * jnp.exp(acum_last_x - acum_x)
    dtot_x = jnp.exp(acum_last_x)
    lane_lo = lax.broadcasted_iota(jnp.int32, (q, LANES), 1) < SSM_HEAD_DIM

    gw = SSM_HEADS // SSM_GROUPS * SSM_HEAD_DIM
    for g in range(SSM_GROUPS):
        cg = cm[:, g * D_STATE:(g + 1) * D_STATE]
        bg = bm[:, g * D_STATE:(g + 1) * D_STATE]
        cb = _dot_nt(cg, bg)
        st_g = st_ref[g * gw:(g + 1) * gw, :]
        yoff = _dot_nt(cg, st_g.astype(BF16))
        yacc_ref[:, g * gw:(g + 1) * gw] = yoff * exp_acum_x[:, g * gw:(g + 1) * gw]
        for pair in range(gw // LANES):
            h0 = g * (gw // SSM_HEAD_DIM) + 2 * pair
            xp = xdt_b[:, h0 * SSM_HEAD_DIM:h0 * SSM_HEAD_DIM + LANES]
            ys = []
            for h in (h0, h0 + 1):
                seg = jnp.broadcast_to(acum[:, h:h + 1], (q, q)) - jnp.broadcast_to(acum_t[h:h + 1, :], (q, q))
                w = cb * jnp.exp(jnp.where(causal, seg, NEG))
                ys.append(_dot(w.astype(BF16), xp))
            yacc_ref[:, h0 * SSM_HEAD_DIM:h0 * SSM_HEAD_DIM + LANES] += jnp.where(lane_lo, ys[0], ys[1])
        xd_t = jnp.concatenate(
            [xd[:, g * gw + t * LANES:g * gw + (t + 1) * LANES].T for t in range(gw // LANES)], axis=0)
        upd = _dot(xd_t.astype(BF16), bg)
        dtot = jnp.concatenate(
            [jnp.broadcast_to(dtot_x[:, g * gw + t * LANES:g * gw + (t + 1) * LANES], (LANES, LANES)).T
             for t in range(gw // LANES)], axis=0)
        st_ref[g * gw:(g + 1) * gw, :] = dtot * st_g + upd

    y = yacc_ref[...] + xs * dx_ref[...]
    gated = y * (z * _sigmoid(z))
    outs = []
    for g in range(SSM_GROUPS):
        gg = gated[:, g * gw:(g + 1) * gw]
        outs.append(gg * lax.rsqrt(jnp.mean(gg * gg, axis=-1, keepdims=True) + EPS))
    yn = jnp.concatenate(outs, axis=1) * ng_ref[...]
    y_ref[0] = yn[0:n_valid].astype(BF16)

    @pl.when(c == last)
    def _():
        ns_ref[0] = st_ref[...]


def _ssd_mixer(z, xbc, dt, conv_state, ssm_states, state_idx, conv_w, conv_b, dt_bias, a_log, d_skip, norm_g):
    bt, seq, _ = z.shape
    q = SSD_CHUNK if seq >= SSD_CHUNK else SHORT_CHUNK
    n_valid = min(q, seq)
    nc = max(1, seq // q)
    pad = lambda v: jnp.pad(v.astype(F32), (0, LANES - SSM_HEADS)).reshape(1, LANES)
    hh = jnp.arange(LANES)[:, None]
    cc = jnp.arange(D_INNER)[None, :] // SSM_HEAD_DIM
    expand = jnp.tile((hh == cc).astype(BF16), (3, 1))
    dx = jnp.repeat(d_skip.astype(F32), SSM_HEAD_DIM).reshape(1, D_INNER)
    sst = ssm_states.reshape(ssm_states.shape[0], bt, SSM_HEADS * SSM_HEAD_DIM, D_STATE)
    scratch = [pltpu.VMEM((q + 8, CONV_DIM), F32),
               pltpu.VMEM((SSM_HEADS * SSM_HEAD_DIM, D_STATE), F32),
               pltpu.VMEM((q, D_INNER), F32)]
    if n_valid < q:
        scratch += [pltpu.VMEM((q, D_INNER), F32), pltpu.VMEM((q, LANES), F32)]
    y, ns = pl.pallas_call(
        functools.partial(_ssd_kernel, q=q, n_valid=n_valid),
        grid=(bt, nc),
        in_specs=[pl.BlockSpec((1, n_valid, D_INNER), lambda b, c: (b, c, 0)),
                  pl.BlockSpec((1, n_valid, CONV_DIM), lambda b, c: (b, c, 0)),
                  pl.BlockSpec((1, n_valid, LANES), lambda b, c: (b, c, 0)),
                  pl.BlockSpec((1, CONV_W - 1, CONV_DIM), lambda b, c: (b, 0, 0)),
                  pl.BlockSpec((1, 1, SSM_HEADS * SSM_HEAD_DIM, D_STATE), lambda b, c: (state_idx, b, 0, 0)),
                  _resident((CONV_W, CONV_DIM)), _resident((1, CONV_DIM)),
                  _resident((1, LANES)), _resident((1, LANES)),
                  _resident((1, D_INNER)), _resident((1, D_INNER)), _resident((3 * LANES, D_INNER))],
        out_specs=[pl.BlockSpec((1, n_valid, D_INNER), lambda b, c: (b, c, 0)),
                   pl.BlockSpec((1, SSM_HEADS * SSM_HEAD_DIM, D_STATE), lambda b, c: (b, 0, 0))],
        out_shape=[jax.ShapeDtypeStruct((bt, seq, D_INNER), BF16),
                   jax.ShapeDtypeStruct((bt, SSM_HEADS * SSM_HEAD_DIM, D_STATE), F32)],
        scratch_shapes=scratch,
        compiler_params=_params(("parallel", "arbitrary")),
        name="ssd_mixer",
    )(z, xbc, dt, conv_state, sst, conv_w, conv_b.reshape(1, CONV_DIM), pad(dt_bias), pad(a_log),
      dx, norm_g.reshape(1, D_INNER), expand)
    return y, ns.reshape(bt, SSM_HEADS, SSM_HEAD_DIM, D_STATE)


def _t5_bucket(dist):
    max_exact = NUM_BUCKETS // 2
    d = jnp.maximum(dist, 0)
    df = jnp.maximum(d, 1).astype(F32)
    large = max_exact + (jnp.log(df / max_exact) / math.log(MAX_DISTANCE / max_exact)
                         * (NUM_BUCKETS - max_exact)).astype(jnp.int32)
    return jnp.where(d < max_exact, d, jnp.minimum(large, NUM_BUCKETS - 1))


def _bias_lookup(tab, dist):
    onehot = jax.nn.one_hot(_t5_bucket(dist), NUM_BUCKETS, dtype=F32)
    return jnp.einsum('...b,bh->...h', onehot, tab.astype(F32), precision=lax.Precision.HIGHEST)


def _swa_prompt_bias(rel_bias, g, dil):
    i = jnp.arange(SWA_STEPS)[:, None]
    j = jnp.arange(2 * SWA_STEPS)[None, :]
    step = SWA_STEPS + i - j
    tab = rel_bias[:, g * SWA_HEADS_PER_GROUP:(g + 1) * SWA_HEADS_PER_GROUP].astype(F32)
    bias = jnp.transpose(_bias_lookup(tab, jnp.clip(step, 0, SWA_STEPS) * dil), (2, 0, 1))
    valid = (step >= 0) & (step <= SWA_STEPS)
    first = valid & (j >= SWA_STEPS)
    both = jnp.stack([jnp.where(first, bias, NEG), jnp.where(valid, bias, NEG)])
    return both.reshape(2, SWA_HEADS_PER_GROUP * SWA_STEPS, 2 * SWA_STEPS)


def _swa_fused_kernel(q_ref, k_ref, v_ref, b_ref, a_ref, qs_ref, ks_ref, vs_ref, oc_ref, lc_ref, ot_ref, lt_ref,
                      *, seq):
    g = pl.program_id(1)
    gd = SWA_GROUP_DIM
    blk = SWA_STEPS
    n_units = seq // blk
    lane_head = lax.broadcasted_iota(jnp.int32, (blk, gd), 1) // HEAD_DIM

    def group_body(gi, dil):
        rows = seq // dil
        nb = rows // blk
        ks_ref[0:blk, :] = jnp.zeros((blk, gd), BF16)
        vs_ref[0:blk, :] = jnp.zeros((blk, gd), BF16)
        for c in range(dil):
            src = pl.ds(c, rows, stride=dil) if dil > 1 else pl.ds(0, rows)
            dst = slice(blk + c * rows, blk + (c + 1) * rows)
            for half in range(gd // LANES):
                lanes = slice(half * LANES, (half + 1) * LANES)
                qs_ref[dst, lanes] = (q_ref[0, half, src, :] * (HEAD_DIM ** -0.5)).astype(BF16)
                ks_ref[dst, lanes] = k_ref[0, half, src, :].astype(BF16)
                vs_ref[dst, lanes] = v_ref[0, half, src, :].astype(BF16)

        def unit(u, carry):
            start = pl.multiple_of(u * blk, blk)
            first = lax.rem(u, nb) == 0
            qv = qs_ref[pl.ds(blk + start, blk), :]
            kk = ks_ref[pl.ds(start, 2 * blk), :]
            vv = vs_ref[pl.ds(start, 2 * blk), :]
            sel = jnp.where(first, 0, 1)
            nh = SWA_HEADS_PER_GROUP
            qh = jnp.concatenate([jnp.where(lane_head == h, qv, jnp.zeros_like(qv)) for h in range(nh)], axis=0)
            s = _dot_nt(qh, kk) + b_ref[gi, sel]
            m = jnp.max(s, axis=-1, keepdims=True)
            p = jnp.exp(s - m)
            l = jnp.sum(p, axis=-1, keepdims=True)
            of = _dot(p.astype(BF16), vv) / l
            lf = jnp.broadcast_to(m + jnp.log(l), (nh * blk, gd))
            o, lse = of[0:blk], lf[0:blk]
            for h in range(1, nh):
                o = jnp.where(lane_head == h, of[h * blk:(h + 1) * blk], o)
                lse = jnp.where(lane_head == h, lf[h * blk:(h + 1) * blk], lse)
            oc_ref[pl.ds(start, blk), :] = o
            lc_ref[pl.ds(start, blk), :] = lse
            return carry

        lax.fori_loop(0, n_units, unit, 0, unroll=2)
        for c in range(dil):
            dst = pl.ds(c, rows, stride=dil) if dil > 1 else pl.ds(0, rows)
            for half in range(gd // LANES):
                lanes = slice(half * LANES, (half + 1) * LANES)
                cb = gi * (gd // LANES) + half
                ot_ref[cb, dst, :] = oc_ref[c * rows:(c + 1) * rows, lanes]
                lt_ref[cb, dst, :] = lc_ref[c * rows:(c + 1) * rows, lanes]

    for gi, (_, dil) in enumerate(SWA_GROUPS):
        pl.when(g == gi)(functools.partial(group_body, gi, dil))

    @pl.when(g == len(SWA_GROUPS) - 1)
    def _():
        def tile(t, carry):
            r = pl.ds(pl.multiple_of(t * blk, blk), blk)
            per = gd // LANES
            for half in range(per):
                ls = [lt_ref[gi * per + half, r, :] for gi in range(len(SWA_GROUPS))]
                m = jnp.maximum(jnp.maximum(ls[0], ls[1]), ls[2])
                es = [jnp.exp(l - m) for l in ls]
                inv = 1.0 / (es[0] + es[1] + es[2])
                for gi in range(len(SWA_GROUPS)):
                    cb = gi * per + half
                    a_ref[0, r, cb * LANES:(cb + 1) * LANES] = (ot_ref[cb, r, :] * es[gi] * inv).astype(BF16)
            return carry

        lax.fori_loop(0, n_units, tile, 0)


def _swa_prompt(q, k, v, rel_bias, bt, seq):
    bias = jnp.stack([_swa_prompt_bias(rel_bias, g, dil) for g, (_, dil) in enumerate(SWA_GROUPS)])
    gd = SWA_GROUP_DIM
    tok = pl.BlockSpec((1, gd // LANES, seq, LANES), lambda b, g: (b, g, 0, 0))
    a = pl.pallas_call(
        functools.partial(_swa_fused_kernel, seq=seq),
        grid=(bt, len(SWA_GROUPS)),
        in_specs=[tok, tok, tok, _resident(bias.shape)],
        out_specs=pl.BlockSpec((1, seq, SWA_DIM), lambda b, g: (b, 0, 0)),
        out_shape=jax.ShapeDtypeStruct((bt, seq, SWA_DIM), BF16),
        scratch_shapes=[pltpu.VMEM((seq + SWA_STEPS, gd), BF16)] * 3
                       + [pltpu.VMEM((seq, gd), F32)] * 2 + [pltpu.VMEM((SWA_DIM // LANES, seq, LANES), F32)] * 2,
        compiler_params=_params(("parallel", "arbitrary")),
        name="swa_prompt",
    )(q, k, v, bias)
    return a.reshape(bt * seq, SWA_DIM)


def _swa_sample_kernel(q_ref, kn_ref, c0_ref, c1_ref, c2_ref, b0_ref, b1_ref, b2_ref, nb_ref,
                       a_ref, o0_ref, o1_ref, o2_ref, *, t_new):
    gd = SWA_GROUP_DIM
    nh = SWA_HEADS_PER_GROUP
    lane_head = lax.broadcasted_iota(jnp.int32, (t_new, gd), 1) // HEAD_DIM
    caches = (c0_ref, c1_ref, c2_ref)
    biases = (b0_ref, b1_ref, b2_ref)
    new_caches = (o0_ref, o1_ref, o2_ref)
    outs, lses = [], []
    for g in range(len(SWA_GROUPS)):
        c_ref = caches[g]
        win = c_ref.shape[2]
        qg = q_ref[0, :, g * gd:(g + 1) * gd] * (HEAD_DIM ** -0.5)
        qs = jnp.concatenate([jnp.where(lane_head == h, qg, 0.0) for h in range(nh)], axis=0).astype(BF16)
        new = kn_ref[0, g]
        s = _dot(qs, c_ref[0, 0:gd, :].astype(BF16)) + biases[g][...]
        sn = _dot(qs, new[0:gd].astype(BF16)) + nb_ref[g]
        m = jnp.maximum(jnp.max(s, axis=-1, keepdims=True), jnp.max(sn, axis=-1, keepdims=True))
        p = jnp.exp(s - m)
        pn = jnp.exp(sn - m)
        l = jnp.sum(p, axis=-1, keepdims=True) + jnp.sum(pn, axis=-1, keepdims=True)
        of = (_dot_nt(p.astype(BF16), c_ref[0, gd:2 * gd, :].astype(BF16))
              + _dot_nt(pn.astype(BF16), new[gd:2 * gd].astype(BF16))) / l
        lf = jnp.broadcast_to(m + jnp.log(l), (nh * t_new, gd))
        o = jnp.zeros((t_new, gd), F32)
        lse = jnp.zeros((t_new, gd), F32)
        for h in range(nh):
            o = jnp.where(lane_head == h, of[h * t_new:(h + 1) * t_new], o)
            lse = jnp.where(lane_head == h, lf[h * t_new:(h + 1) * t_new], lse)
        outs.append(o)
        lses.append(lse)
        rolled = pltpu.roll(c_ref[0], win - t_new, axis=1)
        if win > LANES:
            new_caches[g][0, :, 0:win - LANES] = rolled[:, 0:win - LANES]
        keep = lax.broadcasted_iota(jnp.int32, (2 * gd, LANES), 1) < LANES - t_new
        new_caches[g][0, :, win - LANES:win] = jnp.where(keep, rolled[:, win - LANES:win], new)
    m = jnp.maximum(jnp.maximum(lses[0], lses[1]), lses[2])
    es = [jnp.exp(l - m) for l in lses]
    inv = 1.0 / (es[0] + es[1] + es[2])
    for g in range(len(SWA_GROUPS)):
        a_ref[0, :, g * gd:(g + 1) * gd] = (outs[g] * es[g] * inv).astype(BF16)


def _swa_sample_bias(rel_bias, t_new):
    t = jnp.arange(t_new)
    biases, new_biases = [], []
    for g, (win, dil) in enumerate(SWA_GROUPS):
        tab = rel_bias[:, g * SWA_HEADS_PER_GROUP:(g + 1) * SWA_HEADS_PER_GROUP].astype(F32)
        back = t[:, None] - jnp.arange(win)[None, :]
        ok = (back <= 0) & (back % dil == 0)
        b = jnp.where(ok[..., None], _bias_lookup(tab, win + back), NEG)
        biases.append(jnp.transpose(b, (2, 0, 1)).reshape(SWA_HEADS_PER_GROUP * t_new, win))
        j = jnp.arange(LANES)[None, :] - (LANES - t_new)
        nd = t[:, None] - j
        nok = (j >= 0) & (nd >= 0) & (nd % dil == 0)
        nb = jnp.where(nok[..., None], _bias_lookup(tab, nd), NEG)
        new_biases.append(jnp.transpose(nb, (2, 0, 1)).reshape(SWA_HEADS_PER_GROUP * t_new, LANES))
    return biases, jnp.stack(new_biases)


def _swa_sample(q, k, v, caches, rel_bias, bt, t_new):
    biases, new_bias = _swa_sample_bias(rel_bias, t_new)
    gd = SWA_GROUP_DIM
    tiles = []
    for g in range(len(SWA_GROUPS)):
        kv = jnp.concatenate([k[:, :, g * gd:(g + 1) * gd], v[:, :, g * gd:(g + 1) * gd]], axis=2)
        tiles.append(jnp.pad(jnp.transpose(kv, (0, 2, 1)), ((0, 0), (0, 0), (LANES - t_new, 0))))
    kn = jnp.stack(tiles, axis=1)
    views = [jnp.transpose(c, (0, 2, 3, 4, 1)).reshape(bt, 2 * gd, c.shape[1]) for c in caches]
    cspecs = [pl.BlockSpec((1, 2 * gd, c.shape[2]), lambda b: (b, 0, 0)) for c in views]
    tok = pl.BlockSpec((1, t_new, SWA_DIM), lambda b: (b, 0, 0))
    res = pl.pallas_call(
        functools.partial(_swa_sample_kernel, t_new=t_new),
        grid=(bt,),
        in_specs=[tok, pl.BlockSpec((1, len(SWA_GROUPS), 2 * gd, LANES), lambda b: (b, 0, 0, 0))] + cspecs
                 + [_resident(b.shape) for b in biases] + [_resident(new_bias.shape)],
        out_specs=[tok] + cspecs,
        out_shape=[jax.ShapeDtypeStruct((bt, t_new, SWA_DIM), BF16)]
                  + [jax.ShapeDtypeStruct(c.shape, F32) for c in views],
        compiler_params=_params(("parallel",)),
        name="swa_sample",
    )(q, kn, *views, *biases, new_bias)
    new_caches = [jnp.transpose(c.reshape(bt, 2, SWA_HEADS_PER_GROUP, HEAD_DIM, c.shape[2]), (0, 4, 1, 2, 3))
                  for c in res[1:]]
    return res[0].reshape(bt * t_new, SWA_DIM), new_caches


def _diff_lambda(lam_ref, lam_init):
    lp = lam_ref[...]
    s1 = jnp.sum(lp[0:1] * lp[1:2], axis=-1, keepdims=True)
    s2 = jnp.sum(lp[2:3] * lp[3:4], axis=-1, keepdims=True)
    return jnp.exp(s1) - jnp.exp(s2) + lam_init


def _diff_prompt_kernel(q_ref, k_ref, v_ref, b_ref, lam_ref, on_ref, o_ref, q6_ref, m_ref, acc_ref,
                        *, tq, tk, lam_init):
    i = pl.program_id(2)
    nsub = tk // LANES
    qsub = tq // LANES
    lane_lo = lax.broadcasted_iota(jnp.int32, (tq, LANES), 1) < HEAD_DIM
    for r in range(DIFF_REP):
        qr = q_ref[0, :, r * LANES:(r + 1) * LANES] * (HEAD_DIM ** -0.5)
        q6_ref[(2 * r) * tq:(2 * r + 1) * tq, :] = jnp.where(lane_lo, qr, 0.0).astype(BF16)
        q6_ref[(2 * r + 1) * tq:(2 * r + 2) * tq, :] = jnp.where(lane_lo, 0.0, qr).astype(BF16)
    m_ref[...] = jnp.full(m_ref.shape, NEG, F32)
    acc_ref[...] = jnp.zeros(acc_ref.shape, F32)
    ones = jnp.ones((tk, LANES), BF16)

    def body(j, carry):
        start = pl.multiple_of(j * tk, tk)
        kb = k_ref[0, pl.ds(start, tk), :].astype(BF16)
        vx = jnp.concatenate([v_ref[0, pl.ds(start, tk), :].astype(BF16), ones], axis=1)
        d0 = i * qsub - j * nsub + (nsub - 1)
        for r in range(DIFF_REP):
            rows = slice(2 * r * tq, (2 * r + 2) * tq)
            s = _dot_nt(q6_ref[rows, :], kb)
            bias = jnp.concatenate(
                [jnp.concatenate([b_ref[0, d0 - c + qs, r] for qs in range(qsub)], axis=0) for c in range(nsub)],
                axis=1)
            s = s + jnp.concatenate([bias, bias], axis=0)
            smax = s[:, 0:LANES]
            for c in range(1, tk // LANES):
                smax = jnp.maximum(smax, s[:, c * LANES:(c + 1) * LANES])
            m_old = m_ref[rows, :]
            m_new = jnp.maximum(m_old, jnp.max(smax, axis=-1, keepdims=True))
            alpha = jnp.exp(m_old - m_new)
            p = jnp.concatenate([jnp.exp(s[:, c * LANES:(c + 1) * LANES] - m_new)
                                 for c in range(tk // LANES)], axis=1).astype(BF16)
            acc_ref[rows, :] = jnp.concatenate([alpha, alpha], axis=1) * acc_ref[rows, :] + _dot(p, vx)
            m_ref[rows, :] = m_new
        return carry

    lax.fori_loop(0, ((i + 1) * tq - 1) // tk + 1, body, 0)
    lam = _diff_lambda(lam_ref, lam_init)
    for r in range(DIFF_REP):
        o0 = acc_ref[(2 * r) * tq:(2 * r + 1) * tq, :]
        o1 = acc_ref[(2 * r + 1) * tq:(2 * r + 2) * tq, :]
        a = o0[:, :DIFF_V_DIM] / o0[:, DIFF_V_DIM:] - lam * (o1[:, :DIFF_V_DIM] / o1[:, DIFF_V_DIM:])
        o_ref[0, :, r * LANES:(r + 1) * LANES] = (_rms(a, on_ref[...]) * (1.0 - lam_init)).astype(BF16)


def _diff_prompt_bias(rel_bias, seq, tk):
    nd = seq // LANES + tk // LANES - 1
    delta = jnp.arange(nd) - (tk // LANES - 1)
    d = delta[:, None, None] * LANES + jnp.arange(LANES)[None, :, None] - jnp.arange(LANES)[None, None, :]
    tab = jnp.where((d >= 0)[..., None], _bias_lookup(rel_bias, d), NEG)
    tab = jnp.transpose(tab, (3, 0, 1, 2)).reshape(DIFF_KV_HEADS, DIFF_REP, nd, LANES, LANES)
    return jnp.transpose(tab, (0, 2, 1, 3, 4))


def _diff_prompt(q, k, v, rel_bias, lam_p, out_norm, lam_init, bt, seq):
    tq, tk = 256, 512
    nb = seq // tq
    bias = _diff_prompt_bias(rel_bias, seq, tk)
    qw = DIFF_REP * 2 * HEAD_DIM
    a = pl.pallas_call(
        functools.partial(_diff_prompt_kernel, tq=tq, tk=tk, lam_init=lam_init),
        grid=(bt, DIFF_KV_HEADS, nb),
        in_specs=[pl.BlockSpec((1, tq, qw), lambda b, g, i: (b, i, g)),
                  pl.BlockSpec((1, seq, LANES), lambda b, g, i: (b, 0, g)),
                  pl.BlockSpec((1, seq, LANES), lambda b, g, i: (b, 0, g)),
                  pl.BlockSpec((1,) + bias.shape[1:], lambda b, g, i: (g, 0, 0, 0, 0)),
                  _resident((4, HEAD_DIM)), _resident((1, DIFF_V_DIM))],
        out_specs=pl.BlockSpec((1, tq, qw), lambda b, g, i: (b, i, g)),
        out_shape=jax.ShapeDtypeStruct((bt, seq, DIFF_Q_DIM), BF16),
        scratch_shapes=[pltpu.VMEM((2 * DIFF_REP * tq, LANES), BF16),
                        pltpu.VMEM((2 * DIFF_REP * tq, LANES), F32),
                        pltpu.VMEM((2 * DIFF_REP * tq, 2 * DIFF_V_DIM), F32)],
        compiler_params=_params(("parallel", "parallel", "arbitrary")),
        name="diff_prompt",
    )(q.reshape(bt, seq, DIFF_Q_DIM), k.reshape(bt, seq, DIFF_K_DIM), v.reshape(bt, seq, DIFF_V_ALL),
      bias, lam_p.astype(F32), out_norm.reshape(1, DIFF_V_DIM).astype(F32))
    return a.reshape(bt * seq, DIFF_Q_DIM)


def _diff_sample_kernel(pt_ref, qb_ref, kn_ref, vn_ref, *refs, t_new, lam_init):
    npg = PAGES_PER_STEP
    k_refs = refs[:npg]
    v_refs = refs[npg:2 * npg]
    b_ref, nb_ref, lam_ref, on_ref, o_ref, m_ref, l_ref, acc_ref = refs[2 * npg:]
    s_idx = pl.program_id(1)
    per_map = DIFF_REP * t_new
    per_g = 2 * per_map

    @pl.when(s_idx == 0)
    def _():
        m_ref[...] = jnp.full(m_ref.shape, NEG, F32)
        l_ref[...] = jnp.zeros(l_ref.shape, F32)
        acc_ref[...] = jnp.zeros(acc_ref.shape, F32)

    qb = (qb_ref[0] * (HEAD_DIM ** -0.5)).astype(BF16)

    def absorb(kt_refs, vv_refs, bias):
        kcat = jnp.concatenate([kt[0].astype(BF16) for kt in kt_refs], axis=1)
        s = _dot(qb, kcat) + bias
        chunks = [s[:, c * LANES:(c + 1) * LANES] for c in range(len(kt_refs))]
        smax = chunks[0]
        for c in chunks[1:]:
            smax = jnp.maximum(smax, c)
        m_old = m_ref[...]
        m_new = jnp.maximum(m_old, jnp.max(smax, axis=-1, keepdims=True))
        alpha = jnp.exp(m_old - m_new)
        ps = [jnp.exp(c - m_new) for c in chunks]
        psum = ps[0]
        for p in ps[1:]:
            psum = psum + p
        l_ref[...] = alpha * l_ref[...] + jnp.sum(psum, axis=-1, keepdims=True)
        pcat = jnp.concatenate(ps, axis=1).astype(BF16)
        for g in range(DIFF_KV_HEADS):
            rows = slice(g * per_g, (g + 1) * per_g)
            vcat = jnp.concatenate([vv[0, pl.ds(g, PAGE_SIZE, stride=DIFF_KV_HEADS), :].astype(BF16)
                                    for vv in vv_refs], axis=0)
            acc_ref[rows, :] = alpha[rows] * acc_ref[rows, :] + _dot(pcat[rows], vcat)
        m_ref[...] = m_new

    absorb(k_refs, v_refs, b_ref[0])

    @pl.when(s_idx == pl.num_programs(1) - 1)
    def _():
        absorb([kn_ref], [vn_ref], nb_ref[...])
        lam = _diff_lambda(lam_ref, lam_init)
        o = acc_ref[...] / l_ref[...]
        for g in range(DIFF_KV_HEADS):
            for r in range(DIFF_REP):
                r0 = g * per_g + r * t_new
                blk = o[r0:r0 + t_new] - lam * o[r0 + per_map:r0 + per_map + t_new]
                col = (g * DIFF_REP + r) * DIFF_V_DIM
                o_ref[0, :, col:col + DIFF_V_DIM] = (_rms(blk, on_ref[...]) * (1.0 - lam_init)).astype(BF16)


def _diff_sample(q, k, v, cache_k, cache_v, page_table, rel_bias, lam_p, out_norm, lam_init, bt, t_new):
    n_pages = page_table.shape[1]
    past = n_pages * PAGE_SIZE
    n_phys = cache_k.shape[0]
    nrow = DIFF_KV_HEADS * 2 * DIFF_REP * t_new
    q6 = q.reshape(bt, t_new, DIFF_KV_HEADS, DIFF_REP, 2, HEAD_DIM)
    q6 = jnp.transpose(q6, (0, 2, 4, 3, 1, 5))
    eye_m = jnp.eye(2, dtype=F32)
    eye_g = jnp.eye(DIFF_KV_HEADS, dtype=F32)
    qb = jnp.einsum('bgmrtd,mn,gh->bgmrthnd', q6, eye_m, eye_g).reshape(bt, nrow, DIFF_K_DIM)
    tab = rel_bias.astype(F32)
    tt = jnp.arange(t_new)

    def rows_of(b):
        n = b.shape[1]
        b = jnp.transpose(b, (2, 0, 1)).reshape(DIFF_KV_HEADS, 1, DIFF_REP, t_new, n)
        return jnp.broadcast_to(b, (DIFF_KV_HEADS, 2, DIFF_REP, t_new, n)).reshape(nrow, n)

    dist = past + tt[:, None] - jnp.arange(past)[None, :]
    npg = PAGES_PER_STEP
    bias = rows_of(_bias_lookup(tab, dist)).reshape(nrow, n_pages // npg, npg * PAGE_SIZE)
    bias = jnp.transpose(bias, (1, 0, 2))
    nd = tt[:, None] - jnp.arange(PAGE_SIZE)[None, :]
    nbias = rows_of(jnp.where((nd >= 0)[..., None], _bias_lookup(tab, nd), NEG))
    kn = jnp.transpose(k.reshape(bt, t_new, DIFF_K_DIM), (0, 2, 1))
    kn = jnp.pad(kn, ((0, 0), (0, 0), (0, PAGE_SIZE - t_new)))
    vn = jnp.pad(v.reshape(bt, t_new * DIFF_KV_HEADS, DIFF_V_DIM),
                 ((0, 0), (0, (PAGE_SIZE - t_new) * DIFF_KV_HEADS), (0, 0)))
    ck = jnp.transpose(cache_k, (0, 2, 3, 4, 1)).reshape(n_phys, DIFF_K_DIM, PAGE_SIZE)
    cv = cache_v.reshape(n_phys, PAGE_SIZE * DIFF_KV_HEADS, DIFF_V_DIM)

    def page_spec(pp):
        return pl.BlockSpec((1, DIFF_K_DIM, PAGE_SIZE), lambda b, s, pt: (pt[b, s * npg + pp], 0, 0))

    const = lambda shape: pl.BlockSpec(shape, lambda b, s, pt: (0,) * len(shape))
    grid_spec = pltpu.PrefetchScalarGridSpec(
        num_scalar_prefetch=1,
        grid=(bt, n_pages // npg),
        in_specs=[pl.BlockSpec((1, nrow, DIFF_K_DIM), lambda b, s, pt: (b, 0, 0)),
                  pl.BlockSpec((1, DIFF_K_DIM, PAGE_SIZE), lambda b, s, pt: (b, 0, 0)),
                  pl.BlockSpec((1, DIFF_K_DIM, PAGE_SIZE), lambda b, s, pt: (b, 0, 0))]
                 + [page_spec(pp) for pp in range(npg)] * 2
                 + [pl.BlockSpec((1, nrow, npg * PAGE_SIZE), lambda b, s, pt: (s, 0, 0)),
                    const((nrow, PAGE_SIZE)), const((4, HEAD_DIM)), const((1, DIFF_V_DIM))],
        out_specs=pl.BlockSpec((1, t_new, DIFF_Q_DIM), lambda b, s, pt: (b, 0, 0)),
        scratch_shapes=[pltpu.VMEM((nrow, LANES), F32), pltpu.VMEM((nrow, LANES), F32),
                        pltpu.VMEM((nrow, DIFF_V_DIM), F32)],
    )
    a = pl.pallas_call(
        functools.partial(_diff_sample_kernel, t_new=t_new, lam_init=lam_init),
        grid_spec=grid_spec,
        out_shape=jax.ShapeDtypeStruct((bt, t_new, DIFF_Q_DIM), BF16),
        compiler_params=_params(("parallel", "arbitrary")),
        name="diff_sample",
    )(page_table, qb, kn, vn, *([ck] * npg), *([cv] * npg), bias, nbias,
      lam_p.astype(F32), out_norm.reshape(1, DIFF_V_DIM).astype(F32))
    return a.reshape(bt * t_new, DIFF_Q_DIM)


def kernel(x_prompt, x_sample, state_ssm_conv, state_ssm, cache_swa_kv0, cache_swa_kv1, cache_swa_kv2, cache_diff_k, cache_diff_v, page_table, rel_bias, norm_mix, norm_ffn, ffn_w_in, ffn_w_out, ssm_w_in, ssm_conv_w, ssm_conv_b, ssm_dt_bias, ssm_a_log, ssm_d, ssm_norm, ssm_w_out, swa_w_qkv, swa_q_norm, swa_k_norm, swa_w_out, diff_w_qkv, diff_q_norm, diff_k_norm, diff_lambda, diff_out_norm, diff_w_out):
    bp, seq, d = x_prompt.shape
    bs, t_new, _ = x_sample.shape
    mp, ms = bp * seq, bs * t_new
    tm_p, tm_s = 256, ms
    xp = x_prompt.reshape(mp, d)
    xs = x_sample.reshape(ms, d)
    swa_caches = (cache_swa_kv0, cache_swa_kv1, cache_swa_kv2)
    conv_p, conv_s, ssm_p, ssm_s = [], [], [], []
    swa_p = tuple([] for _ in SWA_GROUPS)
    swa_s = tuple([] for _ in SWA_GROUPS)
    dk_p, dk_s, dv_p, dv_s = [], [], [], []
    i_ssd = i_swa = i_diff = 0
    w1, w2 = ffn_w_in.astype(BF16), ffn_w_out.astype(BF16)
    for layer in range(DEPTH):
        kind = layer % 3
        g_mix = norm_mix[layer].astype(F32)
        if kind == 0:
            i = i_ssd
            i_ssd += 1
            w_in = ssm_w_in[i].astype(BF16)
            weights = [w_in[:, :D_INNER], w_in[:, D_INNER:D_INNER + CONV_DIM],
                       jnp.pad(w_in[:, D_INNER + CONV_DIM:], ((0, 0), (0, LANES - SSM_HEADS)))]
            wo = ssm_w_out[i].astype(BF16)
            cw, cb = ssm_conv_w[i].astype(F32), ssm_conv_b[i].astype(F32)
            ssd_args = (cw, cb, ssm_dt_bias[i], ssm_a_log[i], ssm_d[i], ssm_norm[i].astype(F32))
            z, xbc, dt = _norm_proj(xp, g_mix, weights, [None] * 3, tm_p)
            xbc3 = xbc.reshape(bp, seq, CONV_DIM)
            y, ns = _ssd_mixer(z.reshape(bp, seq, D_INNER), xbc3, dt.reshape(bp, seq, LANES),
                               jnp.zeros((bp, CONV_W - 1, CONV_DIM), F32),
                               jnp.zeros((1, bp, SSM_HEADS, SSM_HEAD_DIM, D_STATE), F32), 0, *ssd_args)
            conv_p.append(xbc3[:, seq - (CONV_W - 1):])
            ssm_p.append(ns)
            ap = y.reshape(mp, D_INNER)
            z, xbc, dt = _norm_proj(xs, g_mix, weights, [None] * 3, tm_s)
            xbc3 = xbc.reshape(bs, t_new, CONV_DIM)
            y, ns = _ssd_mixer(z.reshape(bs, t_new, D_INNER), xbc3, dt.reshape(bs, t_new, LANES),
                               state_ssm_conv[i], state_ssm, i, *ssd_args)
            conv_s.append(xbc3[:, t_new - (CONV_W - 1):])
            ssm_s.append(ns)
            as_ = y.reshape(ms, D_INNER)
        elif kind == 1:
            i = i_swa
            i_swa += 1
            w_qkv = swa_w_qkv[i].astype(BF16)
            weights = [w_qkv[:, :SWA_DIM], w_qkv[:, SWA_DIM:2 * SWA_DIM], w_qkv[:, 2 * SWA_DIM:]]
            gains = [jnp.tile(swa_q_norm[i].astype(F32), N_ATTN_HEADS).reshape(1, SWA_DIM),
                     jnp.tile(swa_k_norm[i].astype(F32), N_ATTN_HEADS).reshape(1, SWA_DIM), None]
            wo = swa_w_out[i].astype(BF16)
            q, kt, k, vt, v = _norm_proj(xp, g_mix, weights, gains, tm_p, emit=("b", "tb", "tb"), seq=seq)
            ap = _swa_prompt(q, k, v, rel_bias, bp, seq)
            for g, (win, dil) in enumerate(SWA_GROUPS):
                keep = min(win, seq)
                sl = slice(g * SWA_GROUP_DIM, (g + 1) * SWA_GROUP_DIM)
                kv_t = jnp.stack([kt[:, sl, seq - keep:], vt[:, sl, seq - keep:]], axis=1)
                kv_t = kv_t.reshape(bp, 2, SWA_HEADS_PER_GROUP, HEAD_DIM, keep)
                swa_p[g].append(jnp.transpose(kv_t, (0, 4, 1, 2, 3)))
            q, k, v = _norm_proj(xs, g_mix, weights, gains, tm_s)
            q3, k3, v3 = (t.reshape(bs, t_new, SWA_DIM) for t in (q, k, v))
            as_, new_bufs = _swa_sample(q3, k3, v3, [c[i] for c in swa_caches], rel_bias, bs, t_new)
            for g in range(len(SWA_GROUPS)):
                swa_s[g].append(new_bufs[g])
        else:
            i = i_diff
            i_diff += 1
            lam_init = 0.8 - 0.6 * math.exp(-0.3 * layer)
            w_qkv = diff_w_qkv[i].astype(BF16)
            weights = [w_qkv[:, :DIFF_Q_DIM], w_qkv[:, DIFF_Q_DIM:DIFF_Q_DIM + DIFF_K_DIM],
                       w_qkv[:, DIFF_Q_DIM + DIFF_K_DIM:]]
            gains = [jnp.tile(diff_q_norm[i].astype(F32), DIFF_Q_DIM // HEAD_DIM).reshape(1, DIFF_Q_DIM),
                     jnp.tile(diff_k_norm[i].astype(F32), DIFF_K_DIM // HEAD_DIM).reshape(1, DIFF_K_DIM), None]
            wo = diff_w_out[i].astype(BF16)
            q, k, kt, v = _norm_proj(xp, g_mix, weights, gains, tm_p, emit=("n", "nt", "n"), seq=seq)
            ap = _diff_prompt(q, k, v, rel_bias, diff_lambda[i], diff_out_norm[i], lam_init, bp, seq)
            dk_p.append(jnp.transpose(kt.reshape(bp, DIFF_KV_HEADS, 2, HEAD_DIM, seq), (0, 4, 1, 2, 3)))
            dv_p.append(v.reshape(bp, seq, DIFF_KV_HEADS, DIFF_V_DIM))
            q, k, v = _norm_proj(xs, g_mix, weights, gains, tm_s)
            as_ = _diff_sample(q, k, v, cache_diff_k[i], cache_diff_v[i], page_table, rel_bias, diff_lambda[i],
                               diff_out_norm[i], lam_init, bs, t_new)
            dk_s.append(k.reshape(bs, t_new, DIFF_KV_HEADS, 2, HEAD_DIM))
            dv_s.append(v.reshape(bs, t_new, DIFF_KV_HEADS, DIFF_V_DIM))
        g_ffn = norm_ffn[layer].astype(F32)
        xp = _mix_ffn(xp, ap, wo, g_ffn, w1, w2, layer, 2 * tm_p)
        xs = _mix_ffn(xs, as_, wo, g_ffn, w1, w2, layer, tm_s)
    return (xp.reshape(bp, seq, d), xs.reshape(bs, t_new, d),
            jnp.stack(conv_p), jnp.stack(conv_s), jnp.stack(ssm_p), jnp.stack(ssm_s),
            jnp.stack(swa_p[0]), jnp.stack(swa_s[0]), jnp.stack(swa_p[1]), jnp.stack(swa_s[1]),
            jnp.stack(swa_p[2]), jnp.stack(swa_s[2]),
            jnp.stack(dk_p), jnp.stack(dk_s), jnp.stack(dv_p), jnp.stack(dv_s))
```

```python
import functools
import math

import jax
import jax.numpy as jnp
from jax import lax
from jax.experimental import pallas as pl
from jax.experimental.pallas import tpu as pltpu

F32 = jnp.float32
BF16 = jnp.bfloat16

D_MODEL = 1024
DEPTH = 4
D_FF = 2816
D_INNER = 2048
SSM_HEADS = 32
SSM_HEAD_DIM = 64
SSM_GROUPS = 4
D_STATE = 128
CONV_W = 4
CONV_DIM = D_INNER + 2 * SSM_GROUPS * D_STATE
SSD_CHUNK = 128
SHORT_CHUNK = 16
HEAD_DIM = 64
N_ATTN_HEADS = 12
NUM_BUCKETS = 32
MAX_DISTANCE = 2048
SWA_GROUPS = ((128, 1), (512, 4), (2048, 16))
SWA_HEADS_PER_GROUP = 4
SWA_GROUP_DIM = SWA_HEADS_PER_GROUP * HEAD_DIM
SWA_DIM = N_ATTN_HEADS * HEAD_DIM
SWA_STEPS = 128
DIFF_KV_HEADS = 4
DIFF_REP = 3
DIFF_V_DIM = 128
DIFF_Q_DIM = N_ATTN_HEADS * 2 * HEAD_DIM
DIFF_K_DIM = DIFF_KV_HEADS * 2 * HEAD_DIM
DIFF_V_ALL = DIFF_KV_HEADS * DIFF_V_DIM
PAGE_SIZE = 128
EPS = 1e-6
NEG = -1e30

LANES = 128
VMEM_LIMIT = 56 * 1024 * 1024
PAGES_PER_STEP = 16


def _dot(a, b):
    return jnp.dot(a, b, preferred_element_type=F32)


def _dot_nt(a, b):
    return lax.dot_general(a, b, (((1,), (1,)), ((), ())), preferred_element_type=F32)


def _split3(x):
    hi = x.astype(BF16)
    r = x - hi.astype(F32)
    mid = r.astype(BF16)
    lo = (r - mid.astype(F32)).astype(BF16)
    return hi, mid, lo


def _dot_exact_rhs(a3_bf16, x):
    return _dot(a3_bf16, jnp.concatenate(_split3(x), axis=0))


def _dot_exact_lhs(x, a3_bf16):
    return _dot(jnp.concatenate(_split3(x), axis=1), a3_bf16)


def _sigmoid(x):
    return 1.0 / (1.0 + jnp.exp(-x))


def _rms(x, g):
    return x * lax.rsqrt(jnp.mean(x * x, axis=-1, keepdims=True) + EPS) * g


def _resident(shape):
    n = len(shape)
    return pl.BlockSpec(shape, lambda *_: (0,) * n, pipeline_mode=pl.Buffered(1))


def _params(sem):
    return pltpu.CompilerParams(dimension_semantics=sem, vmem_limit_bytes=VMEM_LIMIT)


def _norm_proj_kernel(*refs, n_out, head_norm, emit):
    x_ref, g_ref = refs[0], refs[1]
    w_refs = refs[2:2 + n_out]
    n_hn = sum(head_norm)
    hn_refs = refs[2 + n_out:2 + n_out + n_hn]
    pos = 2 + n_out + n_hn
    seg_ref = refs[pos] if n_hn else None
    pos += 1 if n_hn else 0
    o_refs = list(refs[pos:])
    h = _rms(x_ref[...], g_ref[...]).astype(BF16)
    k = 0
    for i in range(n_out):
        y = _dot(h, w_refs[i][...])
        if head_norm[i]:
            gain = hn_refs[k][...]
            k += 1
            seg = seg_ref[...]
            parts = []
            for c in range(y.shape[1] // seg.shape[0]):
                yc = y[:, c * seg.shape[0]:(c + 1) * seg.shape[0]]
                sq = yc * yc
                hi = sq.astype(BF16)
                lo = (sq - hi.astype(F32)).astype(BF16)
                ms = (_dot(hi, seg) + _dot(lo, seg)) * (1.0 / HEAD_DIM)
                parts.append(yc * lax.rsqrt(ms + EPS))
            y = jnp.concatenate(parts, axis=1) * gain
        if "n" in emit[i]:
            o_refs.pop(0)[...] = y
        if "t" in emit[i]:
            o_refs.pop(0)[0] = y.T
        if "b" in emit[i]:
            ref = o_refs.pop(0)
            for cb in range(y.shape[1] // LANES):
                ref[0, cb] = y[:, cb * LANES:(cb + 1) * LANES]


def _norm_proj(x, g, weights, head_gains, tm, emit=None, seq=None):
    m, d = x.shape
    n_out = len(weights)
    emit = tuple(emit or ("n",) * n_out)
    head_norm = tuple(hg is not None for hg in head_gains)
    ins = [x, g.reshape(1, d)] + list(weights)
    specs = [pl.BlockSpec((tm, d), lambda i: (i, 0)), _resident((1, d))]
    specs += [_resident(w.shape) for w in weights]
    for hg in head_gains:
        if hg is not None:
            ins.append(hg)
            specs.append(_resident(hg.shape))
    if any(head_norm):
        r = jnp.arange(2 * LANES)
        seg = (r[:, None] // HEAD_DIM == r[None, :] // HEAD_DIM).astype(BF16)
        ins.append(seg)
        specs.append(_resident(seg.shape))
    per_seq = seq // tm if seq else 1
    out_specs, out_shape = [], []
    for w, e in zip(weights, emit):
        n = w.shape[1]
        if "n" in e:
            out_specs.append(pl.BlockSpec((tm, n), lambda i: (i, 0)))
            out_shape.append(jax.ShapeDtypeStruct((m, n), F32))
        if "t" in e:
            out_specs.append(pl.BlockSpec((1, n, tm), lambda i: (i // per_seq, 0, i % per_seq)))
            out_shape.append(jax.ShapeDtypeStruct((m // seq, n, seq), F32))
        if "b" in e:
            out_specs.append(pl.BlockSpec((1, n // LANES, tm, LANES), lambda i: (i // per_seq, 0, i % per_seq, 0)))
            out_shape.append(jax.ShapeDtypeStruct((m // seq, n // LANES, seq, LANES), F32))
    return pl.pallas_call(
        functools.partial(_norm_proj_kernel, n_out=n_out, head_norm=head_norm, emit=emit),
        grid=(m // tm,),
        in_specs=specs,
        out_specs=out_specs,
        out_shape=out_shape,
        compiler_params=_params(("parallel",)),
        name="norm_proj",
    )(*ins)


def _mix_ffn_kernel(x_ref, a_ref, wo_ref, g_ref, w1_ref, w2_ref, o_ref):
    x1 = x_ref[...] + _dot(a_ref[...], wo_ref[...])
    h = _rms(x1, g_ref[...]).astype(BF16)
    gu = _dot(h, w1_ref[0])
    gate = gu[:, :D_FF]
    act = (gate * _sigmoid(gate) * gu[:, D_FF:]).astype(BF16)
    o_ref[...] = x1 + _dot(act, w2_ref[0])


def _mix_ffn(x, a, wo, g, w1, w2, layer, tm):
    m, d = x.shape
    ka = a.shape[1]
    slab = lambda w: pl.BlockSpec((1,) + w.shape[1:], lambda i: (layer, 0, 0), pipeline_mode=pl.Buffered(1))
    return pl.pallas_call(
        _mix_ffn_kernel,
        grid=(m // tm,),
        in_specs=[pl.BlockSpec((tm, d), lambda i: (i, 0)),
                  pl.BlockSpec((tm, ka), lambda i: (i, 0)),
                  _resident(wo.shape), _resident((1, d)), slab(w1), slab(w2)],
        out_specs=pl.BlockSpec((tm, d), lambda i: (i, 0)),
        out_shape=jax.ShapeDtypeStruct((m, d), F32),
        compiler_params=_params(("parallel",)),
        name="mix_ffn",
    )(x, a, wo, g.reshape(1, d), w1, w2)


def _ssd_kernel(z_ref, xbc_ref, dt_ref, cst_ref, sst_ref, cw_ref, cb_ref, dtb_ref, alog_ref,
                dx_ref, ng_ref, e_ref, *rest, q, n_valid, n_prev):
    prev_ref = rest[0] if n_prev else None
    y_ref, ns_ref, ext_ref, st_ref, yacc_ref, *pad_refs = rest[1 if n_prev else 0:]
    c = pl.program_id(1)
    last = pl.num_programs(1) - 1
    padded = n_valid < q

    @pl.when(c == 0)
    def _():
        ext_ref[...] = jnp.zeros(ext_ref.shape, F32)
        ext_ref[5:8, :] = cst_ref[0]
        st_ref[...] = sst_ref[0, 0]
        if padded:
            for r in pad_refs:
                r[...] = jnp.zeros(r.shape, F32)

    ext_ref[8:8 + n_valid, :] = xbc_ref[0]
    conv = cb_ref[...]
    for k in range(CONV_W):
        conv = conv + ext_ref[5 + k:5 + k + q, :] * cw_ref[k:k + 1, :]
    if not padded:
        ext_ref[5:8, :] = ext_ref[q + 5:q + 8, :]
    act = conv * _sigmoid(conv)
    xs = act[:, :D_INNER]
    bm = act[:, D_INNER:D_INNER + SSM_GROUPS * D_STATE].astype(BF16)
    cm = act[:, D_INNER + SSM_GROUPS * D_STATE:].astype(BF16)

    if padded:
        zpad_ref, dtpad_ref = pad_refs
        zpad_ref[0:n_valid, :] = z_ref[0]
        dtpad_ref[0:n_valid, :] = dt_ref[0]
        z = zpad_ref[...]
        dt_raw = dtpad_ref[...]
    else:
        z = z_ref[0]
        dt_raw = dt_ref[0]

    row = lax.broadcasted_iota(jnp.int32, (q, q), 0)
    col = lax.broadcasted_iota(jnp.int32, (q, q), 1)
    causal = col <= row
    tri = jnp.where(causal, 1.0, 0.0).astype(BF16)
    tri_t = jnp.where(row <= col, 1.0, 0.0).astype(BF16)
    tri3 = jnp.concatenate([tri] * 3, axis=1)
    tri_t3 = jnp.concatenate([tri_t] * 3, axis=0)

    xdt_in = dt_raw + dtb_ref[...]
    dt = jnp.maximum(xdt_in, 0.0) + jnp.log(1.0 + jnp.exp(-jnp.abs(xdt_in)))
    if padded:
        dt = jnp.where(lax.broadcasted_iota(jnp.int32, dt.shape, 0) < n_valid, dt, 0.0)
    adt = dt * (-jnp.exp(alog_ref[...]))
    acum = _dot_exact_rhs(tri3, adt)
    acum_t = _dot_exact_lhs(adt.T, tri_t3)
    both_x = _dot_exact_lhs(jnp.concatenate([dt, acum], axis=0), e_ref[...])
    dt_x, acum_x = both_x[0:q], both_x[q:2 * q]
    exp_acum_x = jnp.exp(acum_x)
    acum_last_x = acum_x[q - 1:q, :]
    xdt = xs * dt_x
    xdt_b = xdt.astype(BF16)
    xd = xdt * jnp.exp(acum_last_x - acum_x)
    dtot_x = jnp.exp(acum_last_x)
    lane_lo = lax.broadcasted_iota(jnp.int32, (q, LANES), 1) < SSM_HEAD_DIM

    gw = SSM_HEADS // SSM_GROUPS * SSM_HEAD_DIM
    for g in range(SSM_GROUPS):
        cg = cm[:, g * D_STATE:(g + 1) * D_STATE]
        bg = bm[:, g * D_STATE:(g + 1) * D_STATE]
        cb = _dot_nt(cg, bg)
        st_g = st_ref[g * gw:(g + 1) * gw, :]
        yoff = _dot_nt(cg, st_g.astype(BF16))
        yacc_ref[:, g * gw:(g + 1) * gw] = yoff * exp_acum_x[:, g * gw:(g + 1) * gw]
        for pair in range(gw // LANES):
            h0 = g * (gw // SSM_HEAD_DIM) + 2 * pair
            xp = xdt_b[:, h0 * SSM_HEAD_DIM:h0 * SSM_HEAD_DIM + LANES]
            ys = []
            for h in (h0, h0 + 1):
                seg = jnp.broadcast_to(acum[:, h:h + 1], (q, q)) - jnp.broadcast_to(acum_t[h:h + 1, :], (q, q))
                w = cb * jnp.exp(jnp.where(causal, seg, NEG))
                ys.append(_dot(w.astype(BF16), xp))
            yacc_ref[:, h0 * SSM_HEAD_DIM:h0 * SSM_HEAD_DIM + LANES] += jnp.where(lane_lo, ys[0], ys[1])
        xd_t = jnp.concatenate(
            [xd[:, g * gw + t * LANES:g * gw + (t + 1) * LANES].T for t in range(gw // LANES)], axis=0)
        upd = _dot(xd_t.astype(BF16), bg)
        dtot = jnp.concatenate(
            [jnp.broadcast_to(dtot_x[:, g * gw + t * LANES:g * gw + (t + 1) * LANES], (LANES, LANES)).T
             for t in range(gw // LANES)], axis=0)
        st_ref[g * gw:(g + 1) * gw, :] = dtot * st_g + upd

    y = yacc_ref[...] + xs * dx_ref[...]
    gated = y * (z * _sigmoid(z))
    outs = []
    for g in range(SSM_GROUPS):
        gg = gated[:, g * gw:(g + 1) * gw]
        outs.append(gg * lax.rsqrt(jnp.mean(gg * gg, axis=-1, keepdims=True) + EPS))
    yn = jnp.concatenate(outs, axis=1) * ng_ref[...]
    y_ref[0] = yn[0:n_valid].astype(BF16)

    @pl.when(c == last)
    def _():
        for k in range(n_prev):
            ns_ref[k, 0] = prev_ref[k, 0]
        ns_ref[n_prev, 0] = st_ref[...]


def _ssd_mixer(z, xbc, dt, conv_state, ssm_states, state_idx, conv_w, conv_b, dt_bias, a_log, d_skip, norm_g,
               prev_new=None):
    bt, seq, _ = z.shape
    q = SSD_CHUNK if seq >= SSD_CHUNK else SHORT_CHUNK
    n_valid = min(q, seq)
    nc = max(1, seq // q)
    pad = lambda v: jnp.pad(v.astype(F32), (0, LANES - SSM_HEADS)).reshape(1, LANES)
    hh = jnp.arange(LANES)[:, None]
    cc = jnp.arange(D_INNER)[None, :] // SSM_HEAD_DIM
    expand = jnp.tile((hh == cc).astype(BF16), (3, 1))
    dx = jnp.repeat(d_skip.astype(F32), SSM_HEAD_DIM).reshape(1, D_INNER)
    rows = SSM_HEADS * SSM_HEAD_DIM
    sst = ssm_states.reshape(ssm_states.shape[0], bt, rows, D_STATE)
    n_prev = 0 if prev_new is None else prev_new.shape[0]
    prev_in = [] if prev_new is None else [prev_new.reshape(n_prev, bt, rows, D_STATE)]
    prev_spec = [pl.BlockSpec((n_prev, 1, rows, D_STATE), lambda b, c: (0, b, 0, 0))] * len(prev_in)
    scratch = [pltpu.VMEM((q + 8, CONV_DIM), F32),
               pltpu.VMEM((SSM_HEADS * SSM_HEAD_DIM, D_STATE), F32),
               pltpu.VMEM((q, D_INNER), F32)]
    if n_valid < q:
        scratch += [pltpu.VMEM((q, D_INNER), F32), pltpu.VMEM((q, LANES), F32)]
    y, ns = pl.pallas_call(
        functools.partial(_ssd_kernel, q=q, n_valid=n_valid, n_prev=n_prev),
        grid=(bt, nc),
        in_specs=[pl.BlockSpec((1, n_valid, D_INNER), lambda b, c: (b, c, 0)),
                  pl.BlockSpec((1, n_valid, CONV_DIM), lambda b, c: (b, c, 0)),
                  pl.BlockSpec((1, n_valid, LANES), lambda b, c: (b, c, 0)),
                  pl.BlockSpec((1, CONV_W - 1, CONV_DIM), lambda b, c: (b, 0, 0)),
                  pl.BlockSpec((1, 1, SSM_HEADS * SSM_HEAD_DIM, D_STATE), lambda b, c: (state_idx, b, 0, 0)),
                  _resident((CONV_W, CONV_DIM)), _resident((1, CONV_DIM)),
                  _resident((1, LANES)), _resident((1, LANES)),
                  _resident((1, D_INNER)), _resident((1, D_INNER)), _resident((3 * LANES, D_INNER))] + prev_spec,
        out_specs=[pl.BlockSpec((1, n_valid, D_INNER), lambda b, c: (b, c, 0)),
                   pl.BlockSpec((n_prev + 1, 1, rows, D_STATE), lambda b, c: (0, b, 0, 0))],
        out_shape=[jax.ShapeDtypeStruct((bt, seq, D_INNER), BF16),
                   jax.ShapeDtypeStruct((n_prev + 1, bt, rows, D_STATE), F32)],
        scratch_shapes=scratch,
        compiler_params=_params(("parallel", "arbitrary")),
        name="ssd_mixer",
    )(z, xbc, dt, conv_state, sst, conv_w, conv_b.reshape(1, CONV_DIM), pad(dt_bias), pad(a_log),
      dx, norm_g.reshape(1, D_INNER), expand, *prev_in)
    return y, ns.reshape(n_prev + 1, bt, SSM_HEADS, SSM_HEAD_DIM, D_STATE)


def _t5_bucket(dist):
    max_exact = NUM_BUCKETS // 2
    d = jnp.maximum(dist, 0)
    df = jnp.maximum(d, 1).astype(F32)
    large = max_exact + (jnp.log(df / max_exact) / math.log(MAX_DISTANCE / max_exact)
                         * (NUM_BUCKETS - max_exact)).astype(jnp.int32)
    return jnp.where(d < max_exact, d, jnp.minimum(large, NUM_BUCKETS - 1))


def _bias_lookup(tab, dist):
    onehot = jax.nn.one_hot(_t5_bucket(dist), NUM_BUCKETS, dtype=F32)
    return jnp.einsum('...b,bh->...h', onehot, tab.astype(F32), precision=lax.Precision.HIGHEST)


def _swa_prompt_bias(rel_bias, g, dil):
    i = jnp.arange(SWA_STEPS)[:, None]
    j = jnp.arange(2 * SWA_STEPS)[None, :]
    step = SWA_STEPS + i - j
    tab = rel_bias[:, g * SWA_HEADS_PER_GROUP:(g + 1) * SWA_HEADS_PER_GROUP].astype(F32)
    bias = jnp.transpose(_bias_lookup(tab, jnp.clip(step, 0, SWA_STEPS) * dil), (2, 0, 1))
    valid = (step >= 0) & (step <= SWA_STEPS)
    first = valid & (j >= SWA_STEPS)
    both = jnp.stack([jnp.where(first, bias, NEG), jnp.where(valid, bias, NEG)])
    return both.reshape(2, SWA_HEADS_PER_GROUP * SWA_STEPS, 2 * SWA_STEPS)


def _swa_fused_kernel(q_ref, k_ref, v_ref, b_ref, a_ref, qs_ref, ks_ref, vs_ref, oc_ref, lc_ref, ot_ref, lt_ref,
                      *, seq):
    g = pl.program_id(1)
    gd = SWA_GROUP_DIM
    blk = SWA_STEPS
    n_units = seq // blk
    lane_head = lax.broadcasted_iota(jnp.int32, (blk, gd), 1) // HEAD_DIM

    def group_body(gi, dil):
        rows = seq // dil
        nb = rows // blk
        ks_ref[0:blk, :] = jnp.zeros((blk, gd), BF16)
        vs_ref[0:blk, :] = jnp.zeros((blk, gd), BF16)
        for c in range(dil):
            src = pl.ds(c, rows, stride=dil) if dil > 1 else pl.ds(0, rows)
            dst = slice(blk + c * rows, blk + (c + 1) * rows)
            for half in range(gd // LANES):
                lanes = slice(half * LANES, (half + 1) * LANES)
                qs_ref[dst, lanes] = (q_ref[0, half, src, :] * (HEAD_DIM ** -0.5)).astype(BF16)
                ks_ref[dst, lanes] = k_ref[0, half, src, :].astype(BF16)
                vs_ref[dst, lanes] = v_ref[0, half, src, :].astype(BF16)

        def unit(u, carry):
            start = pl.multiple_of(u * blk, blk)
            first = lax.rem(u, nb) == 0
            qv = qs_ref[pl.ds(blk + start, blk), :]
            kk = ks_ref[pl.ds(start, 2 * blk), :]
            vv = vs_ref[pl.ds(start, 2 * blk), :]
            sel = jnp.where(first, 0, 1)
            nh = SWA_HEADS_PER_GROUP
            qh = jnp.concatenate([jnp.where(lane_head == h, qv, jnp.zeros_like(qv)) for h in range(nh)], axis=0)
            s = _dot_nt(qh, kk) + b_ref[gi, sel]
            m = jnp.max(s, axis=-1, keepdims=True)
            p = jnp.exp(s - m)
            l = jnp.sum(p, axis=-1, keepdims=True)
            of = _dot(p.astype(BF16), vv) / l
            lf = jnp.broadcast_to(m + jnp.log(l), (nh * blk, gd))
            o, lse = of[0:blk], lf[0:blk]
            for h in range(1, nh):
                o = jnp.where(lane_head == h, of[h * blk:(h + 1) * blk], o)
                lse = jnp.where(lane_head == h, lf[h * blk:(h + 1) * blk], lse)
            oc_ref[pl.ds(start, blk), :] = o
            lc_ref[pl.ds(start, blk), :] = lse
            return carry

        lax.fori_loop(0, n_units, unit, 0, unroll=2)
        for c in range(dil):
            dst = pl.ds(c, rows, stride=dil) if dil > 1 else pl.ds(0, rows)
            for half in range(gd // LANES):
                lanes = slice(half * LANES, (half + 1) * LANES)
                cb = gi * (gd // LANES) + half
                ot_ref[cb, dst, :] = oc_ref[c * rows:(c + 1) * rows, lanes]
                lt_ref[cb, dst, :] = lc_ref[c * rows:(c + 1) * rows, lanes]

    for gi, (_, dil) in enumerate(SWA_GROUPS):
        pl.when(g == gi)(functools.partial(group_body, gi, dil))

    @pl.when(g == len(SWA_GROUPS) - 1)
    def _():
        def tile(t, carry):
            r = pl.ds(pl.multiple_of(t * blk, blk), blk)
            per = gd // LANES
            for half in range(per):
                ls = [lt_ref[gi * per + half, r, :] for gi in range(len(SWA_GROUPS))]
                m = jnp.maximum(jnp.maximum(ls[0], ls[1]), ls[2])
                es = [jnp.exp(l - m) for l in ls]
                inv = 1.0 / (es[0] + es[1] + es[2])
                for gi in range(len(SWA_GROUPS)):
                    cb = gi * per + half
                    a_ref[0, r, cb * LANES:(cb + 1) * LANES] = (ot_ref[cb, r, :] * es[gi] * inv).astype(BF16)
            return carry

        lax.fori_loop(0, n_units, tile, 0)


def _swa_prompt(q, k, v, rel_bias, bt, seq):
    bias = jnp.stack([_swa_prompt_bias(rel_bias, g, dil) for g, (_, dil) in enumerate(SWA_GROUPS)])
    gd = SWA_GROUP_DIM
    tok = pl.BlockSpec((1, gd // LANES, seq, LANES), lambda b, g: (b, g, 0, 0))
    a = pl.pallas_call(
        functools.partial(_swa_fused_kernel, seq=seq),
        grid=(bt, len(SWA_GROUPS)),
        in_specs=[tok, tok, tok, _resident(bias.shape)],
        out_specs=pl.BlockSpec((1, seq, SWA_DIM), lambda b, g: (b, 0, 0)),
        out_shape=jax.ShapeDtypeStruct((bt, seq, SWA_DIM), BF16),
        scratch_shapes=[pltpu.VMEM((seq + SWA_STEPS, gd), BF16)] * 3
                       + [pltpu.VMEM((seq, gd), F32)] * 2 + [pltpu.VMEM((SWA_DIM // LANES, seq, LANES), F32)] * 2,
        compiler_params=_params(("parallel", "arbitrary")),
        name="swa_prompt",
    )(q, k, v, bias)
    return a.reshape(bt * seq, SWA_DIM)


def _swa_sample_kernel(q_ref, kn_ref, c0_ref, c1_ref, c2_ref, b0_ref, b1_ref, b2_ref, nb_ref,
                       a_ref, o0_ref, o1_ref, o2_ref, *, t_new):
    gd = SWA_GROUP_DIM
    nh = SWA_HEADS_PER_GROUP
    lane_head = lax.broadcasted_iota(jnp.int32, (t_new, gd), 1) // HEAD_DIM
    caches = (c0_ref, c1_ref, c2_ref)
    biases = (b0_ref, b1_ref, b2_ref)
    new_caches = (o0_ref, o1_ref, o2_ref)
    outs, lses = [], []
    for g in range(len(SWA_GROUPS)):
        c_ref = caches[g]
        win = c_ref.shape[2]
        qg = q_ref[0, :, g * gd:(g + 1) * gd] * (HEAD_DIM ** -0.5)
        qs = jnp.concatenate([jnp.where(lane_head == h, qg, 0.0) for h in range(nh)], axis=0).astype(BF16)
        new = kn_ref[0, g]
        s = _dot(qs, c_ref[0, 0:gd, :].astype(BF16)) + biases[g][...]
        sn = _dot(qs, new[0:gd].astype(BF16)) + nb_ref[g]
        m = jnp.maximum(jnp.max(s, axis=-1, keepdims=True), jnp.max(sn, axis=-1, keepdims=True))
        p = jnp.exp(s - m)
        pn = jnp.exp(sn - m)
        l = jnp.sum(p, axis=-1, keepdims=True) + jnp.sum(pn, axis=-1, keepdims=True)
        of = (_dot_nt(p.astype(BF16), c_ref[0, gd:2 * gd, :].astype(BF16))
              + _dot_nt(pn.astype(BF16), new[gd:2 * gd].astype(BF16))) / l
        lf = jnp.broadcast_to(m + jnp.log(l), (nh * t_new, gd))
        o = jnp.zeros((t_new, gd), F32)
        lse = jnp.zeros((t_new, gd), F32)
        for h in range(nh):
            o = jnp.where(lane_head == h, of[h * t_new:(h + 1) * t_new], o)
            lse = jnp.where(lane_head == h, lf[h * t_new:(h + 1) * t_new], lse)
        outs.append(o)
        lses.append(lse)
        rolled = pltpu.roll(c_ref[0], win - t_new, axis=1)
        if win > LANES:
            new_caches[g][0, :, 0:win - LANES] = rolled[:, 0:win - LANES]
        keep = lax.broadcasted_iota(jnp.int32, (2 * gd, LANES), 1) < LANES - t_new
        new_caches[g][0, :, win - LANES:win] = jnp.where(keep, rolled[:, win - LANES:win], new)
    m = jnp.maximum(jnp.maximum(lses[0], lses[1]), lses[2])
    es = [jnp.exp(l - m) for l in lses]
    inv = 1.0 / (es[0] + es[1] + es[2])
    for g in range(len(SWA_GROUPS)):
        a_ref[0, :, g * gd:(g + 1) * gd] = (outs[g] * es[g] * inv).astype(BF16)


def _swa_sample_bias(rel_bias, t_new):
    t = jnp.arange(t_new)
    biases, new_biases = [], []
    for g, (win, dil) in enumerate(SWA_GROUPS):
        tab = rel_bias[:, g * SWA_HEADS_PER_GROUP:(g + 1) * SWA_HEADS_PER_GROUP].astype(F32)
        back = t[:, None] - jnp.arange(win)[None, :]
        ok = (back <= 0) & (back % dil == 0)
        b = jnp.where(ok[..., None], _bias_lookup(tab, win + back), NEG)
        biases.append(jnp.transpose(b, (2, 0, 1)).reshape(SWA_HEADS_PER_GROUP * t_new, win))
        j = jnp.arange(LANES)[None, :] - (LANES - t_new)
        nd = t[:, None] - j
        nok = (j >= 0) & (nd >= 0) & (nd % dil == 0)
        nb = jnp.where(nok[..., None], _bias_lookup(tab, nd), NEG)
        new_biases.append(jnp.transpose(nb, (2, 0, 1)).reshape(SWA_HEADS_PER_GROUP * t_new, LANES))
    return biases, jnp.stack(new_biases)


def _swa_sample(q, k, v, caches, rel_bias, bt, t_new):
    biases, new_bias = _swa_sample_bias(rel_bias, t_new)
    gd = SWA_GROUP_DIM
    tiles = []
    for g in range(len(SWA_GROUPS)):
        kv = jnp.concatenate([k[:, :, g * gd:(g + 1) * gd], v[:, :, g * gd:(g + 1) * gd]], axis=2)
        tiles.append(jnp.pad(jnp.transpose(kv, (0, 2, 1)), ((0, 0), (0, 0), (LANES - t_new, 0))))
    kn = jnp.stack(tiles, axis=1)
    views = [jnp.transpose(c, (0, 2, 3, 4, 1)).reshape(bt, 2 * gd, c.shape[1]) for c in caches]
    cspecs = [pl.BlockSpec((1, 2 * gd, c.shape[2]), lambda b: (b, 0, 0)) for c in views]
    tok = pl.BlockSpec((1, t_new, SWA_DIM), lambda b: (b, 0, 0))
    res = pl.pallas_call(
        functools.partial(_swa_sample_kernel, t_new=t_new),
        grid=(bt,),
        in_specs=[tok, pl.BlockSpec((1, len(SWA_GROUPS), 2 * gd, LANES), lambda b: (b, 0, 0, 0))] + cspecs
                 + [_resident(b.shape) for b in biases] + [_resident(new_bias.shape)],
        out_specs=[tok] + cspecs,
        out_shape=[jax.ShapeDtypeStruct((bt, t_new, SWA_DIM), BF16)]
                  + [jax.ShapeDtypeStruct(c.shape, F32) for c in views],
        compiler_params=_params(("parallel",)),
        name="swa_sample",
    )(q, kn, *views, *biases, new_bias)
    new_caches = [jnp.transpose(c.reshape(bt, 2, SWA_HEADS_PER_GROUP, HEAD_DIM, c.shape[2]), (0, 4, 1, 2, 3))
                  for c in res[1:]]
    return res[0].reshape(bt * t_new, SWA_DIM), new_caches


def _diff_lambda(lam_ref, lam_init):
    lp = lam_ref[...]
    s1 = jnp.sum(lp[0:1] * lp[1:2], axis=-1, keepdims=True)
    s2 = jnp.sum(lp[2:3] * lp[3:4], axis=-1, keepdims=True)
    return jnp.exp(s1) - jnp.exp(s2) + lam_init


def _diff_prompt_kernel(q_ref, k_ref, v_ref, b_ref, lam_ref, on_ref, o_ref, q6_ref, m_ref, acc_ref,
                        *, tq, tk, lam_init):
    i = pl.program_id(2)
    nsub = tk // LANES
    qsub = tq // LANES
    lane_lo = lax.broadcasted_iota(jnp.int32, (tq, LANES), 1) < HEAD_DIM
    for r in range(DIFF_REP):
        qr = q_ref[0, :, r * LANES:(r + 1) * LANES] * (HEAD_DIM ** -0.5)
        q6_ref[(2 * r) * tq:(2 * r + 1) * tq, :] = jnp.where(lane_lo, qr, 0.0).astype(BF16)
        q6_ref[(2 * r + 1) * tq:(2 * r + 2) * tq, :] = jnp.where(lane_lo, 0.0, qr).astype(BF16)
    m_ref[...] = jnp.full(m_ref.shape, NEG, F32)
    acc_ref[...] = jnp.zeros(acc_ref.shape, F32)
    ones = jnp.ones((tk, LANES), BF16)

    def body(j, carry):
        start = pl.multiple_of(j * tk, tk)
        kb = k_ref[0, pl.ds(start, tk), :].astype(BF16)
        vx = jnp.concatenate([v_ref[0, pl.ds(start, tk), :].astype(BF16), ones], axis=1)
        d0 = i * qsub - j * nsub + (nsub - 1)
        for r in range(DIFF_REP):
            rows = slice(2 * r * tq, (2 * r + 2) * tq)
            s = _dot_nt(q6_ref[rows, :], kb)
            bias = jnp.concatenate(
                [jnp.concatenate([b_ref[0, d0 - c + qs, r] for qs in range(qsub)], axis=0) for c in range(nsub)],
                axis=1)
            s = s + jnp.concatenate([bias, bias], axis=0)
            smax = s[:, 0:LANES]
            for c in range(1, tk // LANES):
                smax = jnp.maximum(smax, s[:, c * LANES:(c + 1) * LANES])
            m_old = m_ref[rows, :]
            m_new = jnp.maximum(m_old, jnp.max(smax, axis=-1, keepdims=True))
            alpha = jnp.exp(m_old - m_new)
            p = jnp.concatenate([jnp.exp(s[:, c * LANES:(c + 1) * LANES] - m_new)
                                 for c in range(tk // LANES)], axis=1).astype(BF16)
            acc_ref[rows, :] = jnp.concatenate([alpha, alpha], axis=1) * acc_ref[rows, :] + _dot(p, vx)
            m_ref[rows, :] = m_new
        return carry

    lax.fori_loop(0, ((i + 1) * tq - 1) // tk + 1, body, 0)
    lam = _diff_lambda(lam_ref, lam_init)
    for r in range(DIFF_REP):
        o0 = acc_ref[(2 * r) * tq:(2 * r + 1) * tq, :]
        o1 = acc_ref[(2 * r + 1) * tq:(2 * r + 2) * tq, :]
        a = o0[:, :DIFF_V_DIM] / o0[:, DIFF_V_DIM:] - lam * (o1[:, :DIFF_V_DIM] / o1[:, DIFF_V_DIM:])
        o_ref[0, :, r * LANES:(r + 1) * LANES] = (_rms(a, on_ref[...]) * (1.0 - lam_init)).astype(BF16)


def _diff_prompt_bias(rel_bias, seq, tk):
    nd = seq // LANES + tk // LANES - 1
    delta = jnp.arange(nd) - (tk // LANES - 1)
    d = delta[:, None, None] * LANES + jnp.arange(LANES)[None, :, None] - jnp.arange(LANES)[None, None, :]
    tab = jnp.where((d >= 0)[..., None], _bias_lookup(rel_bias, d), NEG)
    tab = jnp.transpose(tab, (3, 0, 1, 2)).reshape(DIFF_KV_HEADS, DIFF_REP, nd, LANES, LANES)
    return jnp.transpose(tab, (0, 2, 1, 3, 4))


def _diff_prompt(q, k, v, rel_bias, lam_p, out_norm, lam_init, bt, seq):
    tq, tk = 512, 512
    nb = seq // tq
    bias = _diff_prompt_bias(rel_bias, seq, tk)
    qw = DIFF_REP * 2 * HEAD_DIM
    a = pl.pallas_call(
        functools.partial(_diff_prompt_kernel, tq=tq, tk=tk, lam_init=lam_init),
        grid=(bt, DIFF_KV_HEADS, nb),
        in_specs=[pl.BlockSpec((1, tq, qw), lambda b, g, i: (b, i, g)),
                  pl.BlockSpec((1, seq, LANES), lambda b, g, i: (b, 0, g)),
                  pl.BlockSpec((1, seq, LANES), lambda b, g, i: (b, 0, g)),
                  pl.BlockSpec((1,) + bias.shape[1:], lambda b, g, i: (g, 0, 0, 0, 0)),
                  _resident((4, HEAD_DIM)), _resident((1, DIFF_V_DIM))],
        out_specs=pl.BlockSpec((1, tq, qw), lambda b, g, i: (b, i, g)),
        out_shape=jax.ShapeDtypeStruct((bt, seq, DIFF_Q_DIM), BF16),
        scratch_shapes=[pltpu.VMEM((2 * DIFF_REP * tq, LANES), BF16),
                        pltpu.VMEM((2 * DIFF_REP * tq, LANES), F32),
                        pltpu.VMEM((2 * DIFF_REP * tq, 2 * DIFF_V_DIM), F32)],
        compiler_params=_params(("parallel", "parallel", "arbitrary")),
        name="diff_prompt",
    )(q.reshape(bt, seq, DIFF_Q_DIM), k.reshape(bt, seq, DIFF_K_DIM), v.reshape(bt, seq, DIFF_V_ALL),
      bias, lam_p.astype(F32), out_norm.reshape(1, DIFF_V_DIM).astype(F32))
    return a.reshape(bt * seq, DIFF_Q_DIM)


def _diff_sample_kernel(pt_ref, qb_ref, kn_ref, vn_ref, *refs, t_new, lam_init):
    npg = PAGES_PER_STEP
    k_refs = refs[:npg]
    v_refs = refs[npg:2 * npg]
    b_ref, nb_ref, lam_ref, on_ref, o_ref, m_ref, l_ref, acc_ref, qs_ref = refs[2 * npg:]
    s_idx = pl.program_id(1)
    per_map = DIFF_REP * t_new
    per_g = 2 * per_map

    @pl.when(s_idx == 0)
    def _():
        m_ref[...] = jnp.full(m_ref.shape, NEG, F32)
        l_ref[...] = jnp.zeros(l_ref.shape, F32)
        acc_ref[...] = jnp.zeros(acc_ref.shape, F32)
        qr = qb_ref[0] * (HEAD_DIM ** -0.5)
        own = lax.broadcasted_iota(jnp.int32, qr.shape, 0) // per_map
        for c in range(2 * DIFF_KV_HEADS):
            qs_ref[:, c * HEAD_DIM:(c + 1) * HEAD_DIM] = jnp.where(own == c, qr, 0.0)

    qb = qs_ref[...].astype(BF16)

    def absorb(kt_refs, vv_refs, bias):
        kcat = jnp.concatenate([kt[0].astype(BF16) for kt in kt_refs], axis=1)
        s = _dot(qb, kcat) + bias
        chunks = [s[:, c * LANES:(c + 1) * LANES] for c in range(len(kt_refs))]
        smax = chunks[0]
        for c in chunks[1:]:
            smax = jnp.maximum(smax, c)
        m_old = m_ref[...]
        m_new = jnp.maximum(m_old, jnp.max(smax, axis=-1, keepdims=True))
        alpha = jnp.exp(m_old - m_new)
        ps = [jnp.exp(c - m_new) for c in chunks]
        psum = ps[0]
        for p in ps[1:]:
            psum = psum + p
        l_ref[...] = alpha * l_ref[...] + jnp.sum(psum, axis=-1, keepdims=True)
        pcat = jnp.concatenate(ps, axis=1).astype(BF16)
        for g in range(DIFF_KV_HEADS):
            rows = slice(g * per_g, (g + 1) * per_g)
            vcat = jnp.concatenate([vv[0, pl.ds(g, PAGE_SIZE, stride=DIFF_KV_HEADS), :].astype(BF16)
                                    for vv in vv_refs], axis=0)
            acc_ref[rows, :] = alpha[rows] * acc_ref[rows, :] + _dot(pcat[rows], vcat)
        m_ref[...] = m_new

    absorb(k_refs, v_refs, b_ref[0])

    @pl.when(s_idx == pl.num_programs(1) - 1)
    def _():
        absorb([kn_ref], [vn_ref], nb_ref[...])
        lam = _diff_lambda(lam_ref, lam_init)
        o = acc_ref[...] / l_ref[...]
        for g in range(DIFF_KV_HEADS):
            for r in range(DIFF_REP):
                r0 = g * per_g + r * t_new
                blk = o[r0:r0 + t_new] - lam * o[r0 + per_map:r0 + per_map + t_new]
                col = (g * DIFF_REP + r) * DIFF_V_DIM
                o_ref[0, :, col:col + DIFF_V_DIM] = (_rms(blk, on_ref[...]) * (1.0 - lam_init)).astype(BF16)


def _diff_sample(q, k, v, cache_k, cache_v, page_table, rel_bias, lam_p, out_norm, lam_init, bt, t_new):
    n_pages = page_table.shape[1]
    past = n_pages * PAGE_SIZE
    n_phys = cache_k.shape[0]
    nrow = DIFF_KV_HEADS * 2 * DIFF_REP * t_new
    q6 = q.reshape(bt, t_new, DIFF_KV_HEADS, DIFF_REP, 2, HEAD_DIM)
    qb = jnp.transpose(q6, (0, 2, 4, 3, 1, 5)).reshape(bt, nrow, HEAD_DIM)
    tab = rel_bias.astype(F32)
    tt = jnp.arange(t_new)

    def rows_of(b):
        n = b.shape[1]
        b = jnp.transpose(b, (2, 0, 1)).reshape(DIFF_KV_HEADS, 1, DIFF_REP, t_new, n)
        return jnp.broadcast_to(b, (DIFF_KV_HEADS, 2, DIFF_REP, t_new, n)).reshape(nrow, n)

    dist = past + tt[:, None] - jnp.arange(past)[None, :]
    npg = PAGES_PER_STEP
    bias = rows_of(_bias_lookup(tab, dist)).reshape(nrow, n_pages // npg, npg * PAGE_SIZE)
    bias = jnp.transpose(bias, (1, 0, 2))
    nd = tt[:, None] - jnp.arange(PAGE_SIZE)[None, :]
    nbias = rows_of(jnp.where((nd >= 0)[..., None], _bias_lookup(tab, nd), NEG))
    kn = jnp.transpose(k.reshape(bt, t_new, DIFF_K_DIM), (0, 2, 1))
    kn = jnp.pad(kn, ((0, 0), (0, 0), (0, PAGE_SIZE - t_new)))
    vn = jnp.pad(v.reshape(bt, t_new * DIFF_KV_HEADS, DIFF_V_DIM),
                 ((0, 0), (0, (PAGE_SIZE - t_new) * DIFF_KV_HEADS), (0, 0)))
    ck = jnp.transpose(cache_k, (0, 2, 3, 4, 1)).reshape(n_phys, DIFF_K_DIM, PAGE_SIZE)
    cv = cache_v.reshape(n_phys, PAGE_SIZE * DIFF_KV_HEADS, DIFF_V_DIM)

    def page_spec(pp):
        return pl.BlockSpec((1, DIFF_K_DIM, PAGE_SIZE), lambda b, s, pt: (pt[b, s * npg + pp], 0, 0))

    const = lambda shape: pl.BlockSpec(shape, lambda b, s, pt: (0,) * len(shape))
    grid_spec = pltpu.PrefetchScalarGridSpec(
        num_scalar_prefetch=1,
        grid=(bt, n_pages // npg),
        in_specs=[pl.BlockSpec((1, nrow, HEAD_DIM), lambda b, s, pt: (b, 0, 0)),
                  pl.BlockSpec((1, DIFF_K_DIM, PAGE_SIZE), lambda b, s, pt: (b, 0, 0)),
                  pl.BlockSpec((1, DIFF_K_DIM, PAGE_SIZE), lambda b, s, pt: (b, 0, 0))]
                 + [page_spec(pp) for pp in range(npg)] * 2
                 + [pl.BlockSpec((1, nrow, npg * PAGE_SIZE), lambda b, s, pt: (s, 0, 0)),
                    const((nrow, PAGE_SIZE)), const((4, HEAD_DIM)), const((1, DIFF_V_DIM))],
        out_specs=pl.BlockSpec((1, t_new, DIFF_Q_DIM), lambda b, s, pt: (b, 0, 0)),
        scratch_shapes=[pltpu.VMEM((nrow, LANES), F32), pltpu.VMEM((nrow, LANES), F32),
                        pltpu.VMEM((nrow, DIFF_V_DIM), F32), pltpu.VMEM((nrow, DIFF_K_DIM), F32)],
    )
    a = pl.pallas_call(
        functools.partial(_diff_sample_kernel, t_new=t_new, lam_init=lam_init),
        grid_spec=grid_spec,
        out_shape=jax.ShapeDtypeStruct((bt, t_new, DIFF_Q_DIM), BF16),
        compiler_params=_params(("parallel", "arbitrary")),
        name="diff_sample",
    )(page_table, qb, kn, vn, *([ck] * npg), *([cv] * npg), bias, nbias,
      lam_p.astype(F32), out_norm.reshape(1, DIFF_V_DIM).astype(F32))
    return a.reshape(bt * t_new, DIFF_Q_DIM)


def kernel(x_prompt, x_sample, state_ssm_conv, state_ssm, cache_swa_kv0, cache_swa_kv1, cache_swa_kv2, cache_diff_k, cache_diff_v, page_table, rel_bias, norm_mix, norm_ffn, ffn_w_in, ffn_w_out, ssm_w_in, ssm_conv_w, ssm_conv_b, ssm_dt_bias, ssm_a_log, ssm_d, ssm_norm, ssm_w_out, swa_w_qkv, swa_q_norm, swa_k_norm, swa_w_out, diff_w_qkv, diff_q_norm, diff_k_norm, diff_lambda, diff_out_norm, diff_w_out):
    bp, seq, d = x_prompt.shape
    bs, t_new, _ = x_sample.shape
    mp, ms = bp * seq, bs * t_new
    tm_p, tm_s = 256, ms
    xp = x_prompt.reshape(mp, d)
    xs = x_sample.reshape(ms, d)
    swa_caches = (cache_swa_kv0, cache_swa_kv1, cache_swa_kv2)
    conv_p, conv_s = [], []
    ssm_p = ssm_s = None
    swa_p = tuple([] for _ in SWA_GROUPS)
    swa_s = tuple([] for _ in SWA_GROUPS)
    dk_p, dk_s, dv_p, dv_s = [], [], [], []
    i_ssd = i_swa = i_diff = 0
    w1, w2 = ffn_w_in.astype(BF16), ffn_w_out.astype(BF16)
    for layer in range(DEPTH):
        kind = layer % 3
        g_mix = norm_mix[layer].astype(F32)
        if kind == 0:
            i = i_ssd
            i_ssd += 1
            w_in = ssm_w_in[i].astype(BF16)
            weights = [w_in[:, :D_INNER], w_in[:, D_INNER:D_INNER + CONV_DIM],
                       jnp.pad(w_in[:, D_INNER + CONV_DIM:], ((0, 0), (0, LANES - SSM_HEADS)))]
            wo = ssm_w_out[i].astype(BF16)
            cw, cb = ssm_conv_w[i].astype(F32), ssm_conv_b[i].astype(F32)
            ssd_args = (cw, cb, ssm_dt_bias[i], ssm_a_log[i], ssm_d[i], ssm_norm[i].astype(F32))
            z, xbc, dt = _norm_proj(xp, g_mix, weights, [None] * 3, tm_p)
            xbc3 = xbc.reshape(bp, seq, CONV_DIM)
            y, ns = _ssd_mixer(z.reshape(bp, seq, D_INNER), xbc3, dt.reshape(bp, seq, LANES),
                               jnp.zeros((bp, CONV_W - 1, CONV_DIM), F32),
                               jnp.zeros((1, bp, SSM_HEADS, SSM_HEAD_DIM, D_STATE), F32), 0, *ssd_args,
                               prev_new=ssm_p)
            conv_p.append(xbc3[:, seq - (CONV_W - 1):])
            ssm_p = ns
            ap = y.reshape(mp, D_INNER)
            z, xbc, dt = _norm_proj(xs, g_mix, weights, [None] * 3, tm_s)
            xbc3 = xbc.reshape(bs, t_new, CONV_DIM)
            y, ns = _ssd_mixer(z.reshape(bs, t_new, D_INNER), xbc3, dt.reshape(bs, t_new, LANES),
                               state_ssm_conv[i], state_ssm, i, *ssd_args, prev_new=ssm_s)
            conv_s.append(xbc3[:, t_new - (CONV_W - 1):])
            ssm_s = ns
            as_ = y.reshape(ms, D_INNER)
        elif kind == 1:
            i = i_swa
            i_swa += 1
            w_qkv = swa_w_qkv[i].astype(BF16)
            weights = [w_qkv[:, :SWA_DIM], w_qkv[:, SWA_DIM:2 * SWA_DIM], w_qkv[:, 2 * SWA_DIM:]]
            gains = [jnp.tile(swa_q_norm[i].astype(F32), N_ATTN_HEADS).reshape(1, SWA_DIM),
                     jnp.tile(swa_k_norm[i].astype(F32), N_ATTN_HEADS).reshape(1, SWA_DIM), None]
            wo = swa_w_out[i].astype(BF16)
            q, kt, k, vt, v = _norm_proj(xp, g_mix, weights, gains, tm_p, emit=("b", "tb", "tb"), seq=seq)
            ap = _swa_prompt(q, k, v, rel_bias, bp, seq)
            for g, (win, dil) in enumerate(SWA_GROUPS):
                keep = min(win, seq)
                sl = slice(g * SWA_GROUP_DIM, (g + 1) * SWA_GROUP_DIM)
                kv_t = jnp.stack([kt[:, sl, seq - keep:], vt[:, sl, seq - keep:]], axis=1)
                kv_t = kv_t.reshape(bp, 2, SWA_HEADS_PER_GROUP, HEAD_DIM, keep)
                swa_p[g].append(jnp.transpose(kv_t, (0, 4, 1, 2, 3)))
            q, k, v = _norm_proj(xs, g_mix, weights, gains, tm_s)
            q3, k3, v3 = (t.reshape(bs, t_new, SWA_DIM) for t in (q, k, v))
            as_, new_bufs = _swa_sample(q3, k3, v3, [c[i] for c in swa_caches], rel_bias, bs, t_new)
            for g in range(len(SWA_GROUPS)):
                swa_s[g].append(new_bufs[g])
        else:
            i = i_diff
            i_diff += 1
            lam_init = 0.8 - 0.6 * math.exp(-0.3 * layer)
            w_qkv = diff_w_qkv[i].astype(BF16)
            weights = [w_qkv[:, :DIFF_Q_DIM], w_qkv[:, DIFF_Q_DIM:DIFF_Q_DIM + DIFF_K_DIM],
                       w_qkv[:, DIFF_Q_DIM + DIFF_K_DIM:]]
            gains = [jnp.tile(diff_q_norm[i].astype(F32), DIFF_Q_DIM // HEAD_DIM).reshape(1, DIFF_Q_DIM),
                     jnp.tile(diff_k_norm[i].astype(F32), DIFF_K_DIM // HEAD_DIM).reshape(1, DIFF_K_DIM), None]
            wo = diff_w_out[i].astype(BF16)
            q, k, kt, v = _norm_proj(xp, g_mix, weights, gains, tm_p, emit=("n", "nt", "n"), seq=seq)
            ap = _diff_prompt(q, k, v, rel_bias, diff_lambda[i], diff_out_norm[i], lam_init, bp, seq)
            dk_p.append(jnp.transpose(kt.reshape(bp, DIFF_KV_HEADS, 2, HEAD_DIM, seq), (0, 4, 1, 2, 3)))
            dv_p.append(v.reshape(bp, seq, DIFF_KV_HEADS, DIFF_V_DIM))
            q, k, v = _norm_proj(xs, g_mix, weights, gains, tm_s)
            as_ = _diff_sample(q, k, v, cache_diff_k[i], cache_diff_v[i], page_table, rel_bias, diff_lambda[i],
                               diff_out_norm[i], lam_init, bs, t_new)
            dk_s.append(k.reshape(bs, t_new, DIFF_KV_HEADS, 2, HEAD_DIM))
            dv_s.append(v.reshape(bs, t_new, DIFF_KV_HEADS, DIFF_V_DIM))
        g_ffn = norm_ffn[layer].astype(F32)
        xp = _mix_ffn(xp, ap, wo, g_ffn, w1, w2, layer, 2 * tm_p)
        xs = _mix_ffn(xs, as_, wo, g_ffn, w1, w2, layer, tm_s)
    return (xp.reshape(bp, seq, d), xs.reshape(bs, t_new, d),
            jnp.stack(conv_p), jnp.stack(conv_s), ssm_p, ssm_s,
            jnp.stack(swa_p[0]), jnp.stack(swa_s[0]), jnp.stack(swa_p[1]), jnp.stack(swa_s[1]),
            jnp.stack(swa_p[2]), jnp.stack(swa_s[2]),
            jnp.stack(dk_p), jnp.stack(dk_s), jnp.stack(dv_p), jnp.stack(dv_s))
```

```python
import functools
import math

import jax
import jax.numpy as jnp
from jax import lax
from jax.experimental import pallas as pl
from jax.experimental.pallas import tpu as pltpu

F32 = jnp.float32
BF16 = jnp.bfloat16

D_MODEL = 1024
DEPTH = 4
D_FF = 2816
D_INNER = 2048
SSM_HEADS = 32
SSM_HEAD_DIM = 64
SSM_GROUPS = 4
D_STATE = 128
CONV_W = 4
CONV_DIM = D_INNER + 2 * SSM_GROUPS * D_STATE
SSD_CHUNK = 128
SHORT_CHUNK = 16
HEAD_DIM = 64
N_ATTN_HEADS = 12
NUM_BUCKETS = 32
MAX_DISTANCE = 2048
SWA_GROUPS = ((128, 1), (512, 4), (2048, 16))
SWA_HEADS_PER_GROUP = 4
SWA_GROUP_DIM = SWA_HEADS_PER_GROUP * HEAD_DIM
SWA_DIM = N_ATTN_HEADS * HEAD_DIM
SWA_STEPS = 128
DIFF_KV_HEADS = 4
DIFF_REP = 3
DIFF_V_DIM = 128
DIFF_Q_DIM = N_ATTN_HEADS * 2 * HEAD_DIM
DIFF_K_DIM = DIFF_KV_HEADS * 2 * HEAD_DIM
DIFF_V_ALL = DIFF_KV_HEADS * DIFF_V_DIM
PAGE_SIZE = 128
EPS = 1e-6
NEG = -1e30

LANES = 128
VMEM_LIMIT = 56 * 1024 * 1024
PAGES_PER_STEP = 32


def _dot(a, b):
    return jnp.dot(a, b, preferred_element_type=F32)


def _dot_nt(a, b):
    return lax.dot_general(a, b, (((1,), (1,)), ((), ())), preferred_element_type=F32)


def _split3(x):
    hi = x.astype(BF16)
    r = x - hi.astype(F32)
    mid = r.astype(BF16)
    lo = (r - mid.astype(F32)).astype(BF16)
    return hi, mid, lo


def _dot_exact_rhs(a3_bf16, x):
    return _dot(a3_bf16, jnp.concatenate(_split3(x), axis=0))


def _dot_exact_lhs(x, a3_bf16):
    return _dot(jnp.concatenate(_split3(x), axis=1), a3_bf16)


def _sigmoid(x):
    return 1.0 / (1.0 + jnp.exp(-x))


def _rms(x, g):
    return x * lax.rsqrt(jnp.mean(x * x, axis=-1, keepdims=True) + EPS) * g


def _resident(shape):
    n = len(shape)
    return pl.BlockSpec(shape, lambda *_: (0,) * n, pipeline_mode=pl.Buffered(1))


def _params(sem):
    return pltpu.CompilerParams(dimension_semantics=sem, vmem_limit_bytes=VMEM_LIMIT)


def _norm_proj_kernel(*refs, n_out, head_norm, emit):
    x_ref, g_ref = refs[0], refs[1]
    w_refs = refs[2:2 + n_out]
    n_hn = sum(head_norm)
    hn_refs = refs[2 + n_out:2 + n_out + n_hn]
    pos = 2 + n_out + n_hn
    seg_ref = refs[pos] if n_hn else None
    pos += 1 if n_hn else 0
    o_refs = list(refs[pos:])
    h = _rms(x_ref[...], g_ref[...]).astype(BF16)
    k = 0
    for i in range(n_out):
        y = _dot(h, w_refs[i][...])
        if head_norm[i]:
            gain = hn_refs[k][...]
            k += 1
            seg = seg_ref[...]
            parts = []
            for c in range(y.shape[1] // seg.shape[0]):
                yc = y[:, c * seg.shape[0]:(c + 1) * seg.shape[0]]
                sq = yc * yc
                hi = sq.astype(BF16)
                lo = (sq - hi.astype(F32)).astype(BF16)
                ms = (_dot(hi, seg) + _dot(lo, seg)) * (1.0 / HEAD_DIM)
                parts.append(yc * lax.rsqrt(ms + EPS))
            y = jnp.concatenate(parts, axis=1) * gain
        if "n" in emit[i]:
            o_refs.pop(0)[...] = y
        if "t" in emit[i]:
            o_refs.pop(0)[0] = y.T
        if "b" in emit[i]:
            ref = o_refs.pop(0)
            for cb in range(y.shape[1] // LANES):
                ref[0, cb] = y[:, cb * LANES:(cb + 1) * LANES]


def _norm_proj(x, g, weights, head_gains, tm, emit=None, seq=None):
    m, d = x.shape
    n_out = len(weights)
    emit = tuple(emit or ("n",) * n_out)
    head_norm = tuple(hg is not None for hg in head_gains)
    ins = [x, g.reshape(1, d)] + list(weights)
    specs = [pl.BlockSpec((tm, d), lambda i: (i, 0)), _resident((1, d))]
    specs += [_resident(w.shape) for w in weights]
    for hg in head_gains:
        if hg is not None:
            ins.append(hg)
            specs.append(_resident(hg.shape))
    if any(head_norm):
        r = jnp.arange(2 * LANES)
        seg = (r[:, None] // HEAD_DIM == r[None, :] // HEAD_DIM).astype(BF16)
        ins.append(seg)
        specs.append(_resident(seg.shape))
    per_seq = seq // tm if seq else 1
    out_specs, out_shape = [], []
    for w, e in zip(weights, emit):
        n = w.shape[1]
        if "n" in e:
            out_specs.append(pl.BlockSpec((tm, n), lambda i: (i, 0)))
            out_shape.append(jax.ShapeDtypeStruct((m, n), F32))
        if "t" in e:
            out_specs.append(pl.BlockSpec((1, n, tm), lambda i: (i // per_seq, 0, i % per_seq)))
            out_shape.append(jax.ShapeDtypeStruct((m // seq, n, seq), F32))
        if "b" in e:
            out_specs.append(pl.BlockSpec((1, n // LANES, tm, LANES), lambda i: (i // per_seq, 0, i % per_seq, 0)))
            out_shape.append(jax.ShapeDtypeStruct((m // seq, n // LANES, seq, LANES), F32))
    return pl.pallas_call(
        functools.partial(_norm_proj_kernel, n_out=n_out, head_norm=head_norm, emit=emit),
        grid=(m // tm,),
        in_specs=specs,
        out_specs=out_specs,
        out_shape=out_shape,
        compiler_params=_params(("parallel",)),
        name="norm_proj",
    )(*ins)


def _mix_ffn_kernel(x_ref, a_ref, wo_ref, g_ref, w1_ref, w2_ref, o_ref):
    x1 = x_ref[...] + _dot(a_ref[...], wo_ref[...])
    h = _rms(x1, g_ref[...]).astype(BF16)
    gu = _dot(h, w1_ref[0])
    gate = gu[:, :D_FF]
    act = (gate * _sigmoid(gate) * gu[:, D_FF:]).astype(BF16)
    o_ref[...] = x1 + _dot(act, w2_ref[0])


def _mix_ffn(x, a, wo, g, w1, w2, layer, tm):
    m, d = x.shape
    ka = a.shape[1]
    slab = lambda w: pl.BlockSpec((1,) + w.shape[1:], lambda i: (layer, 0, 0), pipeline_mode=pl.Buffered(1))
    return pl.pallas_call(
        _mix_ffn_kernel,
        grid=(m // tm,),
        in_specs=[pl.BlockSpec((tm, d), lambda i: (i, 0)),
                  pl.BlockSpec((tm, ka), lambda i: (i, 0)),
                  _resident(wo.shape), _resident((1, d)), slab(w1), slab(w2)],
        out_specs=pl.BlockSpec((tm, d), lambda i: (i, 0)),
        out_shape=jax.ShapeDtypeStruct((m, d), F32),
        compiler_params=_params(("parallel",)),
        name="mix_ffn",
    )(x, a, wo, g.reshape(1, d), w1, w2)


def _ssd_kernel(z_ref, xbc_ref, dt_ref, cst_ref, sst_ref, cw_ref, cb_ref, dtb_ref, alog_ref,
                dx_ref, ng_ref, e_ref, *rest, q, n_valid, n_prev):
    prev_ref = rest[0] if n_prev else None
    y_ref, ns_ref, ext_ref, st_ref, yacc_ref, *pad_refs = rest[1 if n_prev else 0:]
    c = pl.program_id(1)
    last = pl.num_programs(1) - 1
    padded = n_valid < q

    @pl.when(c == 0)
    def _():
        ext_ref[...] = jnp.zeros(ext_ref.shape, F32)
        ext_ref[5:8, :] = cst_ref[0]
        st_ref[...] = sst_ref[0, 0]
        if padded:
            for r in pad_refs:
                r[...] = jnp.zeros(r.shape, F32)

    ext_ref[8:8 + n_valid, :] = xbc_ref[0]
    ext = ext_ref[...]
    ext1 = pltpu.roll(ext, 1, axis=0)
    older = pltpu.roll(ext * cw_ref[1:2, :] + ext1 * cw_ref[0:1, :], 2, axis=0)
    conv = (cb_ref[...] + ext[8:8 + q] * cw_ref[3:4, :] + ext1[8:8 + q] * cw_ref[2:3, :]) + older[8:8 + q]
    if not padded:
        ext_ref[5:8, :] = ext_ref[q + 5:q + 8, :]
    act = conv * _sigmoid(conv)
    xs = act[:, :D_INNER]
    bm = act[:, D_INNER:D_INNER + SSM_GROUPS * D_STATE].astype(BF16)
    cm = act[:, D_INNER + SSM_GROUPS * D_STATE:].astype(BF16)

    if padded:
        zpad_ref, dtpad_ref = pad_refs
        zpad_ref[0:n_valid, :] = z_ref[0]
        dtpad_ref[0:n_valid, :] = dt_ref[0]
        z = zpad_ref[...]
        dt_raw = dtpad_ref[...]
    else:
        z = z_ref[0]
        dt_raw = dt_ref[0]

    row = lax.broadcasted_iota(jnp.int32, (q, q), 0)
    col = lax.broadcasted_iota(jnp.int32, (q, q), 1)
    causal = col <= row
    tri = jnp.where(causal, 1.0, 0.0).astype(BF16)
    tri_t = jnp.where(row <= col, 1.0, 0.0).astype(BF16)
    tri3 = jnp.concatenate([tri] * 3, axis=1)
    tri_t3 = jnp.concatenate([tri_t] * 3, axis=0)

    xdt_in = dt_raw + dtb_ref[...]
    dt = jnp.maximum(xdt_in, 0.0) + jnp.log(1.0 + jnp.exp(-jnp.abs(xdt_in)))
    if padded:
        dt = jnp.where(lax.broadcasted_iota(jnp.int32, dt.shape, 0) < n_valid, dt, 0.0)
    adt = dt * (-jnp.exp(alog_ref[...]))
    acum = _dot_exact_rhs(tri3, adt)
    acum_t = _dot_exact_lhs(adt.T, tri_t3)
    both_x = _dot_exact_lhs(jnp.concatenate([dt, acum], axis=0), e_ref[...])
    dt_x, acum_x = both_x[0:q], both_x[q:2 * q]
    exp_acum_x = jnp.exp(acum_x)
    acum_last_x = acum_x[q - 1:q, :]
    xdt = xs * dt_x
    xdt_b = xdt.astype(BF16)
    xd = xdt * jnp.exp(acum_last_x - acum_x)
    dtot_x = jnp.exp(acum_last_x)
    lane_lo = lax.broadcasted_iota(jnp.int32, (q, LANES), 1) < SSM_HEAD_DIM

    gw = SSM_HEADS // SSM_GROUPS * SSM_HEAD_DIM
    for g in range(SSM_GROUPS):
        cg = cm[:, g * D_STATE:(g + 1) * D_STATE]
        bg = bm[:, g * D_STATE:(g + 1) * D_STATE]
        cb = _dot_nt(cg, bg)
        st_g = st_ref[g * gw:(g + 1) * gw, :]
        yoff = _dot_nt(cg, st_g.astype(BF16))
        yacc_ref[:, g * gw:(g + 1) * gw] = yoff * exp_acum_x[:, g * gw:(g + 1) * gw]
        for pair in range(gw // LANES):
            h0 = g * (gw // SSM_HEAD_DIM) + 2 * pair
            xp = xdt_b[:, h0 * SSM_HEAD_DIM:h0 * SSM_HEAD_DIM + LANES]
            ys = []
            for h in (h0, h0 + 1):
                seg = jnp.broadcast_to(acum[:, h:h + 1], (q, q)) - jnp.broadcast_to(acum_t[h:h + 1, :], (q, q))
                w = cb * jnp.exp(jnp.where(causal, seg, NEG))
                ys.append(_dot(w.astype(BF16), xp))
            yacc_ref[:, h0 * SSM_HEAD_DIM:h0 * SSM_HEAD_DIM + LANES] += jnp.where(lane_lo, ys[0], ys[1])
        xd_t = jnp.concatenate(
            [xd[:, g * gw + t * LANES:g * gw + (t + 1) * LANES].T for t in range(gw // LANES)], axis=0)
        upd = _dot(xd_t.astype(BF16), bg)
        dtot = jnp.concatenate(
            [jnp.broadcast_to(dtot_x[:, g * gw + t * LANES:g * gw + (t + 1) * LANES], (LANES, LANES)).T
             for t in range(gw // LANES)], axis=0)
        st_ref[g * gw:(g + 1) * gw, :] = dtot * st_g + upd

    y = yacc_ref[...] + xs * dx_ref[...]
    gated = y * (z * _sigmoid(z))
    outs = []
    for g in range(SSM_GROUPS):
        gg = gated[:, g * gw:(g + 1) * gw]
        outs.append(gg * lax.rsqrt(jnp.mean(gg * gg, axis=-1, keepdims=True) + EPS))
    yn = jnp.concatenate(outs, axis=1) * ng_ref[...]
    y_ref[0] = yn[0:n_valid].astype(BF16)

    @pl.when(c == last)
    def _():
        for k in range(n_prev):
            ns_ref[k, 0] = prev_ref[k, 0]
        ns_ref[n_prev, 0] = st_ref[...]


def _ssd_mixer(z, xbc, dt, conv_state, ssm_states, state_idx, conv_w, conv_b, dt_bias, a_log, d_skip, norm_g,
               prev_new=None):
    bt, seq, _ = z.shape
    q = SSD_CHUNK if seq >= SSD_CHUNK else SHORT_CHUNK
    n_valid = min(q, seq)
    nc = max(1, seq // q)
    pad = lambda v: jnp.pad(v.astype(F32), (0, LANES - SSM_HEADS)).reshape(1, LANES)
    hh = jnp.arange(LANES)[:, None]
    cc = jnp.arange(D_INNER)[None, :] // SSM_HEAD_DIM
    expand = jnp.tile((hh == cc).astype(BF16), (3, 1))
    dx = jnp.repeat(d_skip.astype(F32), SSM_HEAD_DIM).reshape(1, D_INNER)
    rows = SSM_HEADS * SSM_HEAD_DIM
    sst = ssm_states.reshape(ssm_states.shape[0], bt, rows, D_STATE)
    n_prev = 0 if prev_new is None else prev_new.shape[0]
    prev_in = [] if prev_new is None else [prev_new.reshape(n_prev, bt, rows, D_STATE)]
    prev_spec = [pl.BlockSpec((n_prev, 1, rows, D_STATE), lambda b, c: (0, b, 0, 0))] * len(prev_in)
    scratch = [pltpu.VMEM((q + 8, CONV_DIM), F32),
               pltpu.VMEM((SSM_HEADS * SSM_HEAD_DIM, D_STATE), F32),
               pltpu.VMEM((q, D_INNER), F32)]
    if n_valid < q:
        scratch += [pltpu.VMEM((q, D_INNER), F32), pltpu.VMEM((q, LANES), F32)]
    y, ns = pl.pallas_call(
        functools.partial(_ssd_kernel, q=q, n_valid=n_valid, n_prev=n_prev),
        grid=(bt, nc),
        in_specs=[pl.BlockSpec((1, n_valid, D_INNER), lambda b, c: (b, c, 0)),
                  pl.BlockSpec((1, n_valid, CONV_DIM), lambda b, c: (b, c, 0)),
                  pl.BlockSpec((1, n_valid, LANES), lambda b, c: (b, c, 0)),
                  pl.BlockSpec((1, CONV_W - 1, CONV_DIM), lambda b, c: (b, 0, 0)),
                  pl.BlockSpec((1, 1, SSM_HEADS * SSM_HEAD_DIM, D_STATE), lambda b, c: (state_idx, b, 0, 0)),
                  _resident((CONV_W, CONV_DIM)), _resident((1, CONV_DIM)),
                  _resident((1, LANES)), _resident((1, LANES)),
                  _resident((1, D_INNER)), _resident((1, D_INNER)), _resident((3 * LANES, D_INNER))] + prev_spec,
        out_specs=[pl.BlockSpec((1, n_valid, D_INNER), lambda b, c: (b, c, 0)),
                   pl.BlockSpec((n_prev + 1, 1, rows, D_STATE), lambda b, c: (0, b, 0, 0))],
        out_shape=[jax.ShapeDtypeStruct((bt, seq, D_INNER), BF16),
                   jax.ShapeDtypeStruct((n_prev + 1, bt, rows, D_STATE), F32)],
        scratch_shapes=scratch,
        compiler_params=_params(("parallel", "arbitrary")),
        name="ssd_mixer",
    )(z, xbc, dt, conv_state, sst, conv_w, conv_b.reshape(1, CONV_DIM), pad(dt_bias), pad(a_log),
      dx, norm_g.reshape(1, D_INNER), expand, *prev_in)
    return y, ns.reshape(n_prev + 1, bt, SSM_HEADS, SSM_HEAD_DIM, D_STATE)


def _t5_bucket(dist):
    max_exact = NUM_BUCKETS // 2
    d = jnp.maximum(dist, 0)
    df = jnp.maximum(d, 1).astype(F32)
    large = max_exact + (jnp.log(df / max_exact) / math.log(MAX_DISTANCE / max_exact)
                         * (NUM_BUCKETS - max_exact)).astype(jnp.int32)
    return jnp.where(d < max_exact, d, jnp.minimum(large, NUM_BUCKETS - 1))


def _bias_lookup(tab, dist):
    onehot = jax.nn.one_hot(_t5_bucket(dist), NUM_BUCKETS, dtype=F32)
    return jnp.einsum('...b,bh->...h', onehot, tab.astype(F32), precision=lax.Precision.HIGHEST)


def _swa_prompt_bias(rel_bias, g, dil):
    i = jnp.arange(SWA_STEPS)[:, None]
    j = jnp.arange(2 * SWA_STEPS)[None, :]
    step = SWA_STEPS + i - j
    tab = rel_bias[:, g * SWA_HEADS_PER_GROUP:(g + 1) * SWA_HEADS_PER_GROUP].astype(F32)
    bias = jnp.transpose(_bias_lookup(tab, jnp.clip(step, 0, SWA_STEPS) * dil), (2, 0, 1))
    valid = (step >= 0) & (step <= SWA_STEPS)
    first = valid & (j >= SWA_STEPS)
    both = jnp.stack([jnp.where(first, bias, NEG), jnp.where(valid, bias, NEG)])
    return both.reshape(2, SWA_HEADS_PER_GROUP * SWA_STEPS, 2 * SWA_STEPS)


def _swa_fused_kernel(q_ref, k_ref, v_ref, b_ref, a_ref, qs_ref, ks_ref, vs_ref, oc_ref, lc_ref, ot_ref, lt_ref,
                      *, seq):
    g = pl.program_id(1)
    gd = SWA_GROUP_DIM
    blk = SWA_STEPS
    n_units = seq // blk
    lane_head = lax.broadcasted_iota(jnp.int32, (blk, gd), 1) // HEAD_DIM

    def group_body(gi, dil):
        rows = seq // dil
        nb = rows // blk
        ks_ref[0:blk, :] = jnp.zeros((blk, gd), BF16)
        vs_ref[0:blk, :] = jnp.zeros((blk, gd), BF16)
        for c in range(dil):
            src = pl.ds(c, rows, stride=dil) if dil > 1 else pl.ds(0, rows)
            dst = slice(blk + c * rows, blk + (c + 1) * rows)
            for half in range(gd // LANES):
                lanes = slice(half * LANES, (half + 1) * LANES)
                qs_ref[dst, lanes] = (q_ref[0, half, src, :] * (HEAD_DIM ** -0.5)).astype(BF16)
                ks_ref[dst, lanes] = k_ref[0, half, src, :].astype(BF16)
                vs_ref[dst, lanes] = v_ref[0, half, src, :].astype(BF16)

        def unit(u, carry):
            start = pl.multiple_of(u * blk, blk)
            first = lax.rem(u, nb) == 0
            qv = qs_ref[pl.ds(blk + start, blk), :]
            kk = ks_ref[pl.ds(start, 2 * blk), :]
            vv = vs_ref[pl.ds(start, 2 * blk), :]
            sel = jnp.where(first, 0, 1)
            nh = SWA_HEADS_PER_GROUP
            qh = jnp.concatenate([jnp.where(lane_head == h, qv, jnp.zeros_like(qv)) for h in range(nh)], axis=0)
            s = _dot_nt(qh, kk) + b_ref[gi, sel]
            m = jnp.max(s, axis=-1, keepdims=True)
            p = jnp.exp(s - m)
            l = jnp.sum(p, axis=-1, keepdims=True)
            of = _dot(p.astype(BF16), vv) / l
            lf = jnp.broadcast_to(m + jnp.log(l), (nh * blk, gd))
            o, lse = of[0:blk], lf[0:blk]
            for h in range(1, nh):
                o = jnp.where(lane_head == h, of[h * blk:(h + 1) * blk], o)
                lse = jnp.where(lane_head == h, lf[h * blk:(h + 1) * blk], lse)
            oc_ref[pl.ds(start, blk), :] = o
            lc_ref[pl.ds(start, blk), :] = lse
            return carry

        lax.fori_loop(0, n_units, unit, 0, unroll=2)
        for c in range(dil):
            dst = pl.ds(c, rows, stride=dil) if dil > 1 else pl.ds(0, rows)
            for half in range(gd // LANES):
                lanes = slice(half * LANES, (half + 1) * LANES)
                cb = gi * (gd // LANES) + half
                ot_ref[cb, dst, :] = oc_ref[c * rows:(c + 1) * rows, lanes]
                lt_ref[cb, dst, :] = lc_ref[c * rows:(c + 1) * rows, lanes]

    for gi, (_, dil) in enumerate(SWA_GROUPS):
        pl.when(g == gi)(functools.partial(group_body, gi, dil))

    @pl.when(g == len(SWA_GROUPS) - 1)
    def _():
        def tile(t, carry):
            r = pl.ds(pl.multiple_of(t * blk, blk), blk)
            per = gd // LANES
            for half in range(per):
                ls = [lt_ref[gi * per + half, r, :] for gi in range(len(SWA_GROUPS))]
                m = jnp.maximum(jnp.maximum(ls[0], ls[1]), ls[2])
                es = [jnp.exp(l - m) for l in ls]
                inv = 1.0 / (es[0] + es[1] + es[2])
                for gi in range(len(SWA_GROUPS)):
                    cb = gi * per + half
                    a_ref[0, r, cb * LANES:(cb + 1) * LANES] = (ot_ref[cb, r, :] * es[gi] * inv).astype(BF16)
            return carry

        lax.fori_loop(0, n_units, tile, 0)


def _swa_prompt(q, k, v, rel_bias, bt, seq):
    bias = jnp.stack([_swa_prompt_bias(rel_bias, g, dil) for g, (_, dil) in enumerate(SWA_GROUPS)])
    gd = SWA_GROUP_DIM
    tok = pl.BlockSpec((1, gd // LANES, seq, LANES), lambda b, g: (b, g, 0, 0))
    a = pl.pallas_call(
        functools.partial(_swa_fused_kernel, seq=seq),
        grid=(bt, len(SWA_GROUPS)),
        in_specs=[tok, tok, tok, _resident(bias.shape)],
        out_specs=pl.BlockSpec((1, seq, SWA_DIM), lambda b, g: (b, 0, 0)),
        out_shape=jax.ShapeDtypeStruct((bt, seq, SWA_DIM), BF16),
        scratch_shapes=[pltpu.VMEM((seq + SWA_STEPS, gd), BF16)] * 3
                       + [pltpu.VMEM((seq, gd), F32)] * 2 + [pltpu.VMEM((SWA_DIM // LANES, seq, LANES), F32)] * 2,
        compiler_params=_params(("parallel", "arbitrary")),
        name="swa_prompt",
    )(q, k, v, bias)
    return a.reshape(bt * seq, SWA_DIM)


def _swa_sample_kernel(q_ref, kn_ref, c0_ref, c1_ref, c2_ref, b0_ref, b1_ref, b2_ref, nb_ref,
                       a_ref, o0_ref, o1_ref, o2_ref, *, t_new):
    gd = SWA_GROUP_DIM
    nh = SWA_HEADS_PER_GROUP
    lane_head = lax.broadcasted_iota(jnp.int32, (t_new, gd), 1) // HEAD_DIM
    caches = (c0_ref, c1_ref, c2_ref)
    biases = (b0_ref, b1_ref, b2_ref)
    new_caches = (o0_ref, o1_ref, o2_ref)
    outs, lses = [], []
    for g in range(len(SWA_GROUPS)):
        c_ref = caches[g]
        win = c_ref.shape[2]
        qg = q_ref[0, :, g * gd:(g + 1) * gd] * (HEAD_DIM ** -0.5)
        qs = jnp.concatenate([jnp.where(lane_head == h, qg, 0.0) for h in range(nh)], axis=0).astype(BF16)
        new = kn_ref[0, g]
        s = _dot(qs, c_ref[0, 0:gd, :].astype(BF16)) + biases[g][...]
        sn = _dot(qs, new[0:gd].astype(BF16)) + nb_ref[g]
        m = jnp.maximum(jnp.max(s, axis=-1, keepdims=True), jnp.max(sn, axis=-1, keepdims=True))
        p = jnp.exp(s - m)
        pn = jnp.exp(sn - m)
        l = jnp.sum(p, axis=-1, keepdims=True) + jnp.sum(pn, axis=-1, keepdims=True)
        of = (_dot_nt(p.astype(BF16), c_ref[0, gd:2 * gd, :].astype(BF16))
              + _dot_nt(pn.astype(BF16), new[gd:2 * gd].astype(BF16))) / l
        lf = jnp.broadcast_to(m + jnp.log(l), (nh * t_new, gd))
        o = jnp.zeros((t_new, gd), F32)
        lse = jnp.zeros((t_new, gd), F32)
        for h in range(nh):
            o = jnp.where(lane_head == h, of[h * t_new:(h + 1) * t_new], o)
            lse = jnp.where(lane_head == h, lf[h * t_new:(h + 1) * t_new], lse)
        outs.append(o)
        lses.append(lse)
        rolled = pltpu.roll(c_ref[0], win - t_new, axis=1)
        if win > LANES:
            new_caches[g][0, :, 0:win - LANES] = rolled[:, 0:win - LANES]
        keep = lax.broadcasted_iota(jnp.int32, (2 * gd, LANES), 1) < LANES - t_new
        new_caches[g][0, :, win - LANES:win] = jnp.where(keep, rolled[:, win - LANES:win], new)
    m = jnp.maximum(jnp.maximum(lses[0], lses[1]), lses[2])
    es = [jnp.exp(l - m) for l in lses]
    inv = 1.0 / (es[0] + es[1] + es[2])
    for g in range(len(SWA_GROUPS)):
        a_ref[0, :, g * gd:(g + 1) * gd] = (outs[g] * es[g] * inv).astype(BF16)


def _swa_sample_bias(rel_bias, t_new):
    t = jnp.arange(t_new)
    biases, new_biases = [], []
    for g, (win, dil) in enumerate(SWA_GROUPS):
        tab = rel_bias[:, g * SWA_HEADS_PER_GROUP:(g + 1) * SWA_HEADS_PER_GROUP].astype(F32)
        back = t[:, None] - jnp.arange(win)[None, :]
        ok = (back <= 0) & (back % dil == 0)
        b = jnp.where(ok[..., None], _bias_lookup(tab, win + back), NEG)
        biases.append(jnp.transpose(b, (2, 0, 1)).reshape(SWA_HEADS_PER_GROUP * t_new, win))
        j = jnp.arange(LANES)[None, :] - (LANES - t_new)
        nd = t[:, None] - j
        nok = (j >= 0) & (nd >= 0) & (nd % dil == 0)
        nb = jnp.where(nok[..., None], _bias_lookup(tab, nd), NEG)
        new_biases.append(jnp.transpose(nb, (2, 0, 1)).reshape(SWA_HEADS_PER_GROUP * t_new, LANES))
    return biases, jnp.stack(new_biases)


def _swa_sample(q, k, v, caches, rel_bias, bt, t_new):
    biases, new_bias = _swa_sample_bias(rel_bias, t_new)
    gd = SWA_GROUP_DIM
    tiles = []
    for g in range(len(SWA_GROUPS)):
        kv = jnp.concatenate([k[:, :, g * gd:(g + 1) * gd], v[:, :, g * gd:(g + 1) * gd]], axis=2)
        tiles.append(jnp.pad(jnp.transpose(kv, (0, 2, 1)), ((0, 0), (0, 0), (LANES - t_new, 0))))
    kn = jnp.stack(tiles, axis=1)
    views = [jnp.transpose(c, (0, 2, 3, 4, 1)).reshape(bt, 2 * gd, c.shape[1]) for c in caches]
    cspecs = [pl.BlockSpec((1, 2 * gd, c.shape[2]), lambda b: (b, 0, 0)) for c in views]
    tok = pl.BlockSpec((1, t_new, SWA_DIM), lambda b: (b, 0, 0))
    res = pl.pallas_call(
        functools.partial(_swa_sample_kernel, t_new=t_new),
        grid=(bt,),
        in_specs=[tok, pl.BlockSpec((1, len(SWA_GROUPS), 2 * gd, LANES), lambda b: (b, 0, 0, 0))] + cspecs
                 + [_resident(b.shape) for b in biases] + [_resident(new_bias.shape)],
        out_specs=[tok] + cspecs,
        out_shape=[jax.ShapeDtypeStruct((bt, t_new, SWA_DIM), BF16)]
                  + [jax.ShapeDtypeStruct(c.shape, F32) for c in views],
        compiler_params=_params(("parallel",)),
        name="swa_sample",
    )(q, kn, *views, *biases, new_bias)
    new_caches = [jnp.transpose(c.reshape(bt, 2, SWA_HEADS_PER_GROUP, HEAD_DIM, c.shape[2]), (0, 4, 1, 2, 3))
                  for c in res[1:]]
    return res[0].reshape(bt * t_new, SWA_DIM), new_caches


def _diff_lambda(lam_ref, lam_init):
    lp = lam_ref[...]
    s1 = jnp.sum(lp[0:1] * lp[1:2], axis=-1, keepdims=True)
    s2 = jnp.sum(lp[2:3] * lp[3:4], axis=-1, keepdims=True)
    return jnp.exp(s1) - jnp.exp(s2) + lam_init


def _diff_prompt_kernel(q_ref, k_ref, v_ref, b_ref, lam_ref, on_ref, o_ref, q6_ref, m_ref, acc_ref,
                        *, tq, tk, lam_init):
    i = pl.program_id(2)
    nsub = tk // LANES
    qsub = tq // LANES
    lane_lo = lax.broadcasted_iota(jnp.int32, (tq, LANES), 1) < HEAD_DIM
    for r in range(DIFF_REP):
        qr = q_ref[0, :, r * LANES:(r + 1) * LANES] * (HEAD_DIM ** -0.5)
        q6_ref[(2 * r) * tq:(2 * r + 1) * tq, :] = jnp.where(lane_lo, qr, 0.0).astype(BF16)
        q6_ref[(2 * r + 1) * tq:(2 * r + 2) * tq, :] = jnp.where(lane_lo, 0.0, qr).astype(BF16)
    m_ref[...] = jnp.full(m_ref.shape, NEG, F32)
    acc_ref[...] = jnp.zeros(acc_ref.shape, F32)
    ones = jnp.ones((tk, LANES), BF16)

    def body(j, carry):
        start = pl.multiple_of(j * tk, tk)
        kb = k_ref[0, pl.ds(start, tk), :].astype(BF16)
        vx = jnp.concatenate([v_ref[0, pl.ds(start, tk), :].astype(BF16), ones], axis=1)
        d0 = i * qsub - j * nsub + (nsub - 1)
        for r in range(DIFF_REP):
            rows = slice(2 * r * tq, (2 * r + 2) * tq)
            s = _dot_nt(q6_ref[rows, :], kb)
            bias = jnp.concatenate(
                [jnp.concatenate([b_ref[0, d0 - c + qs, r] for qs in range(qsub)], axis=0) for c in range(nsub)],
                axis=1)
            s = s + jnp.concatenate([bias, bias], axis=0)
            smax = s[:, 0:LANES]
            for c in range(1, tk // LANES):
                smax = jnp.maximum(smax, s[:, c * LANES:(c + 1) * LANES])
            m_old = m_ref[rows, :]
            m_new = jnp.maximum(m_old, jnp.max(smax, axis=-1, keepdims=True))
            alpha = jnp.exp(m_old - m_new)
            p = jnp.concatenate([jnp.exp(s[:, c * LANES:(c + 1) * LANES] - m_new)
                                 for c in range(tk // LANES)], axis=1).astype(BF16)
            acc_ref[rows, :] = jnp.concatenate([alpha, alpha], axis=1) * acc_ref[rows, :] + _dot(p, vx)
            m_ref[rows, :] = m_new
        return carry

    lax.fori_loop(0, ((i + 1) * tq - 1) // tk + 1, body, 0)
    lam = _diff_lambda(lam_ref, lam_init)
    for r in range(DIFF_REP):
        o0 = acc_ref[(2 * r) * tq:(2 * r + 1) * tq, :]
        o1 = acc_ref[(2 * r + 1) * tq:(2 * r + 2) * tq, :]
        a = o0[:, :DIFF_V_DIM] / o0[:, DIFF_V_DIM:] - lam * (o1[:, :DIFF_V_DIM] / o1[:, DIFF_V_DIM:])
        o_ref[0, :, r * LANES:(r + 1) * LANES] = (_rms(a, on_ref[...]) * (1.0 - lam_init)).astype(BF16)


def _diff_prompt_bias(rel_bias, seq, tk):
    nd = seq // LANES + tk // LANES - 1
    delta = jnp.arange(nd) - (tk // LANES - 1)
    d = delta[:, None, None] * LANES + jnp.arange(LANES)[None, :, None] - jnp.arange(LANES)[None, None, :]
    tab = jnp.where((d >= 0)[..., None], _bias_lookup(rel_bias, d), NEG)
    tab = jnp.transpose(tab, (3, 0, 1, 2)).reshape(DIFF_KV_HEADS, DIFF_REP, nd, LANES, LANES)
    return jnp.transpose(tab, (0, 2, 1, 3, 4))


def _diff_prompt(q, k, v, rel_bias, lam_p, out_norm, lam_init, bt, seq):
    tq, tk = 512, 512
    nb = seq // tq
    bias = _diff_prompt_bias(rel_bias, seq, tk)
    qw = DIFF_REP * 2 * HEAD_DIM
    a = pl.pallas_call(
        functools.partial(_diff_prompt_kernel, tq=tq, tk=tk, lam_init=lam_init),
        grid=(bt, DIFF_KV_HEADS, nb),
        in_specs=[pl.BlockSpec((1, tq, qw), lambda b, g, i: (b, i, g)),
                  pl.BlockSpec((1, seq, LANES), lambda b, g, i: (b, 0, g)),
                  pl.BlockSpec((1, seq, LANES), lambda b, g, i: (b, 0, g)),
                  pl.BlockSpec((1,) + bias.shape[1:], lambda b, g, i: (g, 0, 0, 0, 0)),
                  _resident((4, HEAD_DIM)), _resident((1, DIFF_V_DIM))],
        out_specs=pl.BlockSpec((1, tq, qw), lambda b, g, i: (b, i, g)),
        out_shape=jax.ShapeDtypeStruct((bt, seq, DIFF_Q_DIM), BF16),
        scratch_shapes=[pltpu.VMEM((2 * DIFF_REP * tq, LANES), BF16),
                        pltpu.VMEM((2 * DIFF_REP * tq, LANES), F32),
                        pltpu.VMEM((2 * DIFF_REP * tq, 2 * DIFF_V_DIM), F32)],
        compiler_params=_params(("parallel", "parallel", "arbitrary")),
        name="diff_prompt",
    )(q.reshape(bt, seq, DIFF_Q_DIM), k.reshape(bt, seq, DIFF_K_DIM), v.reshape(bt, seq, DIFF_V_ALL),
      bias, lam_p.astype(F32), out_norm.reshape(1, DIFF_V_DIM).astype(F32))
    return a.reshape(bt * seq, DIFF_Q_DIM)


def _diff_sample_kernel(pt_ref, qb_ref, kn_ref, vn_ref, *refs, t_new, lam_init):
    npg = PAGES_PER_STEP
    k_refs = refs[:npg]
    v_refs = refs[npg:2 * npg]
    b_ref, nb_ref, lam_ref, on_ref, o_ref, m_ref, l_ref, acc_ref, qs_ref = refs[2 * npg:]
    s_idx = pl.program_id(1)
    per_map = DIFF_REP * t_new
    per_g = 2 * per_map

    @pl.when(s_idx == 0)
    def _():
        m_ref[...] = jnp.full(m_ref.shape, NEG, F32)
        l_ref[...] = jnp.zeros(l_ref.shape, F32)
        acc_ref[...] = jnp.zeros(acc_ref.shape, F32)
        qr = qb_ref[0] * (HEAD_DIM ** -0.5)
        own = lax.broadcasted_iota(jnp.int32, qr.shape, 0) // per_map
        for c in range(2 * DIFF_KV_HEADS):
            qs_ref[:, c * HEAD_DIM:(c + 1) * HEAD_DIM] = jnp.where(own == c, qr, 0.0)

    qb = qs_ref[...].astype(BF16)

    def absorb(kt_refs, vv_refs, bias):
        kcat = jnp.concatenate([kt[0].astype(BF16) for kt in kt_refs], axis=1)
        s = _dot(qb, kcat) + bias
        chunks = [s[:, c * LANES:(c + 1) * LANES] for c in range(len(kt_refs))]
        smax = chunks[0]
        for c in chunks[1:]:
            smax = jnp.maximum(smax, c)
        m_old = m_ref[...]
        m_new = jnp.maximum(m_old, jnp.max(smax, axis=-1, keepdims=True))
        alpha = jnp.exp(m_old - m_new)
        ps = [jnp.exp(c - m_new) for c in chunks]
        psum = ps[0]
        for p in ps[1:]:
            psum = psum + p
        l_ref[...] = alpha * l_ref[...] + jnp.sum(psum, axis=-1, keepdims=True)
        pcat = jnp.concatenate(ps, axis=1).astype(BF16)
        for g in range(DIFF_KV_HEADS):
            rows = slice(g * per_g, (g + 1) * per_g)
            vcat = jnp.concatenate([vv[0, pl.ds(g, PAGE_SIZE, stride=DIFF_KV_HEADS), :].astype(BF16)
                                    for vv in vv_refs], axis=0)
            acc_ref[rows, :] = alpha[rows] * acc_ref[rows, :] + _dot(pcat[rows], vcat)
        m_ref[...] = m_new

    absorb(k_refs, v_refs, b_ref[0])

    @pl.when(s_idx == pl.num_programs(1) - 1)
    def _():
        absorb([kn_ref], [vn_ref], nb_ref[...])
        lam = _diff_lambda(lam_ref, lam_init)
        o = acc_ref[...] / l_ref[...]
        for g in range(DIFF_KV_HEADS):
            for r in range(DIFF_REP):
                r0 = g * per_g + r * t_new
                blk = o[r0:r0 + t_new] - lam * o[r0 + per_map:r0 + per_map + t_new]
                col = (g * DIFF_REP + r) * DIFF_V_DIM
                o_ref[0, :, col:col + DIFF_V_DIM] = (_rms(blk, on_ref[...]) * (1.0 - lam_init)).astype(BF16)


def _diff_sample(q, k, v, cache_k, cache_v, page_table, rel_bias, lam_p, out_norm, lam_init, bt, t_new):
    n_pages = page_table.shape[1]
    past = n_pages * PAGE_SIZE
    n_phys = cache_k.shape[0]
    nrow = DIFF_KV_HEADS * 2 * DIFF_REP * t_new
    q6 = q.reshape(bt, t_new, DIFF_KV_HEADS, DIFF_REP, 2, HEAD_DIM)
    qb = jnp.transpose(q6, (0, 2, 4, 3, 1, 5)).reshape(bt, nrow, HEAD_DIM)
    tab = rel_bias.astype(F32)
    tt = jnp.arange(t_new)

    def rows_of(b):
        n = b.shape[1]
        b = jnp.transpose(b, (2, 0, 1)).reshape(DIFF_KV_HEADS, 1, DIFF_REP, t_new, n)
        return jnp.broadcast_to(b, (DIFF_KV_HEADS, 2, DIFF_REP, t_new, n)).reshape(nrow, n)

    dist = past + tt[:, None] - jnp.arange(past)[None, :]
    npg = PAGES_PER_STEP
    bias = rows_of(_bias_lookup(tab, dist)).reshape(nrow, n_pages // npg, npg * PAGE_SIZE)
    bias = jnp.transpose(bias, (1, 0, 2))
    nd = tt[:, None] - jnp.arange(PAGE_SIZE)[None, :]
    nbias = rows_of(jnp.where((nd >= 0)[..., None], _bias_lookup(tab, nd), NEG))
    kn = jnp.transpose(k.reshape(bt, t_new, DIFF_K_DIM), (0, 2, 1))
    kn = jnp.pad(kn, ((0, 0), (0, 0), (0, PAGE_SIZE - t_new)))
    vn = jnp.pad(v.reshape(bt, t_new * DIFF_KV_HEADS, DIFF_V_DIM),
                 ((0, 0), (0, (PAGE_SIZE - t_new) * DIFF_KV_HEADS), (0, 0)))
    ck = jnp.transpose(cache_k, (0, 2, 3, 4, 1)).reshape(n_phys, DIFF_K_DIM, PAGE_SIZE)
    cv = cache_v.reshape(n_phys, PAGE_SIZE * DIFF_KV_HEADS, DIFF_V_DIM)

    def page_spec(pp):
        return pl.BlockSpec((1, DIFF_K_DIM, PAGE_SIZE), lambda b, s, pt: (pt[b, s * npg + pp], 0, 0))

    const = lambda shape: pl.BlockSpec(shape, lambda b, s, pt: (0,) * len(shape))
    grid_spec = pltpu.PrefetchScalarGridSpec(
        num_scalar_prefetch=1,
        grid=(bt, n_pages // npg),
        in_specs=[pl.BlockSpec((1, nrow, HEAD_DIM), lambda b, s, pt: (b, 0, 0)),
                  pl.BlockSpec((1, DIFF_K_DIM, PAGE_SIZE), lambda b, s, pt: (b, 0, 0)),
                  pl.BlockSpec((1, DIFF_K_DIM, PAGE_SIZE), lambda b, s, pt: (b, 0, 0))]
                 + [page_spec(pp) for pp in range(npg)] * 2
                 + [pl.BlockSpec((1, nrow, npg * PAGE_SIZE), lambda b, s, pt: (s, 0, 0)),
                    const((nrow, PAGE_SIZE)), const((4, HEAD_DIM)), const((1, DIFF_V_DIM))],
        out_specs=pl.BlockSpec((1, t_new, DIFF_Q_DIM), lambda b, s, pt: (b, 0, 0)),
        scratch_shapes=[pltpu.VMEM((nrow, LANES), F32), pltpu.VMEM((nrow, LANES), F32),
                        pltpu.VMEM((nrow, DIFF_V_DIM), F32), pltpu.VMEM((nrow, DIFF_K_DIM), F32)],
    )
    a = pl.pallas_call(
        functools.partial(_diff_sample_kernel, t_new=t_new, lam_init=lam_init),
        grid_spec=grid_spec,
        out_shape=jax.ShapeDtypeStruct((bt, t_new, DIFF_Q_DIM), BF16),
        compiler_params=_params(("parallel", "arbitrary")),
        name="diff_sample",
    )(page_table, qb, kn, vn, *([ck] * npg), *([cv] * npg), bias, nbias,
      lam_p.astype(F32), out_norm.reshape(1, DIFF_V_DIM).astype(F32))
    return a.reshape(bt * t_new, DIFF_Q_DIM)


def kernel(x_prompt, x_sample, state_ssm_conv, state_ssm, cache_swa_kv0, cache_swa_kv1, cache_swa_kv2, cache_diff_k, cache_diff_v, page_table, rel_bias, norm_mix, norm_ffn, ffn_w_in, ffn_w_out, ssm_w_in, ssm_conv_w, ssm_conv_b, ssm_dt_bias, ssm_a_log, ssm_d, ssm_norm, ssm_w_out, swa_w_qkv, swa_q_norm, swa_k_norm, swa_w_out, diff_w_qkv, diff_q_norm, diff_k_norm, diff_lambda, diff_out_norm, diff_w_out):
    bp, seq, d = x_prompt.shape
    bs, t_new, _ = x_sample.shape
    mp, ms = bp * seq, bs * t_new
    tm_p, tm_s = 256, ms
    xp = x_prompt.reshape(mp, d)
    xs = x_sample.reshape(ms, d)
    swa_caches = (cache_swa_kv0, cache_swa_kv1, cache_swa_kv2)
    conv_p, conv_s = [], []
    ssm_p = ssm_s = None
    swa_p = tuple([] for _ in SWA_GROUPS)
    swa_s = tuple([] for _ in SWA_GROUPS)
    dk_p, dk_s, dv_p, dv_s = [], [], [], []
    i_ssd = i_swa = i_diff = 0
    w1, w2 = ffn_w_in.astype(BF16), ffn_w_out.astype(BF16)
    for layer in range(DEPTH):
        kind = layer % 3
        g_mix = norm_mix[layer].astype(F32)
        if kind == 0:
            i = i_ssd
            i_ssd += 1
            w_in = ssm_w_in[i].astype(BF16)
            weights = [w_in[:, :D_INNER], w_in[:, D_INNER:D_INNER + CONV_DIM],
                       jnp.pad(w_in[:, D_INNER + CONV_DIM:], ((0, 0), (0, LANES - SSM_HEADS)))]
            wo = ssm_w_out[i].astype(BF16)
            cw, cb = ssm_conv_w[i].astype(F32), ssm_conv_b[i].astype(F32)
            ssd_args = (cw, cb, ssm_dt_bias[i], ssm_a_log[i], ssm_d[i], ssm_norm[i].astype(F32))
            z, xbc, dt = _norm_proj(xp, g_mix, weights, [None] * 3, tm_p)
            xbc3 = xbc.reshape(bp, seq, CONV_DIM)
            y, ns = _ssd_mixer(z.reshape(bp, seq, D_INNER), xbc3, dt.reshape(bp, seq, LANES),
                               jnp.zeros((bp, CONV_W - 1, CONV_DIM), F32),
                               jnp.zeros((1, bp, SSM_HEADS, SSM_HEAD_DIM, D_STATE), F32), 0, *ssd_args,
                               prev_new=ssm_p)
            conv_p.append(xbc3[:, seq - (CONV_W - 1):])
            ssm_p = ns
            ap = y.reshape(mp, D_INNER)
            z, xbc, dt = _norm_proj(xs, g_mix, weights, [None] * 3, tm_s)
            xbc3 = xbc.reshape(bs, t_new, CONV_DIM)
            y, ns = _ssd_mixer(z.reshape(bs, t_new, D_INNER), xbc3, dt.reshape(bs, t_new, LANES),
                               state_ssm_conv[i], state_ssm, i, *ssd_args, prev_new=ssm_s)
            conv_s.append(xbc3[:, t_new - (CONV_W - 1):])
            ssm_s = ns
            as_ = y.reshape(ms, D_INNER)
        elif kind == 1:
            i = i_swa
            i_swa += 1
            w_qkv = swa_w_qkv[i].astype(BF16)
            weights = [w_qkv[:, :SWA_DIM], w_qkv[:, SWA_DIM:2 * SWA_DIM], w_qkv[:, 2 * SWA_DIM:]]
            gains = [jnp.tile(swa_q_norm[i].astype(F32), N_ATTN_HEADS).reshape(1, SWA_DIM),
                     jnp.tile(swa_k_norm[i].astype(F32), N_ATTN_HEADS).reshape(1, SWA_DIM), None]
            wo = swa_w_out[i].astype(BF16)
            q, kt, k, vt, v = _norm_proj(xp, g_mix, weights, gains, tm_p, emit=("b", "tb", "tb"), seq=seq)
            ap = _swa_prompt(q, k, v, rel_bias, bp, seq)
            for g, (win, dil) in enumerate(SWA_GROUPS):
                keep = min(win, seq)
                sl = slice(g * SWA_GROUP_DIM, (g + 1) * SWA_GROUP_DIM)
                kv_t = jnp.stack([kt[:, sl, seq - keep:], vt[:, sl, seq - keep:]], axis=1)
                kv_t = kv_t.reshape(bp, 2, SWA_HEADS_PER_GROUP, HEAD_DIM, keep)
                swa_p[g].append(jnp.transpose(kv_t, (0, 4, 1, 2, 3)))
            q, k, v = _norm_proj(xs, g_mix, weights, gains, tm_s)
            q3, k3, v3 = (t.reshape(bs, t_new, SWA_DIM) for t in (q, k, v))
            as_, new_bufs = _swa_sample(q3, k3, v3, [c[i] for c in swa_caches], rel_bias, bs, t_new)
            for g in range(len(SWA_GROUPS)):
                swa_s[g].append(new_bufs[g])
        else:
            i = i_diff
            i_diff += 1
            lam_init = 0.8 - 0.6 * math.exp(-0.3 * layer)
            w_qkv = diff_w_qkv[i].astype(BF16)
            weights = [w_qkv[:, :DIFF_Q_DIM], w_qkv[:, DIFF_Q_DIM:DIFF_Q_DIM + DIFF_K_DIM],
                       w_qkv[:, DIFF_Q_DIM + DIFF_K_DIM:]]
            gains = [jnp.tile(diff_q_norm[i].astype(F32), DIFF_Q_DIM // HEAD_DIM).reshape(1, DIFF_Q_DIM),
                     jnp.tile(diff_k_norm[i].astype(F32), DIFF_K_DIM // HEAD_DIM).reshape(1, DIFF_K_DIM), None]
            wo = diff_w_out[i].astype(BF16)
            q, k, kt, v = _norm_proj(xp, g_mix, weights, gains, tm_p, emit=("n", "nt", "n"), seq=seq)
            ap = _diff_prompt(q, k, v, rel_bias, diff_lambda[i], diff_out_norm[i], lam_init, bp, seq)
            dk_p.append(jnp.transpose(kt.reshape(bp, DIFF_KV_HEADS, 2, HEAD_DIM, seq), (0, 4, 1, 2, 3)))
            dv_p.append(v.reshape(bp, seq, DIFF_KV_HEADS, DIFF_V_DIM))
            q, k, v = _norm_proj(xs, g_mix, weights, gains, tm_s)
            as_ = _diff_sample(q, k, v, cache_diff_k[i], cache_diff_v[i], page_table, rel_bias, diff_lambda[i],
                               diff_out_norm[i], lam_init, bs, t_new)
            dk_s.append(k.reshape(bs, t_new, DIFF_KV_HEADS, 2, HEAD_DIM))
            dv_s.append(v.reshape(bs, t_new, DIFF_KV_HEADS, DIFF_V_DIM))
        g_ffn = norm_ffn[layer].astype(F32)
        xp = _mix_ffn(xp, ap, wo, g_ffn, w1, w2, layer, 2 * tm_p)
        xs = _mix_ffn(xs, as_, wo, g_ffn, w1, w2, layer, tm_s)
    return (xp.reshape(bp, seq, d), xs.reshape(bs, t_new, d),
            jnp.stack(conv_p), jnp.stack(conv_s), ssm_p, ssm_s,
            jnp.stack(swa_p[0]), jnp.stack(swa_s[0]), jnp.stack(swa_p[1]), jnp.stack(swa_s[1]),
            jnp.stack(swa_p[2]), jnp.stack(swa_s[2]),
            jnp.stack(dk_p), jnp.stack(dk_s), jnp.stack(dv_p), jnp.stack(dv_s))
```

```python
import functools
import math

import jax
import jax.numpy as jnp
from jax import lax
from jax.experimental import pallas as pl
from jax.experimental.pallas import tpu as pltpu

F32 = jnp.float32
BF16 = jnp.bfloat16

D_MODEL = 1024
DEPTH = 4
D_FF = 2816
D_INNER = 2048
SSM_HEADS = 32
SSM_HEAD_DIM = 64
SSM_GROUPS = 4
D_STATE = 128
CONV_W = 4
CONV_DIM = D_INNER + 2 * SSM_GROUPS * D_STATE
SSD_CHUNK = 128
SHORT_CHUNK = 16
HEAD_DIM = 64
N_ATTN_HEADS = 12
NUM_BUCKETS = 32
MAX_DISTANCE = 2048
SWA_GROUPS = ((128, 1), (512, 4), (2048, 16))
SWA_HEADS_PER_GROUP = 4
SWA_GROUP_DIM = SWA_HEADS_PER_GROUP * HEAD_DIM
SWA_DIM = N_ATTN_HEADS * HEAD_DIM
SWA_STEPS = 128
DIFF_KV_HEADS = 4
DIFF_REP = 3
DIFF_V_DIM = 128
DIFF_Q_DIM = N_ATTN_HEADS * 2 * HEAD_DIM
DIFF_K_DIM = DIFF_KV_HEADS * 2 * HEAD_DIM
DIFF_V_ALL = DIFF_KV_HEADS * DIFF_V_DIM
PAGE_SIZE = 128
EPS = 1e-6
NEG = -1e30

LANES = 128
VMEM_LIMIT = 56 * 1024 * 1024
PAGES_PER_STEP = 32


def _dot(a, b):
    return jnp.dot(a, b, preferred_element_type=F32)


def _dot_nt(a, b):
    return lax.dot_general(a, b, (((1,), (1,)), ((), ())), preferred_element_type=F32)


def _split3(x):
    hi = x.astype(BF16)
    r = x - hi.astype(F32)
    mid = r.astype(BF16)
    lo = (r - mid.astype(F32)).astype(BF16)
    return hi, mid, lo


def _dot_exact_rhs(a3_bf16, x):
    return _dot(a3_bf16, jnp.concatenate(_split3(x), axis=0))


def _dot_exact_lhs(x, a3_bf16):
    return _dot(jnp.concatenate(_split3(x), axis=1), a3_bf16)


def _sigmoid(x):
    return 1.0 / (1.0 + jnp.exp(-x))


def _rms(x, g):
    return x * lax.rsqrt(jnp.mean(x * x, axis=-1, keepdims=True) + EPS) * g


def _resident(shape):
    n = len(shape)
    return pl.BlockSpec(shape, lambda *_: (0,) * n, pipeline_mode=pl.Buffered(1))


def _params(sem):
    return pltpu.CompilerParams(dimension_semantics=sem, vmem_limit_bytes=VMEM_LIMIT)


def _norm_proj_kernel(*refs, n_out, head_norm, emit):
    x_ref, g_ref = refs[0], refs[1]
    w_refs = refs[2:2 + n_out]
    n_hn = sum(head_norm)
    hn_refs = refs[2 + n_out:2 + n_out + n_hn]
    pos = 2 + n_out + n_hn
    seg_ref = refs[pos] if n_hn else None
    pos += 1 if n_hn else 0
    o_refs = list(refs[pos:])
    h = _rms(x_ref[...], g_ref[...]).astype(BF16)
    k = 0
    for i in range(n_out):
        y = _dot(h, w_refs[i][...])
        if head_norm[i]:
            gain = hn_refs[k][...]
            k += 1
            seg = seg_ref[...]
            parts = []
            for c in range(y.shape[1] // seg.shape[0]):
                yc = y[:, c * seg.shape[0]:(c + 1) * seg.shape[0]]
                sq = yc * yc
                hi = sq.astype(BF16)
                lo = (sq - hi.astype(F32)).astype(BF16)
                ms = (_dot(hi, seg) + _dot(lo, seg)) * (1.0 / HEAD_DIM)
                parts.append(yc * lax.rsqrt(ms + EPS))
            y = jnp.concatenate(parts, axis=1) * gain
        if "n" in emit[i]:
            o_refs.pop(0)[...] = y
        if "t" in emit[i]:
            o_refs.pop(0)[0] = y.T
        if "b" in emit[i]:
            ref = o_refs.pop(0)
            for cb in range(y.shape[1] // LANES):
                ref[0, cb] = y[:, cb * LANES:(cb + 1) * LANES]
        if "s" in emit[i]:
            ref = o_refs.pop(0)
            ncb = y.shape[1] // LANES
            for cb in range(ncb):
                ref[pl.ds(cb, y.shape[0], stride=ncb), :] = y[:, cb * LANES:(cb + 1) * LANES]


def _norm_proj(x, g, weights, head_gains, tm, emit=None, seq=None):
    m, d = x.shape
    n_out = len(weights)
    emit = tuple(emit or ("n",) * n_out)
    head_norm = tuple(hg is not None for hg in head_gains)
    ins = [x, g.reshape(1, d)] + list(weights)
    specs = [pl.BlockSpec((tm, d), lambda i: (i, 0)), _resident((1, d))]
    specs += [_resident(w.shape) for w in weights]
    for hg in head_gains:
        if hg is not None:
            ins.append(hg)
            specs.append(_resident(hg.shape))
    if any(head_norm):
        r = jnp.arange(2 * LANES)
        seg = (r[:, None] // HEAD_DIM == r[None, :] // HEAD_DIM).astype(BF16)
        ins.append(seg)
        specs.append(_resident(seg.shape))
    per_seq = seq // tm if seq else 1
    out_specs, out_shape = [], []
    for w, e in zip(weights, emit):
        n = w.shape[1]
        if "n" in e:
            out_specs.append(pl.BlockSpec((tm, n), lambda i: (i, 0)))
            out_shape.append(jax.ShapeDtypeStruct((m, n), F32))
        if "t" in e:
            out_specs.append(pl.BlockSpec((1, n, tm), lambda i: (i // per_seq, 0, i % per_seq)))
            out_shape.append(jax.ShapeDtypeStruct((m // seq, n, seq), F32))
        if "b" in e:
            out_specs.append(pl.BlockSpec((1, n // LANES, tm, LANES), lambda i: (i // per_seq, 0, i % per_seq, 0)))
            out_shape.append(jax.ShapeDtypeStruct((m // seq, n // LANES, seq, LANES), F32))
        if "s" in e:
            out_specs.append(pl.BlockSpec((tm * (n // LANES), LANES), lambda i: (i, 0)))
            out_shape.append(jax.ShapeDtypeStruct((m * (n // LANES), LANES), F32))
    return pl.pallas_call(
        functools.partial(_norm_proj_kernel, n_out=n_out, head_norm=head_norm, emit=emit),
        grid=(m // tm,),
        in_specs=specs,
        out_specs=out_specs,
        out_shape=out_shape,
        compiler_params=_params(("parallel",)),
        name="norm_proj",
    )(*ins)


def _mix_ffn_kernel(x_ref, a_ref, wo_ref, g_ref, w1_ref, w2_ref, o_ref):
    x1 = x_ref[...] + _dot(a_ref[...], wo_ref[...])
    h = _rms(x1, g_ref[...]).astype(BF16)
    gu = _dot(h, w1_ref[0])
    gate = gu[:, :D_FF]
    act = (gate * _sigmoid(gate) * gu[:, D_FF:]).astype(BF16)
    o_ref[...] = x1 + _dot(act, w2_ref[0])


def _mix_ffn(x, a, wo, g, w1, w2, layer, tm):
    m, d = x.shape
    ka = a.shape[1]
    slab = lambda w: pl.BlockSpec((1,) + w.shape[1:], lambda i: (layer, 0, 0), pipeline_mode=pl.Buffered(1))
    return pl.pallas_call(
        _mix_ffn_kernel,
        grid=(m // tm,),
        in_specs=[pl.BlockSpec((tm, d), lambda i: (i, 0)),
                  pl.BlockSpec((tm, ka), lambda i: (i, 0)),
                  _resident(wo.shape), _resident((1, d)), slab(w1), slab(w2)],
        out_specs=pl.BlockSpec((tm, d), lambda i: (i, 0)),
        out_shape=jax.ShapeDtypeStruct((m, d), F32),
        compiler_params=_params(("parallel",)),
        name="mix_ffn",
    )(x, a, wo, g.reshape(1, d), w1, w2)


def _ssd_kernel(z_ref, xbc_ref, dt_ref, cst_ref, sst_ref, cw_ref, cb_ref, dtb_ref, alog_ref,
                dx_ref, ng_ref, e_ref, *rest, q, n_valid, n_prev):
    prev_ref = rest[0] if n_prev else None
    y_ref, ns_ref, ext_ref, st_ref, yacc_ref, *pad_refs = rest[1 if n_prev else 0:]
    c = pl.program_id(1)
    last = pl.num_programs(1) - 1
    padded = n_valid < q

    @pl.when(c == 0)
    def _():
        ext_ref[...] = jnp.zeros(ext_ref.shape, F32)
        ext_ref[5:8, :] = cst_ref[0]
        st_ref[...] = sst_ref[0, 0]
        if padded:
            for r in pad_refs:
                r[...] = jnp.zeros(r.shape, F32)

    ext_ref[8:8 + n_valid, :] = xbc_ref[0]
    ext = ext_ref[...]
    ext1 = pltpu.roll(ext, 1, axis=0)
    older = pltpu.roll(ext * cw_ref[1:2, :] + ext1 * cw_ref[0:1, :], 2, axis=0)
    conv = (cb_ref[...] + ext[8:8 + q] * cw_ref[3:4, :] + ext1[8:8 + q] * cw_ref[2:3, :]) + older[8:8 + q]
    if not padded:
        ext_ref[5:8, :] = ext_ref[q + 5:q + 8, :]
    act = conv * _sigmoid(conv)
    xs = act[:, :D_INNER]
    bm = act[:, D_INNER:D_INNER + SSM_GROUPS * D_STATE].astype(BF16)
    cm = act[:, D_INNER + SSM_GROUPS * D_STATE:].astype(BF16)

    if padded:
        zpad_ref, dtpad_ref = pad_refs
        zpad_ref[0:n_valid, :] = z_ref[0]
        dtpad_ref[0:n_valid, :] = dt_ref[0]
        z = zpad_ref[...]
        dt_raw = dtpad_ref[...]
    else:
        z = z_ref[0]
        dt_raw = dt_ref[0]

    row = lax.broadcasted_iota(jnp.int32, (q, q), 0)
    col = lax.broadcasted_iota(jnp.int32, (q, q), 1)
    causal = col <= row
    tri = jnp.where(causal, 1.0, 0.0).astype(BF16)
    tri_t = jnp.where(row <= col, 1.0, 0.0).astype(BF16)
    tri3 = jnp.concatenate([tri] * 3, axis=1)
    tri_t3 = jnp.concatenate([tri_t] * 3, axis=0)

    xdt_in = dt_raw + dtb_ref[...]
    dt = jnp.maximum(xdt_in, 0.0) + jnp.log(1.0 + jnp.exp(-jnp.abs(xdt_in)))
    if padded:
        dt = jnp.where(lax.broadcasted_iota(jnp.int32, dt.shape, 0) < n_valid, dt, 0.0)
    adt = dt * (-jnp.exp(alog_ref[...]))
    acum = _dot_exact_rhs(tri3, adt)
    acum_t = _dot_exact_lhs(adt.T, tri_t3)
    both_x = _dot_exact_lhs(jnp.concatenate([dt, acum], axis=0), e_ref[...])
    dt_x, acum_x = both_x[0:q], both_x[q:2 * q]
    exp_acum_x = jnp.exp(acum_x)
    acum_last_x = acum_x[q - 1:q, :]
    xdt = xs * dt_x
    xdt_b = xdt.astype(BF16)
    xd = xdt * jnp.exp(acum_last_x - acum_x)
    dtot_x = jnp.exp(acum_last_x)
    lane_lo = lax.broadcasted_iota(jnp.int32, (q, LANES), 1) < SSM_HEAD_DIM

    gw = SSM_HEADS // SSM_GROUPS * SSM_HEAD_DIM
    for g in range(SSM_GROUPS):
        cg = cm[:, g * D_STATE:(g + 1) * D_STATE]
        bg = bm[:, g * D_STATE:(g + 1) * D_STATE]
        cb = _dot_nt(cg, bg)
        st_g = st_ref[g * gw:(g + 1) * gw, :]
        yoff = _dot_nt(cg, st_g.astype(BF16))
        yacc_ref[:, g * gw:(g + 1) * gw] = yoff * exp_acum_x[:, g * gw:(g + 1) * gw]
        for pair in range(gw // LANES):
            h0 = g * (gw // SSM_HEAD_DIM) + 2 * pair
            xp = xdt_b[:, h0 * SSM_HEAD_DIM:h0 * SSM_HEAD_DIM + LANES]
            ys = []
            for h in (h0, h0 + 1):
                seg = jnp.broadcast_to(acum[:, h:h + 1], (q, q)) - jnp.broadcast_to(acum_t[h:h + 1, :], (q, q))
                w = cb * jnp.exp(jnp.where(causal, seg, NEG))
                ys.append(_dot(w.astype(BF16), xp))
            yacc_ref[:, h0 * SSM_HEAD_DIM:h0 * SSM_HEAD_DIM + LANES] += jnp.where(lane_lo, ys[0], ys[1])
        xd_t = jnp.concatenate(
            [xd[:, g * gw + t * LANES:g * gw + (t + 1) * LANES].T for t in range(gw // LANES)], axis=0)
        upd = _dot(xd_t.astype(BF16), bg)
        dtot = jnp.concatenate(
            [jnp.broadcast_to(dtot_x[:, g * gw + t * LANES:g * gw + (t + 1) * LANES], (LANES, LANES)).T
             for t in range(gw // LANES)], axis=0)
        st_ref[g * gw:(g + 1) * gw, :] = dtot * st_g + upd

    y = yacc_ref[...] + xs * dx_ref[...]
    gated = y * (z * _sigmoid(z))
    outs = []
    for g in range(SSM_GROUPS):
        gg = gated[:, g * gw:(g + 1) * gw]
        outs.append(gg * lax.rsqrt(jnp.mean(gg * gg, axis=-1, keepdims=True) + EPS))
    yn = jnp.concatenate(outs, axis=1) * ng_ref[...]
    y_ref[0] = yn[0:n_valid].astype(BF16)

    @pl.when(c == last)
    def _():
        for k in range(n_prev):
            ns_ref[k, 0] = prev_ref[k, 0]
        ns_ref[n_prev, 0] = st_ref[...]


def _ssd_mixer(z, xbc, dt, conv_state, ssm_states, state_idx, conv_w, conv_b, dt_bias, a_log, d_skip, norm_g,
               prev_new=None):
    bt, seq, _ = z.shape
    q = SSD_CHUNK if seq >= SSD_CHUNK else SHORT_CHUNK
    n_valid = min(q, seq)
    nc = max(1, seq // q)
    pad = lambda v: jnp.pad(v.astype(F32), (0, LANES - SSM_HEADS)).reshape(1, LANES)
    hh = jnp.arange(LANES)[:, None]
    cc = jnp.arange(D_INNER)[None, :] // SSM_HEAD_DIM
    expand = jnp.tile((hh == cc).astype(BF16), (3, 1))
    dx = jnp.repeat(d_skip.astype(F32), SSM_HEAD_DIM).reshape(1, D_INNER)
    rows = SSM_HEADS * SSM_HEAD_DIM
    sst = ssm_states.reshape(ssm_states.shape[0], bt, rows, D_STATE)
    n_prev = 0 if prev_new is None else prev_new.shape[0]
    prev_in = [] if prev_new is None else [prev_new.reshape(n_prev, bt, rows, D_STATE)]
    prev_spec = [pl.BlockSpec((n_prev, 1, rows, D_STATE), lambda b, c: (0, b, 0, 0))] * len(prev_in)
    scratch = [pltpu.VMEM((q + 8, CONV_DIM), F32),
               pltpu.VMEM((SSM_HEADS * SSM_HEAD_DIM, D_STATE), F32),
               pltpu.VMEM((q, D_INNER), F32)]
    if n_valid < q:
        scratch += [pltpu.VMEM((q, D_INNER), F32), pltpu.VMEM((q, LANES), F32)]
    y, ns = pl.pallas_call(
        functools.partial(_ssd_kernel, q=q, n_valid=n_valid, n_prev=n_prev),
        grid=(bt, nc),
        in_specs=[pl.BlockSpec((1, n_valid, D_INNER), lambda b, c: (b, c, 0)),
                  pl.BlockSpec((1, n_valid, CONV_DIM), lambda b, c: (b, c, 0)),
                  pl.BlockSpec((1, n_valid, LANES), lambda b, c: (b, c, 0)),
                  pl.BlockSpec((1, CONV_W - 1, CONV_DIM), lambda b, c: (b, 0, 0)),
                  pl.BlockSpec((1, 1, SSM_HEADS * SSM_HEAD_DIM, D_STATE), lambda b, c: (state_idx, b, 0, 0)),
                  _resident((CONV_W, CONV_DIM)), _resident((1, CONV_DIM)),
                  _resident((1, LANES)), _resident((1, LANES)),
                  _resident((1, D_INNER)), _resident((1, D_INNER)), _resident((3 * LANES, D_INNER))] + prev_spec,
        out_specs=[pl.BlockSpec((1, n_valid, D_INNER), lambda b, c: (b, c, 0)),
                   pl.BlockSpec((n_prev + 1, 1, rows, D_STATE), lambda b, c: (0, b, 0, 0))],
        out_shape=[jax.ShapeDtypeStruct((bt, seq, D_INNER), BF16),
                   jax.ShapeDtypeStruct((n_prev + 1, bt, rows, D_STATE), F32)],
        scratch_shapes=scratch,
        compiler_params=_params(("parallel", "arbitrary")),
        name="ssd_mixer",
    )(z, xbc, dt, conv_state, sst, conv_w, conv_b.reshape(1, CONV_DIM), pad(dt_bias), pad(a_log),
      dx, norm_g.reshape(1, D_INNER), expand, *prev_in)
    return y, ns.reshape(n_prev + 1, bt, SSM_HEADS, SSM_HEAD_DIM, D_STATE)


def _t5_bucket(dist):
    max_exact = NUM_BUCKETS // 2
    d = jnp.maximum(dist, 0)
    df = jnp.maximum(d, 1).astype(F32)
    large = max_exact + (jnp.log(df / max_exact) / math.log(MAX_DISTANCE / max_exact)
                         * (NUM_BUCKETS - max_exact)).astype(jnp.int32)
    return jnp.where(d < max_exact, d, jnp.minimum(large, NUM_BUCKETS - 1))


def _bias_lookup(tab, dist):
    onehot = jax.nn.one_hot(_t5_bucket(dist), NUM_BUCKETS, dtype=F32)
    return jnp.einsum('...b,bh->...h', onehot, tab.astype(F32), precision=lax.Precision.HIGHEST)


def _swa_prompt_bias(rel_bias, g, dil):
    i = jnp.arange(SWA_STEPS)[:, None]
    j = jnp.arange(2 * SWA_STEPS)[None, :]
    step = SWA_STEPS + i - j
    tab = rel_bias[:, g * SWA_HEADS_PER_GROUP:(g + 1) * SWA_HEADS_PER_GROUP].astype(F32)
    bias = jnp.transpose(_bias_lookup(tab, jnp.clip(step, 0, SWA_STEPS) * dil), (2, 0, 1))
    valid = (step >= 0) & (step <= SWA_STEPS)
    first = valid & (j >= SWA_STEPS)
    both = jnp.stack([jnp.where(first, bias, NEG), jnp.where(valid, bias, NEG)])
    return both.reshape(2, SWA_HEADS_PER_GROUP * SWA_STEPS, 2 * SWA_STEPS)


def _swa_fused_kernel(q_ref, k_ref, v_ref, b_ref, a_ref, qs_ref, ks_ref, vs_ref, oc_ref, lc_ref, ot_ref, lt_ref,
                      *, seq):
    g = pl.program_id(1)
    gd = SWA_GROUP_DIM
    blk = SWA_STEPS
    n_units = seq // blk
    lane_head = lax.broadcasted_iota(jnp.int32, (blk, gd), 1) // HEAD_DIM

    def group_body(gi, dil):
        rows = seq // dil
        nb = rows // blk
        ks_ref[0:blk, :] = jnp.zeros((blk, gd), BF16)
        vs_ref[0:blk, :] = jnp.zeros((blk, gd), BF16)
        for c in range(dil):
            src = pl.ds(c, rows, stride=dil) if dil > 1 else pl.ds(0, rows)
            dst = slice(blk + c * rows, blk + (c + 1) * rows)
            for half in range(gd // LANES):
                lanes = slice(half * LANES, (half + 1) * LANES)
                qs_ref[dst, lanes] = (q_ref[0, half, src, :] * (HEAD_DIM ** -0.5)).astype(BF16)
                ks_ref[dst, lanes] = k_ref[0, half, src, :].astype(BF16)
                vs_ref[dst, lanes] = v_ref[0, half, src, :].astype(BF16)

        def unit(u, carry):
            start = pl.multiple_of(u * blk, blk)
            first = lax.rem(u, nb) == 0
            qv = qs_ref[pl.ds(blk + start, blk), :]
            kk = ks_ref[pl.ds(start, 2 * blk), :]
            vv = vs_ref[pl.ds(start, 2 * blk), :]
            sel = jnp.where(first, 0, 1)
            nh = SWA_HEADS_PER_GROUP
            qh = jnp.concatenate([jnp.where(lane_head == h, qv, jnp.zeros_like(qv)) for h in range(nh)], axis=0)
            s = _dot_nt(qh, kk) + b_ref[gi, sel]
            m = jnp.max(s, axis=-1, keepdims=True)
            p = jnp.exp(s - m)
            l = jnp.sum(p, axis=-1, keepdims=True)
            of = _dot(p.astype(BF16), vv) / l
            lf = jnp.broadcast_to(m + jnp.log(l), (nh * blk, gd))
            o, lse = of[0:blk], lf[0:blk]
            for h in range(1, nh):
                o = jnp.where(lane_head == h, of[h * blk:(h + 1) * blk], o)
                lse = jnp.where(lane_head == h, lf[h * blk:(h + 1) * blk], lse)
            oc_ref[pl.ds(start, blk), :] = o
            lc_ref[pl.ds(start, blk), :] = lse
            return carry

        lax.fori_loop(0, n_units, unit, 0, unroll=2)
        for c in range(dil):
            dst = pl.ds(c, rows, stride=dil) if dil > 1 else pl.ds(0, rows)
            for half in range(gd // LANES):
                lanes = slice(half * LANES, (half + 1) * LANES)
                cb = gi * (gd // LANES) + half
                ot_ref[cb, dst, :] = oc_ref[c * rows:(c + 1) * rows, lanes]
                lt_ref[cb, dst, :] = lc_ref[c * rows:(c + 1) * rows, lanes]

    for gi, (_, dil) in enumerate(SWA_GROUPS):
        pl.when(g == gi)(functools.partial(group_body, gi, dil))

    @pl.when(g == len(SWA_GROUPS) - 1)
    def _():
        def tile(t, carry):
            r = pl.ds(pl.multiple_of(t * blk, blk), blk)
            per = gd // LANES
            for half in range(per):
                ls = [lt_ref[gi * per + half, r, :] for gi in range(len(SWA_GROUPS))]
                m = jnp.maximum(jnp.maximum(ls[0], ls[1]), ls[2])
                es = [jnp.exp(l - m) for l in ls]
                inv = 1.0 / (es[0] + es[1] + es[2])
                for gi in range(len(SWA_GROUPS)):
                    cb = gi * per + half
                    a_ref[0, r, cb * LANES:(cb + 1) * LANES] = (ot_ref[cb, r, :] * es[gi] * inv).astype(BF16)
            return carry

        lax.fori_loop(0, n_units, tile, 0)


def _swa_prompt(q, k, v, rel_bias, bt, seq):
    bias = jnp.stack([_swa_prompt_bias(rel_bias, g, dil) for g, (_, dil) in enumerate(SWA_GROUPS)])
    gd = SWA_GROUP_DIM
    tok = pl.BlockSpec((1, gd // LANES, seq, LANES), lambda b, g: (b, g, 0, 0))
    a = pl.pallas_call(
        functools.partial(_swa_fused_kernel, seq=seq),
        grid=(bt, len(SWA_GROUPS)),
        in_specs=[tok, tok, tok, _resident(bias.shape)],
        out_specs=pl.BlockSpec((1, seq, SWA_DIM), lambda b, g: (b, 0, 0)),
        out_shape=jax.ShapeDtypeStruct((bt, seq, SWA_DIM), BF16),
        scratch_shapes=[pltpu.VMEM((seq + SWA_STEPS, gd), BF16)] * 3
                       + [pltpu.VMEM((seq, gd), F32)] * 2 + [pltpu.VMEM((SWA_DIM // LANES, seq, LANES), F32)] * 2,
        compiler_params=_params(("parallel", "arbitrary")),
        name="swa_prompt",
    )(q, k, v, bias)
    return a.reshape(bt * seq, SWA_DIM)


def _swa_sample_kernel(q_ref, kn_ref, c0_ref, c1_ref, c2_ref, b0_ref, b1_ref, b2_ref, nb_ref,
                       a_ref, o0_ref, o1_ref, o2_ref, *, t_new):
    gd = SWA_GROUP_DIM
    nh = SWA_HEADS_PER_GROUP
    lane_head = lax.broadcasted_iota(jnp.int32, (t_new, gd), 1) // HEAD_DIM
    caches = (c0_ref, c1_ref, c2_ref)
    biases = (b0_ref, b1_ref, b2_ref)
    new_caches = (o0_ref, o1_ref, o2_ref)
    outs, lses = [], []
    for g in range(len(SWA_GROUPS)):
        c_ref = caches[g]
        win = c_ref.shape[2]
        qg = q_ref[0, :, g * gd:(g + 1) * gd] * (HEAD_DIM ** -0.5)
        qs = jnp.concatenate([jnp.where(lane_head == h, qg, 0.0) for h in range(nh)], axis=0).astype(BF16)
        new = kn_ref[0, g]
        s = _dot(qs, c_ref[0, 0:gd, :].astype(BF16)) + biases[g][...]
        sn = _dot(qs, new[0:gd].astype(BF16)) + nb_ref[g]
        m = jnp.maximum(jnp.max(s, axis=-1, keepdims=True), jnp.max(sn, axis=-1, keepdims=True))
        p = jnp.exp(s - m)
        pn = jnp.exp(sn - m)
        l = jnp.sum(p, axis=-1, keepdims=True) + jnp.sum(pn, axis=-1, keepdims=True)
        of = (_dot_nt(p.astype(BF16), c_ref[0, gd:2 * gd, :].astype(BF16))
              + _dot_nt(pn.astype(BF16), new[gd:2 * gd].astype(BF16))) / l
        lf = jnp.broadcast_to(m + jnp.log(l), (nh * t_new, gd))
        o = jnp.zeros((t_new, gd), F32)
        lse = jnp.zeros((t_new, gd), F32)
        for h in range(nh):
            o = jnp.where(lane_head == h, of[h * t_new:(h + 1) * t_new], o)
            lse = jnp.where(lane_head == h, lf[h * t_new:(h + 1) * t_new], lse)
        outs.append(o)
        lses.append(lse)
        rolled = pltpu.roll(c_ref[0], win - t_new, axis=1)
        if win > LANES:
            new_caches[g][0, :, 0:win - LANES] = rolled[:, 0:win - LANES]
        keep = lax.broadcasted_iota(jnp.int32, (2 * gd, LANES), 1) < LANES - t_new
        new_caches[g][0, :, win - LANES:win] = jnp.where(keep, rolled[:, win - LANES:win], new)
    m = jnp.maximum(jnp.maximum(lses[0], lses[1]), lses[2])
    es = [jnp.exp(l - m) for l in lses]
    inv = 1.0 / (es[0] + es[1] + es[2])
    for g in range(len(SWA_GROUPS)):
        a_ref[0, :, g * gd:(g + 1) * gd] = (outs[g] * es[g] * inv).astype(BF16)


def _swa_sample_bias(rel_bias, t_new):
    t = jnp.arange(t_new)
    biases, new_biases = [], []
    for g, (win, dil) in enumerate(SWA_GROUPS):
        tab = rel_bias[:, g * SWA_HEADS_PER_GROUP:(g + 1) * SWA_HEADS_PER_GROUP].astype(F32)
        back = t[:, None] - jnp.arange(win)[None, :]
        ok = (back <= 0) & (back % dil == 0)
        b = jnp.where(ok[..., None], _bias_lookup(tab, win + back), NEG)
        biases.append(jnp.transpose(b, (2, 0, 1)).reshape(SWA_HEADS_PER_GROUP * t_new, win))
        j = jnp.arange(LANES)[None, :] - (LANES - t_new)
        nd = t[:, None] - j
        nok = (j >= 0) & (nd >= 0) & (nd % dil == 0)
        nb = jnp.where(nok[..., None], _bias_lookup(tab, nd), NEG)
        new_biases.append(jnp.transpose(nb, (2, 0, 1)).reshape(SWA_HEADS_PER_GROUP * t_new, LANES))
    return biases, jnp.stack(new_biases)


def _swa_sample(q, k, v, caches, rel_bias, bt, t_new):
    biases, new_bias = _swa_sample_bias(rel_bias, t_new)
    gd = SWA_GROUP_DIM
    tiles = []
    for g in range(len(SWA_GROUPS)):
        kv = jnp.concatenate([k[:, :, g * gd:(g + 1) * gd], v[:, :, g * gd:(g + 1) * gd]], axis=2)
        tiles.append(jnp.pad(jnp.transpose(kv, (0, 2, 1)), ((0, 0), (0, 0), (LANES - t_new, 0))))
    kn = jnp.stack(tiles, axis=1)
    views = [jnp.transpose(c, (0, 2, 3, 4, 1)).reshape(bt, 2 * gd, c.shape[1]) for c in caches]
    cspecs = [pl.BlockSpec((1, 2 * gd, c.shape[2]), lambda b: (b, 0, 0)) for c in views]
    tok = pl.BlockSpec((1, t_new, SWA_DIM), lambda b: (b, 0, 0))
    res = pl.pallas_call(
        functools.partial(_swa_sample_kernel, t_new=t_new),
        grid=(bt,),
        in_specs=[tok, pl.BlockSpec((1, len(SWA_GROUPS), 2 * gd, LANES), lambda b: (b, 0, 0, 0))] + cspecs
                 + [_resident(b.shape) for b in biases] + [_resident(new_bias.shape)],
        out_specs=[tok] + cspecs,
        out_shape=[jax.ShapeDtypeStruct((bt, t_new, SWA_DIM), BF16)]
                  + [jax.ShapeDtypeStruct(c.shape, F32) for c in views],
        compiler_params=_params(("parallel",)),
        name="swa_sample",
    )(q, kn, *views, *biases, new_bias)
    new_caches = [jnp.transpose(c.reshape(bt, 2, SWA_HEADS_PER_GROUP, HEAD_DIM, c.shape[2]), (0, 4, 1, 2, 3))
                  for c in res[1:]]
    return res[0].reshape(bt * t_new, SWA_DIM), new_caches


def _diff_lambda(lam_ref, lam_init):
    lp = lam_ref[...]
    s1 = jnp.sum(lp[0:1] * lp[1:2], axis=-1, keepdims=True)
    s2 = jnp.sum(lp[2:3] * lp[3:4], axis=-1, keepdims=True)
    return jnp.exp(s1) - jnp.exp(s2) + lam_init


def _diff_prompt_kernel(q_ref, k_ref, v_ref, b_ref, lam_ref, on_ref, o_ref, q6_ref, m_ref, acc_ref,
                        *, tq, tk, lam_init):
    i = pl.program_id(2)
    nsub = tk // LANES
    qsub = tq // LANES
    lane_lo = lax.broadcasted_iota(jnp.int32, (tq, LANES), 1) < HEAD_DIM
    for r in range(DIFF_REP):
        qr = q_ref[0, :, r * LANES:(r + 1) * LANES] * (HEAD_DIM ** -0.5)
        q6_ref[(2 * r) * tq:(2 * r + 1) * tq, :] = jnp.where(lane_lo, qr, 0.0).astype(BF16)
        q6_ref[(2 * r + 1) * tq:(2 * r + 2) * tq, :] = jnp.where(lane_lo, 0.0, qr).astype(BF16)
    m_ref[...] = jnp.full(m_ref.shape, NEG, F32)
    acc_ref[...] = jnp.zeros(acc_ref.shape, F32)
    ones = jnp.ones((tk, LANES), BF16)

    def body(j, carry):
        start = pl.multiple_of(j * tk, tk)
        kb = k_ref[0, pl.ds(start, tk), :].astype(BF16)
        vx = jnp.concatenate([v_ref[0, pl.ds(start, tk), :].astype(BF16), ones], axis=1)
        d0 = i * qsub - j * nsub + (nsub - 1)
        for rm in range(2 * DIFF_REP):
            rows = slice(rm * tq, (rm + 1) * tq)
            bias = jnp.concatenate(
                [jnp.concatenate([b_ref[0, d0 - c + qs, rm // 2] for qs in range(qsub)], axis=0)
                 for c in range(nsub)], axis=1)
            s = _dot_nt(q6_ref[rows, :], kb) + bias
            smax = s[:, 0:LANES]
            for c in range(1, tk // LANES):
                smax = jnp.maximum(smax, s[:, c * LANES:(c + 1) * LANES])
            m_old = m_ref[rows, :]
            m_new = jnp.maximum(m_old, jnp.max(smax, axis=-1, keepdims=True))
            alpha = jnp.exp(m_old - m_new)
            p = jnp.concatenate([jnp.exp(s[:, c * LANES:(c + 1) * LANES] - m_new)
                                 for c in range(tk // LANES)], axis=1).astype(BF16)
            acc_ref[rows, :] = jnp.concatenate([alpha, alpha], axis=1) * acc_ref[rows, :] + _dot(p, vx)
            m_ref[rows, :] = m_new
        return carry

    lax.fori_loop(0, ((i + 1) * tq - 1) // tk + 1, body, 0)
    lam = _diff_lambda(lam_ref, lam_init)
    for r in range(DIFF_REP):
        o0 = acc_ref[(2 * r) * tq:(2 * r + 1) * tq, :]
        o1 = acc_ref[(2 * r + 1) * tq:(2 * r + 2) * tq, :]
        a = o0[:, :DIFF_V_DIM] / o0[:, DIFF_V_DIM:] - lam * (o1[:, :DIFF_V_DIM] / o1[:, DIFF_V_DIM:])
        o_ref[0, :, r * LANES:(r + 1) * LANES] = (_rms(a, on_ref[...]) * (1.0 - lam_init)).astype(BF16)


def _diff_prompt_bias(rel_bias, seq, tk):
    nd = seq // LANES + tk // LANES - 1
    delta = jnp.arange(nd) - (tk // LANES - 1)
    d = delta[:, None, None] * LANES + jnp.arange(LANES)[None, :, None] - jnp.arange(LANES)[None, None, :]
    tab = jnp.where((d >= 0)[..., None], _bias_lookup(rel_bias, d), NEG)
    tab = jnp.transpose(tab, (3, 0, 1, 2)).reshape(DIFF_KV_HEADS, DIFF_REP, nd, LANES, LANES)
    return jnp.transpose(tab, (0, 2, 1, 3, 4))


def _diff_prompt(q, k, v, rel_bias, lam_p, out_norm, lam_init, bt, seq):
    tq, tk = 512, 512
    nb = seq // tq
    bias = _diff_prompt_bias(rel_bias, seq, tk)
    qw = DIFF_REP * 2 * HEAD_DIM
    a = pl.pallas_call(
        functools.partial(_diff_prompt_kernel, tq=tq, tk=tk, lam_init=lam_init),
        grid=(bt, DIFF_KV_HEADS, nb),
        in_specs=[pl.BlockSpec((1, tq, qw), lambda b, g, i: (b, i, g)),
                  pl.BlockSpec((1, seq, LANES), lambda b, g, i: (b, 0, g)),
                  pl.BlockSpec((1, seq, LANES), lambda b, g, i: (b, 0, g)),
                  pl.BlockSpec((1,) + bias.shape[1:], lambda b, g, i: (g, 0, 0, 0, 0)),
                  _resident((4, HEAD_DIM)), _resident((1, DIFF_V_DIM))],
        out_specs=pl.BlockSpec((1, tq, qw), lambda b, g, i: (b, i, g)),
        out_shape=jax.ShapeDtypeStruct((bt, seq, DIFF_Q_DIM), BF16),
        scratch_shapes=[pltpu.VMEM((2 * DIFF_REP * tq, LANES), BF16),
                        pltpu.VMEM((2 * DIFF_REP * tq, LANES), F32),
                        pltpu.VMEM((2 * DIFF_REP * tq, 2 * DIFF_V_DIM), F32)],
        compiler_params=_params(("parallel", "parallel", "arbitrary")),
        name="diff_prompt",
    )(q.reshape(bt, seq, DIFF_Q_DIM), k.reshape(bt, seq, DIFF_K_DIM), v.reshape(bt, seq, DIFF_V_ALL),
      bias, lam_p.astype(F32), out_norm.reshape(1, DIFF_V_DIM).astype(F32))
    return a.reshape(bt * seq, DIFF_Q_DIM)


def _diff_sample_kernel(pt_ref, qb_ref, kn_ref, vn_ref, *refs, t_new, lam_init):
    npg = PAGES_PER_STEP
    k_refs = refs[:npg]
    v_refs = refs[npg:2 * npg]
    b_ref, nb_ref, lam_ref, on_ref, o_ref, m_ref, l_ref, acc_ref, qs_ref = refs[2 * npg:]
    s_idx = pl.program_id(1)
    per_map = DIFF_REP * t_new
    per_g = 2 * per_map

    @pl.when(s_idx == 0)
    def _():
        m_ref[...] = jnp.full(m_ref.shape, NEG, F32)
        l_ref[...] = jnp.zeros(l_ref.shape, F32)
        acc_ref[...] = jnp.zeros(acc_ref.shape, F32)
        qr = qb_ref[0] * (HEAD_DIM ** -0.5)
        own = lax.broadcasted_iota(jnp.int32, qr.shape, 0) // per_map
        for c in range(2 * DIFF_KV_HEADS):
            qs_ref[:, c * HEAD_DIM:(c + 1) * HEAD_DIM] = jnp.where(own == c, qr, 0.0)

    qb = qs_ref[...].astype(BF16)

    def absorb(kt_refs, vv_refs, bias):
        kcat = jnp.concatenate([kt[0].astype(BF16) for kt in kt_refs], axis=1)
        s = _dot(qb, kcat) + bias
        chunks = [s[:, c * LANES:(c + 1) * LANES] for c in range(len(kt_refs))]
        smax = chunks[0]
        for c in chunks[1:]:
            smax = jnp.maximum(smax, c)
        m_old = m_ref[...]
        m_new = jnp.maximum(m_old, jnp.max(smax, axis=-1, keepdims=True))
        alpha = jnp.exp(m_old - m_new)
        ps = [jnp.exp(c - m_new) for c in chunks]
        psum = ps[0]
        for p in ps[1:]:
            psum = psum + p
        l_ref[...] = alpha * l_ref[...] + jnp.sum(psum, axis=-1, keepdims=True)
        pcat = jnp.concatenate(ps, axis=1).astype(BF16)
        for g in range(DIFF_KV_HEADS):
            rows = slice(g * per_g, (g + 1) * per_g)
            vcat = jnp.concatenate([vv[0, pl.ds(g, PAGE_SIZE, stride=DIFF_KV_HEADS), :].astype(BF16)
                                    for vv in vv_refs], axis=0)
            acc_ref[rows, :] = alpha[rows] * acc_ref[rows, :] + _dot(pcat[rows], vcat)
        m_ref[...] = m_new

    absorb(k_refs, v_refs, b_ref[0])

    @pl.when(s_idx == pl.num_programs(1) - 1)
    def _():
        absorb([kn_ref], [vn_ref], nb_ref[...])
        lam = _diff_lambda(lam_ref, lam_init)
        o = acc_ref[...] / l_ref[...]
        for g in range(DIFF_KV_HEADS):
            for r in range(DIFF_REP):
                r0 = g * per_g + r * t_new
                blk = o[r0:r0 + t_new] - lam * o[r0 + per_map:r0 + per_map + t_new]
                col = (g * DIFF_REP + r) * DIFF_V_DIM
                o_ref[0, :, col:col + DIFF_V_DIM] = (_rms(blk, on_ref[...]) * (1.0 - lam_init)).astype(BF16)


def _diff_sample(q, k, v, cache_k, cache_v, page_table, rel_bias, lam_p, out_norm, lam_init, bt, t_new):
    n_pages = page_table.shape[1]
    past = n_pages * PAGE_SIZE
    n_phys = cache_k.shape[0]
    nrow = DIFF_KV_HEADS * 2 * DIFF_REP * t_new
    q6 = q.reshape(bt, t_new, DIFF_KV_HEADS, DIFF_REP, 2, HEAD_DIM)
    qb = jnp.transpose(q6, (0, 2, 4, 3, 1, 5)).reshape(bt, nrow, HEAD_DIM)
    tab = rel_bias.astype(F32)
    tt = jnp.arange(t_new)

    def rows_of(b):
        n = b.shape[1]
        b = jnp.transpose(b, (2, 0, 1)).reshape(DIFF_KV_HEADS, 1, DIFF_REP, t_new, n)
        return jnp.broadcast_to(b, (DIFF_KV_HEADS, 2, DIFF_REP, t_new, n)).reshape(nrow, n)

    dist = past + tt[:, None] - jnp.arange(past)[None, :]
    npg = PAGES_PER_STEP
    bias = rows_of(_bias_lookup(tab, dist)).reshape(nrow, n_pages // npg, npg * PAGE_SIZE)
    bias = jnp.transpose(bias, (1, 0, 2))
    nd = tt[:, None] - jnp.arange(PAGE_SIZE)[None, :]
    nbias = rows_of(jnp.where((nd >= 0)[..., None], _bias_lookup(tab, nd), NEG))
    kn = jnp.transpose(k.reshape(bt, t_new, DIFF_K_DIM), (0, 2, 1))
    kn = jnp.pad(kn, ((0, 0), (0, 0), (0, PAGE_SIZE - t_new)))
    vn = jnp.pad(v.reshape(bt, t_new * DIFF_KV_HEADS, DIFF_V_DIM),
                 ((0, 0), (0, (PAGE_SIZE - t_new) * DIFF_KV_HEADS), (0, 0)))
    ck = jnp.transpose(cache_k, (0, 2, 3, 4, 1)).reshape(n_phys, DIFF_K_DIM, PAGE_SIZE)
    cv = cache_v.reshape(n_phys, PAGE_SIZE * DIFF_KV_HEADS, DIFF_V_DIM)

    def page_spec(pp):
        return pl.BlockSpec((1, DIFF_K_DIM, PAGE_SIZE), lambda b, s, pt: (pt[b, s * npg + pp], 0, 0))

    const = lambda shape: pl.BlockSpec(shape, lambda b, s, pt: (0,) * len(shape))
    grid_spec = pltpu.PrefetchScalarGridSpec(
        num_scalar_prefetch=1,
        grid=(bt, n_pages // npg),
        in_specs=[pl.BlockSpec((1, nrow, HEAD_DIM), lambda b, s, pt: (b, 0, 0)),
                  pl.BlockSpec((1, DIFF_K_DIM, PAGE_SIZE), lambda b, s, pt: (b, 0, 0)),
                  pl.BlockSpec((1, DIFF_K_DIM, PAGE_SIZE), lambda b, s, pt: (b, 0, 0))]
                 + [page_spec(pp) for pp in range(npg)] * 2
                 + [pl.BlockSpec((1, nrow, npg * PAGE_SIZE), lambda b, s, pt: (s, 0, 0)),
                    const((nrow, PAGE_SIZE)), const((4, HEAD_DIM)), const((1, DIFF_V_DIM))],
        out_specs=pl.BlockSpec((1, t_new, DIFF_Q_DIM), lambda b, s, pt: (b, 0, 0)),
        scratch_shapes=[pltpu.VMEM((nrow, LANES), F32), pltpu.VMEM((nrow, LANES), F32),
                        pltpu.VMEM((nrow, DIFF_V_DIM), F32), pltpu.VMEM((nrow, DIFF_K_DIM), F32)],
    )
    a = pl.pallas_call(
        functools.partial(_diff_sample_kernel, t_new=t_new, lam_init=lam_init),
        grid_spec=grid_spec,
        out_shape=jax.ShapeDtypeStruct((bt, t_new, DIFF_Q_DIM), BF16),
        compiler_params=_params(("parallel", "arbitrary")),
        name="diff_sample",
    )(page_table, qb, kn, vn, *([ck] * npg), *([cv] * npg), bias, nbias,
      lam_p.astype(F32), out_norm.reshape(1, DIFF_V_DIM).astype(F32))
    return a.reshape(bt * t_new, DIFF_Q_DIM)


def kernel(x_prompt, x_sample, state_ssm_conv, state_ssm, cache_swa_kv0, cache_swa_kv1, cache_swa_kv2, cache_diff_k, cache_diff_v, page_table, rel_bias, norm_mix, norm_ffn, ffn_w_in, ffn_w_out, ssm_w_in, ssm_conv_w, ssm_conv_b, ssm_dt_bias, ssm_a_log, ssm_d, ssm_norm, ssm_w_out, swa_w_qkv, swa_q_norm, swa_k_norm, swa_w_out, diff_w_qkv, diff_q_norm, diff_k_norm, diff_lambda, diff_out_norm, diff_w_out):
    bp, seq, d = x_prompt.shape
    bs, t_new, _ = x_sample.shape
    mp, ms = bp * seq, bs * t_new
    tm_p, tm_s = 256, ms
    xp = x_prompt.reshape(mp, d)
    xs = x_sample.reshape(ms, d)
    swa_caches = (cache_swa_kv0, cache_swa_kv1, cache_swa_kv2)
    conv_p, conv_s = [], []
    ssm_p = ssm_s = None
    swa_p = tuple([] for _ in SWA_GROUPS)
    swa_s = tuple([] for _ in SWA_GROUPS)
    dk_p, dk_s, dv_p, dv_s = [], [], [], []
    i_ssd = i_swa = i_diff = 0
    w1, w2 = ffn_w_in.astype(BF16), ffn_w_out.astype(BF16)
    for layer in range(DEPTH):
        kind = layer % 3
        g_mix = norm_mix[layer].astype(F32)
        if kind == 0:
            i = i_ssd
            i_ssd += 1
            w_in = ssm_w_in[i].astype(BF16)
            weights = [w_in[:, :D_INNER], w_in[:, D_INNER:D_INNER + CONV_DIM],
                       jnp.pad(w_in[:, D_INNER + CONV_DIM:], ((0, 0), (0, LANES - SSM_HEADS)))]
            wo = ssm_w_out[i].astype(BF16)
            cw, cb = ssm_conv_w[i].astype(F32), ssm_conv_b[i].astype(F32)
            ssd_args = (cw, cb, ssm_dt_bias[i], ssm_a_log[i], ssm_d[i], ssm_norm[i].astype(F32))
            z, xbc, dt = _norm_proj(xp, g_mix, weights, [None] * 3, tm_p)
            xbc3 = xbc.reshape(bp, seq, CONV_DIM)
            y, ns = _ssd_mixer(z.reshape(bp, seq, D_INNER), xbc3, dt.reshape(bp, seq, LANES),
                               jnp.zeros((bp, CONV_W - 1, CONV_DIM), F32),
                               jnp.zeros((1, bp, SSM_HEADS, SSM_HEAD_DIM, D_STATE), F32), 0, *ssd_args,
                               prev_new=ssm_p)
            conv_p.append(xbc3[:, seq - (CONV_W - 1):])
            ssm_p = ns
            ap = y.reshape(mp, D_INNER)
            z, xbc, dt = _norm_proj(xs, g_mix, weights, [None] * 3, tm_s)
            xbc3 = xbc.reshape(bs, t_new, CONV_DIM)
            y, ns = _ssd_mixer(z.reshape(bs, t_new, D_INNER), xbc3, dt.reshape(bs, t_new, LANES),
                               state_ssm_conv[i], state_ssm, i, *ssd_args, prev_new=ssm_s)
            conv_s.append(xbc3[:, t_new - (CONV_W - 1):])
            ssm_s = ns
            as_ = y.reshape(ms, D_INNER)
        elif kind == 1:
            i = i_swa
            i_swa += 1
            w_qkv = swa_w_qkv[i].astype(BF16)
            weights = [w_qkv[:, :SWA_DIM], w_qkv[:, SWA_DIM:2 * SWA_DIM], w_qkv[:, 2 * SWA_DIM:]]
            gains = [jnp.tile(swa_q_norm[i].astype(F32), N_ATTN_HEADS).reshape(1, SWA_DIM),
                     jnp.tile(swa_k_norm[i].astype(F32), N_ATTN_HEADS).reshape(1, SWA_DIM), None]
            wo = swa_w_out[i].astype(BF16)
            q, kt, k, vt, v = _norm_proj(xp, g_mix, weights, gains, tm_p, emit=("b", "tb", "tb"), seq=seq)
            ap = _swa_prompt(q, k, v, rel_bias, bp, seq)
            for g, (win, dil) in enumerate(SWA_GROUPS):
                keep = min(win, seq)
                sl = slice(g * SWA_GROUP_DIM, (g + 1) * SWA_GROUP_DIM)
                kv_t = jnp.stack([kt[:, sl, seq - keep:], vt[:, sl, seq - keep:]], axis=1)
                kv_t = kv_t.reshape(bp, 2, SWA_HEADS_PER_GROUP, HEAD_DIM, keep)
                swa_p[g].append(jnp.transpose(kv_t, (0, 4, 1, 2, 3)))
            q, k, v = _norm_proj(xs, g_mix, weights, gains, tm_s)
            q3, k3, v3 = (t.reshape(bs, t_new, SWA_DIM) for t in (q, k, v))
            as_, new_bufs = _swa_sample(q3, k3, v3, [c[i] for c in swa_caches], rel_bias, bs, t_new)
            for g in range(len(SWA_GROUPS)):
                swa_s[g].append(new_bufs[g])
        else:
            i = i_diff
            i_diff += 1
            lam_init = 0.8 - 0.6 * math.exp(-0.3 * layer)
            w_qkv = diff_w_qkv[i].astype(BF16)
            weights = [w_qkv[:, :DIFF_Q_DIM], w_qkv[:, DIFF_Q_DIM:DIFF_Q_DIM + DIFF_K_DIM],
                       w_qkv[:, DIFF_Q_DIM + DIFF_K_DIM:]]
            gains = [jnp.tile(diff_q_norm[i].astype(F32), DIFF_Q_DIM // HEAD_DIM).reshape(1, DIFF_Q_DIM),
                     jnp.tile(diff_k_norm[i].astype(F32), DIFF_K_DIM // HEAD_DIM).reshape(1, DIFF_K_DIM), None]
            wo = diff_w_out[i].astype(BF16)
            q, k, kt, v, v_rows = _norm_proj(xp, g_mix, weights, gains, tm_p, emit=("n", "nt", "ns"), seq=seq)
            ap = _diff_prompt(q, k, v, rel_bias, diff_lambda[i], diff_out_norm[i], lam_init, bp, seq)
            dk_p.append(jnp.transpose(kt.reshape(bp, DIFF_KV_HEADS, 2, HEAD_DIM, seq), (0, 4, 1, 2, 3)))
            dv_p.append(v_rows.reshape(bp, seq, DIFF_KV_HEADS, DIFF_V_DIM))
            q, k, v = _norm_proj(xs, g_mix, weights, gains, tm_s)
            as_ = _diff_sample(q, k, v, cache_diff_k[i], cache_diff_v[i], page_table, rel_bias, diff_lambda[i],
                               diff_out_norm[i], lam_init, bs, t_new)
            dk_s.append(k.reshape(bs, t_new, DIFF_KV_HEADS, 2, HEAD_DIM))
            dv_s.append(v.reshape(bs, t_new, DIFF_KV_HEADS, DIFF_V_DIM))
        g_ffn = norm_ffn[layer].astype(F32)
        xp = _mix_ffn(xp, ap, wo, g_ffn, w1, w2, layer, 2 * tm_p)
        xs = _mix_ffn(xs, as_, wo, g_ffn, w1, w2, layer, tm_s)
    return (xp.reshape(bp, seq, d), xs.reshape(bs, t_new, d),
            jnp.stack(conv_p), jnp.stack(conv_s), ssm_p, ssm_s,
            jnp.stack(swa_p[0]), jnp.stack(swa_s[0]), jnp.stack(swa_p[1]), jnp.stack(swa_s[1]),
            jnp.stack(swa_p[2]), jnp.stack(swa_s[2]),
            jnp.stack(dk_p), jnp.stack(dk_s), jnp.stack(dv_p), jnp.stack(dv_s))
```

```python
import functools
import math

import jax
import jax.numpy as jnp
from jax import lax
from jax.experimental import pallas as pl
from jax.experimental.pallas import tpu as pltpu

F32 = jnp.float32
BF16 = jnp.bfloat16

D_MODEL = 1024
DEPTH = 4
D_FF = 2816
D_INNER = 2048
SSM_HEADS = 32
SSM_HEAD_DIM = 64
SSM_GROUPS = 4
D_STATE = 128
CONV_W = 4
CONV_DIM = D_INNER + 2 * SSM_GROUPS * D_STATE
SSD_CHUNK = 128
SHORT_CHUNK = 16
HEAD_DIM = 64
N_ATTN_HEADS = 12
NUM_BUCKETS = 32
MAX_DISTANCE = 2048
SWA_GROUPS = ((128, 1), (512, 4), (2048, 16))
SWA_HEADS_PER_GROUP = 4
SWA_GROUP_DIM = SWA_HEADS_PER_GROUP * HEAD_DIM
SWA_DIM = N_ATTN_HEADS * HEAD_DIM
SWA_STEPS = 128
DIFF_KV_HEADS = 4
DIFF_REP = 3
DIFF_V_DIM = 128
DIFF_Q_DIM = N_ATTN_HEADS * 2 * HEAD_DIM
DIFF_K_DIM = DIFF_KV_HEADS * 2 * HEAD_DIM
DIFF_V_ALL = DIFF_KV_HEADS * DIFF_V_DIM
PAGE_SIZE = 128
EPS = 1e-6
NEG = -1e30

LANES = 128
VMEM_LIMIT = 56 * 1024 * 1024
PAGES_PER_STEP = 32


def _dot(a, b):
    return jnp.dot(a, b, preferred_element_type=F32)


def _dot_nt(a, b):
    return lax.dot_general(a, b, (((1,), (1,)), ((), ())), preferred_element_type=F32)


def _split3(x):
    hi = x.astype(BF16)
    r = x - hi.astype(F32)
    mid = r.astype(BF16)
    lo = (r - mid.astype(F32)).astype(BF16)
    return hi, mid, lo


def _dot_exact_rhs(a3_bf16, x):
    return _dot(a3_bf16, jnp.concatenate(_split3(x), axis=0))


def _dot_exact_lhs(x, a3_bf16):
    return _dot(jnp.concatenate(_split3(x), axis=1), a3_bf16)


def _sigmoid(x):
    return 1.0 / (1.0 + jnp.exp(-x))


def _rms(x, g):
    return x * lax.rsqrt(jnp.mean(x * x, axis=-1, keepdims=True) + EPS) * g


def _resident(shape):
    n = len(shape)
    return pl.BlockSpec(shape, lambda *_: (0,) * n, pipeline_mode=pl.Buffered(1))


def _params(sem):
    return pltpu.CompilerParams(dimension_semantics=sem, vmem_limit_bytes=VMEM_LIMIT)


def _norm_proj_kernel(*refs, n_out, head_norm, emit):
    x_ref, g_ref = refs[0], refs[1]
    w_refs = refs[2:2 + n_out]
    n_hn = sum(head_norm)
    hn_refs = refs[2 + n_out:2 + n_out + n_hn]
    pos = 2 + n_out + n_hn
    seg_ref = refs[pos] if n_hn else None
    pos += 1 if n_hn else 0
    o_refs = list(refs[pos:])
    h = _rms(x_ref[...], g_ref[...]).astype(BF16)
    k = 0
    for i in range(n_out):
        y = _dot(h, w_refs[i][...])
        if head_norm[i]:
            gain = hn_refs[k][...]
            k += 1
            seg = seg_ref[...]
            parts = []
            for c in range(y.shape[1] // seg.shape[0]):
                yc = y[:, c * seg.shape[0]:(c + 1) * seg.shape[0]]
                sq = yc * yc
                hi = sq.astype(BF16)
                lo = (sq - hi.astype(F32)).astype(BF16)
                ms = (_dot(hi, seg) + _dot(lo, seg)) * (1.0 / HEAD_DIM)
                parts.append(yc * lax.rsqrt(ms + EPS))
            y = jnp.concatenate(parts, axis=1) * gain
        if "n" in emit[i]:
            o_refs.pop(0)[...] = y
        if "t" in emit[i]:
            o_refs.pop(0)[0] = y.T
        if "b" in emit[i]:
            ref = o_refs.pop(0)
            for cb in range(y.shape[1] // LANES):
                ref[0, cb] = y[:, cb * LANES:(cb + 1) * LANES]
        if "s" in emit[i]:
            ref = o_refs.pop(0)
            ncb = y.shape[1] // LANES
            for cb in range(ncb):
                ref[pl.ds(cb, y.shape[0], stride=ncb), :] = y[:, cb * LANES:(cb + 1) * LANES]


def _norm_proj(x, g, weights, head_gains, tm, emit=None, seq=None):
    m, d = x.shape
    n_out = len(weights)
    emit = tuple(emit or ("n",) * n_out)
    head_norm = tuple(hg is not None for hg in head_gains)
    ins = [x, g.reshape(1, d)] + list(weights)
    specs = [pl.BlockSpec((tm, d), lambda i: (i, 0)), _resident((1, d))]
    specs += [_resident(w.shape) for w in weights]
    for hg in head_gains:
        if hg is not None:
            ins.append(hg)
            specs.append(_resident(hg.shape))
    if any(head_norm):
        r = jnp.arange(2 * LANES)
        seg = (r[:, None] // HEAD_DIM == r[None, :] // HEAD_DIM).astype(BF16)
        ins.append(seg)
        specs.append(_resident(seg.shape))
    per_seq = seq // tm if seq else 1
    out_specs, out_shape = [], []
    for w, e in zip(weights, emit):
        n = w.shape[1]
        if "n" in e:
            out_specs.append(pl.BlockSpec((tm, n), lambda i: (i, 0)))
            out_shape.append(jax.ShapeDtypeStruct((m, n), F32))
        if "t" in e:
            out_specs.append(pl.BlockSpec((1, n, tm), lambda i: (i // per_seq, 0, i % per_seq)))
            out_shape.append(jax.ShapeDtypeStruct((m // seq, n, seq), F32))
        if "b" in e:
            out_specs.append(pl.BlockSpec((1, n // LANES, tm, LANES), lambda i: (i // per_seq, 0, i % per_seq, 0)))
            out_shape.append(jax.ShapeDtypeStruct((m // seq, n // LANES, seq, LANES), F32))
        if "s" in e:
            out_specs.append(pl.BlockSpec((tm * (n // LANES), LANES), lambda i: (i, 0)))
            out_shape.append(jax.ShapeDtypeStruct((m * (n // LANES), LANES), F32))
    return pl.pallas_call(
        functools.partial(_norm_proj_kernel, n_out=n_out, head_norm=head_norm, emit=emit),
        grid=(m // tm,),
        in_specs=specs,
        out_specs=out_specs,
        out_shape=out_shape,
        compiler_params=_params(("parallel",)),
        name="norm_proj",
    )(*ins)


def _mix_ffn_kernel(x_ref, a_ref, wo_ref, g_ref, w1_ref, w2_ref, o_ref):
    x1 = x_ref[...] + _dot(a_ref[...], wo_ref[...])
    h = _rms(x1, g_ref[...]).astype(BF16)
    gu = _dot(h, w1_ref[0])
    gate = gu[:, :D_FF]
    act = (gate * _sigmoid(gate) * gu[:, D_FF:]).astype(BF16)
    o_ref[...] = x1 + _dot(act, w2_ref[0])


def _mix_ffn(x, a, wo, g, w1, w2, layer, tm):
    m, d = x.shape
    ka = a.shape[1]
    slab = lambda w: pl.BlockSpec((1,) + w.shape[1:], lambda i: (layer, 0, 0), pipeline_mode=pl.Buffered(1))
    return pl.pallas_call(
        _mix_ffn_kernel,
        grid=(m // tm,),
        in_specs=[pl.BlockSpec((tm, d), lambda i: (i, 0)),
                  pl.BlockSpec((tm, ka), lambda i: (i, 0)),
                  _resident(wo.shape), _resident((1, d)), slab(w1), slab(w2)],
        out_specs=pl.BlockSpec((tm, d), lambda i: (i, 0)),
        out_shape=jax.ShapeDtypeStruct((m, d), F32),
        compiler_params=_params(("parallel",)),
        name="mix_ffn",
    )(x, a, wo, g.reshape(1, d), w1, w2)


def _ssd_kernel(z_ref, xbc_ref, dt_ref, cst_ref, sst_ref, cw_ref, cb_ref, dtb_ref, alog_ref,
                dx_ref, ng_ref, e_ref, *rest, q, n_valid, n_prev):
    prev_ref = rest[0] if n_prev else None
    y_ref, ns_ref, ext_ref, st_ref, yacc_ref, *pad_refs = rest[1 if n_prev else 0:]
    c = pl.program_id(1)
    last = pl.num_programs(1) - 1
    padded = n_valid < q

    @pl.when(c == 0)
    def _():
        ext_ref[...] = jnp.zeros(ext_ref.shape, F32)
        ext_ref[5:8, :] = cst_ref[0]
        st_ref[...] = sst_ref[0, 0]
        if padded:
            for r in pad_refs:
                r[...] = jnp.zeros(r.shape, F32)

    ext_ref[8:8 + n_valid, :] = xbc_ref[0]
    ext = ext_ref[...]
    ext1 = pltpu.roll(ext, 1, axis=0)
    older = pltpu.roll(ext * cw_ref[1:2, :] + ext1 * cw_ref[0:1, :], 2, axis=0)
    conv = (cb_ref[...] + ext[8:8 + q] * cw_ref[3:4, :] + ext1[8:8 + q] * cw_ref[2:3, :]) + older[8:8 + q]
    if not padded:
        ext_ref[5:8, :] = ext_ref[q + 5:q + 8, :]
    act = conv * _sigmoid(conv)
    xs = act[:, :D_INNER]
    bm = act[:, D_INNER:D_INNER + SSM_GROUPS * D_STATE].astype(BF16)
    cm = act[:, D_INNER + SSM_GROUPS * D_STATE:].astype(BF16)

    if padded:
        zpad_ref, dtpad_ref = pad_refs
        zpad_ref[0:n_valid, :] = z_ref[0]
        dtpad_ref[0:n_valid, :] = dt_ref[0]
        z = zpad_ref[...]
        dt_raw = dtpad_ref[...]
    else:
        z = z_ref[0]
        dt_raw = dt_ref[0]

    row = lax.broadcasted_iota(jnp.int32, (q, q), 0)
    col = lax.broadcasted_iota(jnp.int32, (q, q), 1)
    causal = col <= row
    tri = jnp.where(causal, 1.0, 0.0).astype(BF16)
    tri_t = jnp.where(row <= col, 1.0, 0.0).astype(BF16)
    tri3 = jnp.concatenate([tri] * 3, axis=1)
    tri_t3 = jnp.concatenate([tri_t] * 3, axis=0)

    xdt_in = dt_raw + dtb_ref[...]
    dt = jnp.maximum(xdt_in, 0.0) + jnp.log(1.0 + jnp.exp(-jnp.abs(xdt_in)))
    if padded:
        dt = jnp.where(lax.broadcasted_iota(jnp.int32, dt.shape, 0) < n_valid, dt, 0.0)
    adt = dt * (-jnp.exp(alog_ref[...]))
    acum = _dot_exact_rhs(tri3, adt)
    acum_t = _dot_exact_lhs(adt.T, tri_t3)
    both_x = _dot_exact_lhs(jnp.concatenate([dt, acum], axis=0), e_ref[...])
    dt_x, acum_x = both_x[0:q], both_x[q:2 * q]
    exp_acum_x = jnp.exp(acum_x)
    acum_last_x = acum_x[q - 1:q, :]
    xdt = xs * dt_x
    xdt_b = xdt.astype(BF16)
    xd = xdt * jnp.exp(acum_last_x - acum_x)
    dtot_x = jnp.exp(acum_last_x)
    lane_lo = lax.broadcasted_iota(jnp.int32, (q, LANES), 1) < SSM_HEAD_DIM

    gw = SSM_HEADS // SSM_GROUPS * SSM_HEAD_DIM
    for g in range(SSM_GROUPS):
        cg = cm[:, g * D_STATE:(g + 1) * D_STATE]
        bg = bm[:, g * D_STATE:(g + 1) * D_STATE]
        cb = _dot_nt(cg, bg)
        st_g = st_ref[g * gw:(g + 1) * gw, :]
        yoff = _dot_nt(cg, st_g.astype(BF16))
        yacc_ref[:, g * gw:(g + 1) * gw] = yoff * exp_acum_x[:, g * gw:(g + 1) * gw]
        for pair in range(gw // LANES):
            h0 = g * (gw // SSM_HEAD_DIM) + 2 * pair
            xp = xdt_b[:, h0 * SSM_HEAD_DIM:h0 * SSM_HEAD_DIM + LANES]
            ys = []
            for h in (h0, h0 + 1):
                seg = jnp.broadcast_to(acum[:, h:h + 1], (q, q)) - jnp.broadcast_to(acum_t[h:h + 1, :], (q, q))
                w = cb * jnp.exp(jnp.where(causal, seg, NEG))
                ys.append(_dot(w.astype(BF16), xp))
            yacc_ref[:, h0 * SSM_HEAD_DIM:h0 * SSM_HEAD_DIM + LANES] += jnp.where(lane_lo, ys[0], ys[1])
        xd_t = jnp.concatenate(
            [xd[:, g * gw + t * LANES:g * gw + (t + 1) * LANES].T for t in range(gw // LANES)], axis=0)
        upd = _dot(xd_t.astype(BF16), bg)
        dtot = jnp.concatenate(
            [jnp.broadcast_to(dtot_x[:, g * gw + t * LANES:g * gw + (t + 1) * LANES], (LANES, LANES)).T
             for t in range(gw // LANES)], axis=0)
        st_ref[g * gw:(g + 1) * gw, :] = dtot * st_g + upd

    y = yacc_ref[...] + xs * dx_ref[...]
    gated = y * (z * _sigmoid(z))
    outs = []
    for g in range(SSM_GROUPS):
        gg = gated[:, g * gw:(g + 1) * gw]
        outs.append(gg * lax.rsqrt(jnp.mean(gg * gg, axis=-1, keepdims=True) + EPS))
    yn = jnp.concatenate(outs, axis=1) * ng_ref[...]
    y_ref[0] = yn[0:n_valid].astype(BF16)

    @pl.when(c == last)
    def _():
        for k in range(n_prev):
            ns_ref[k, 0] = prev_ref[k, 0]
        ns_ref[n_prev, 0] = st_ref[...]


def _ssd_mixer(z, xbc, dt, conv_state, ssm_states, state_idx, conv_w, conv_b, dt_bias, a_log, d_skip, norm_g,
               prev_new=None):
    bt, seq, _ = z.shape
    q = SSD_CHUNK if seq >= SSD_CHUNK else SHORT_CHUNK
    n_valid = min(q, seq)
    nc = max(1, seq // q)
    pad = lambda v: jnp.pad(v.astype(F32), (0, LANES - SSM_HEADS)).reshape(1, LANES)
    hh = jnp.arange(LANES)[:, None]
    cc = jnp.arange(D_INNER)[None, :] // SSM_HEAD_DIM
    expand = jnp.tile((hh == cc).astype(BF16), (3, 1))
    dx = jnp.repeat(d_skip.astype(F32), SSM_HEAD_DIM).reshape(1, D_INNER)
    rows = SSM_HEADS * SSM_HEAD_DIM
    sst = ssm_states.reshape(ssm_states.shape[0], bt, rows, D_STATE)
    n_prev = 0 if prev_new is None else prev_new.shape[0]
    prev_in = [] if prev_new is None else [prev_new.reshape(n_prev, bt, rows, D_STATE)]
    prev_spec = [pl.BlockSpec((n_prev, 1, rows, D_STATE), lambda b, c: (0, b, 0, 0))] * len(prev_in)
    scratch = [pltpu.VMEM((q + 8, CONV_DIM), F32),
               pltpu.VMEM((SSM_HEADS * SSM_HEAD_DIM, D_STATE), F32),
               pltpu.VMEM((q, D_INNER), F32)]
    if n_valid < q:
        scratch += [pltpu.VMEM((q, D_INNER), F32), pltpu.VMEM((q, LANES), F32)]
    y, ns = pl.pallas_call(
        functools.partial(_ssd_kernel, q=q, n_valid=n_valid, n_prev=n_prev),
        grid=(bt, nc),
        in_specs=[pl.BlockSpec((1, n_valid, D_INNER), lambda b, c: (b, c, 0)),
                  pl.BlockSpec((1, n_valid, CONV_DIM), lambda b, c: (b, c, 0)),
                  pl.BlockSpec((1, n_valid, LANES), lambda b, c: (b, c, 0)),
                  pl.BlockSpec((1, CONV_W - 1, CONV_DIM), lambda b, c: (b, 0, 0)),
                  pl.BlockSpec((1, 1, SSM_HEADS * SSM_HEAD_DIM, D_STATE), lambda b, c: (state_idx, b, 0, 0)),
                  _resident((CONV_W, CONV_DIM)), _resident((1, CONV_DIM)),
                  _resident((1, LANES)), _resident((1, LANES)),
                  _resident((1, D_INNER)), _resident((1, D_INNER)), _resident((3 * LANES, D_INNER))] + prev_spec,
        out_specs=[pl.BlockSpec((1, n_valid, D_INNER), lambda b, c: (b, c, 0)),
                   pl.BlockSpec((n_prev + 1, 1, rows, D_STATE), lambda b, c: (0, b, 0, 0))],
        out_shape=[jax.ShapeDtypeStruct((bt, seq, D_INNER), BF16),
                   jax.ShapeDtypeStruct((n_prev + 1, bt, rows, D_STATE), F32)],
        scratch_shapes=scratch,
        compiler_params=_params(("parallel", "arbitrary")),
        name="ssd_mixer",
    )(z, xbc, dt, conv_state, sst, conv_w, conv_b.reshape(1, CONV_DIM), pad(dt_bias), pad(a_log),
      dx, norm_g.reshape(1, D_INNER), expand, *prev_in)
    return y, ns.reshape(n_prev + 1, bt, SSM_HEADS, SSM_HEAD_DIM, D_STATE)


def _t5_bucket(dist):
    max_exact = NUM_BUCKETS // 2
    d = jnp.maximum(dist, 0)
    df = jnp.maximum(d, 1).astype(F32)
    large = max_exact + (jnp.log(df / max_exact) / math.log(MAX_DISTANCE / max_exact)
                         * (NUM_BUCKETS - max_exact)).astype(jnp.int32)
    return jnp.where(d < max_exact, d, jnp.minimum(large, NUM_BUCKETS - 1))


def _bias_lookup(tab, dist):
    onehot = jax.nn.one_hot(_t5_bucket(dist), NUM_BUCKETS, dtype=F32)
    return jnp.einsum('...b,bh->...h', onehot, tab.astype(F32), precision=lax.Precision.HIGHEST)


def _swa_prompt_bias(rel_bias, g, dil):
    i = jnp.arange(SWA_STEPS)[:, None]
    j = jnp.arange(2 * SWA_STEPS)[None, :]
    step = SWA_STEPS + i - j
    tab = rel_bias[:, g * SWA_HEADS_PER_GROUP:(g + 1) * SWA_HEADS_PER_GROUP].astype(F32)
    bias = jnp.transpose(_bias_lookup(tab, jnp.clip(step, 0, SWA_STEPS) * dil), (2, 0, 1))
    valid = (step >= 0) & (step <= SWA_STEPS)
    first = valid & (j >= SWA_STEPS)
    both = jnp.stack([jnp.where(first, bias, NEG), jnp.where(valid, bias, NEG)])
    return both.reshape(2, SWA_HEADS_PER_GROUP * SWA_STEPS, 2 * SWA_STEPS)


def _swa_fused_kernel(q_ref, k_ref, v_ref, b_ref, a_ref, qs_ref, ks_ref, vs_ref, oc_ref, lc_ref, ot_ref, lt_ref,
                      *, seq):
    g = pl.program_id(1)
    gd = SWA_GROUP_DIM
    blk = SWA_STEPS
    n_units = seq // blk
    lane_head = lax.broadcasted_iota(jnp.int32, (blk, gd), 1) // HEAD_DIM

    def group_body(gi, dil):
        rows = seq // dil
        nb = rows // blk
        ks_ref[0:blk, :] = jnp.zeros((blk, gd), BF16)
        vs_ref[0:blk, :] = jnp.zeros((blk, gd), BF16)
        for c in range(dil):
            src = pl.ds(c, rows, stride=dil) if dil > 1 else pl.ds(0, rows)
            dst = slice(blk + c * rows, blk + (c + 1) * rows)
            for half in range(gd // LANES):
                lanes = slice(half * LANES, (half + 1) * LANES)
                qs_ref[dst, lanes] = (q_ref[0, half, src, :] * (HEAD_DIM ** -0.5)).astype(BF16)
                ks_ref[dst, lanes] = k_ref[0, half, src, :].astype(BF16)
                vs_ref[dst, lanes] = v_ref[0, half, src, :].astype(BF16)

        def unit(u, carry):
            start = pl.multiple_of(u * blk, blk)
            first = lax.rem(u, nb) == 0
            qv = qs_ref[pl.ds(blk + start, blk), :]
            kk = ks_ref[pl.ds(start, 2 * blk), :]
            vv = vs_ref[pl.ds(start, 2 * blk), :]
            sel = jnp.where(first, 0, 1)
            nh = SWA_HEADS_PER_GROUP
            qh = jnp.concatenate([jnp.where(lane_head == h, qv, jnp.zeros_like(qv)) for h in range(nh)], axis=0)
            s = _dot_nt(qh, kk) + b_ref[gi, sel]
            m = jnp.max(s, axis=-1, keepdims=True)
            p = jnp.exp(s - m)
            l = jnp.sum(p, axis=-1, keepdims=True)
            of = _dot(p.astype(BF16), vv) / l
            lf = jnp.broadcast_to(m + jnp.log(l), (nh * blk, gd))
            o, lse = of[0:blk], lf[0:blk]
            for h in range(1, nh):
                o = jnp.where(lane_head == h, of[h * blk:(h + 1) * blk], o)
                lse = jnp.where(lane_head == h, lf[h * blk:(h + 1) * blk], lse)
            oc_ref[pl.ds(start, blk), :] = o
            lc_ref[pl.ds(start, blk), :] = lse
            return carry

        lax.fori_loop(0, n_units, unit, 0, unroll=2)
        for c in range(dil):
            dst = pl.ds(c, rows, stride=dil) if dil > 1 else pl.ds(0, rows)
            for half in range(gd // LANES):
                lanes = slice(half * LANES, (half + 1) * LANES)
                cb = gi * (gd // LANES) + half
                ot_ref[cb, dst, :] = oc_ref[c * rows:(c + 1) * rows, lanes]
                lt_ref[cb, dst, :] = lc_ref[c * rows:(c + 1) * rows, lanes]

    for gi, (_, dil) in enumerate(SWA_GROUPS):
        pl.when(g == gi)(functools.partial(group_body, gi, dil))

    @pl.when(g == len(SWA_GROUPS) - 1)
    def _():
        def tile(t, carry):
            r = pl.ds(pl.multiple_of(t * blk, blk), blk)
            per = gd // LANES
            for half in range(per):
                ls = [lt_ref[gi * per + half, r, :] for gi in range(len(SWA_GROUPS))]
                m = jnp.maximum(jnp.maximum(ls[0], ls[1]), ls[2])
                es = [jnp.exp(l - m) for l in ls]
                inv = 1.0 / (es[0] + es[1] + es[2])
                for gi in range(len(SWA_GROUPS)):
                    cb = gi * per + half
                    a_ref[0, r, cb * LANES:(cb + 1) * LANES] = (ot_ref[cb, r, :] * es[gi] * inv).astype(BF16)
            return carry

        lax.fori_loop(0, n_units, tile, 0)


def _swa_prompt(q, k, v, rel_bias, bt, seq):
    bias = jnp.stack([_swa_prompt_bias(rel_bias, g, dil) for g, (_, dil) in enumerate(SWA_GROUPS)])
    gd = SWA_GROUP_DIM
    tok = pl.BlockSpec((1, gd // LANES, seq, LANES), lambda b, g: (b, g, 0, 0))
    a = pl.pallas_call(
        functools.partial(_swa_fused_kernel, seq=seq),
        grid=(bt, len(SWA_GROUPS)),
        in_specs=[tok, tok, tok, _resident(bias.shape)],
        out_specs=pl.BlockSpec((1, seq, SWA_DIM), lambda b, g: (b, 0, 0)),
        out_shape=jax.ShapeDtypeStruct((bt, seq, SWA_DIM), BF16),
        scratch_shapes=[pltpu.VMEM((seq + SWA_STEPS, gd), BF16)] * 3
                       + [pltpu.VMEM((seq, gd), F32)] * 2 + [pltpu.VMEM((SWA_DIM // LANES, seq, LANES), F32)] * 2,
        compiler_params=_params(("parallel", "arbitrary")),
        name="swa_prompt",
    )(q, k, v, bias)
    return a.reshape(bt * seq, SWA_DIM)


def _swa_sample_kernel(q_ref, kn_ref, c0_ref, c1_ref, c2_ref, b0_ref, b1_ref, b2_ref, nb_ref,
                       a_ref, o0_ref, o1_ref, o2_ref, *, t_new):
    gd = SWA_GROUP_DIM
    nh = SWA_HEADS_PER_GROUP
    lane_head = lax.broadcasted_iota(jnp.int32, (t_new, gd), 1) // HEAD_DIM
    caches = (c0_ref, c1_ref, c2_ref)
    biases = (b0_ref, b1_ref, b2_ref)
    new_caches = (o0_ref, o1_ref, o2_ref)
    outs, lses = [], []
    for g in range(len(SWA_GROUPS)):
        c_ref = caches[g]
        win = c_ref.shape[2]
        qg = q_ref[0, :, g * gd:(g + 1) * gd] * (HEAD_DIM ** -0.5)
        qs = jnp.concatenate([jnp.where(lane_head == h, qg, 0.0) for h in range(nh)], axis=0).astype(BF16)
        new = kn_ref[0, g]
        s = _dot(qs, c_ref[0, 0:gd, :].astype(BF16)) + biases[g][...]
        sn = _dot(qs, new[0:gd].astype(BF16)) + nb_ref[g]
        m = jnp.maximum(jnp.max(s, axis=-1, keepdims=True), jnp.max(sn, axis=-1, keepdims=True))
        p = jnp.exp(s - m)
        pn = jnp.exp(sn - m)
        l = jnp.sum(p, axis=-1, keepdims=True) + jnp.sum(pn, axis=-1, keepdims=True)
        of = (_dot_nt(p.astype(BF16), c_ref[0, gd:2 * gd, :].astype(BF16))
              + _dot_nt(pn.astype(BF16), new[gd:2 * gd].astype(BF16))) / l
        lf = jnp.broadcast_to(m + jnp.log(l), (nh * t_new, gd))
        o = jnp.zeros((t_new, gd), F32)
        lse = jnp.zeros((t_new, gd), F32)
        for h in range(nh):
            o = jnp.where(lane_head == h, of[h * t_new:(h + 1) * t_new], o)
            lse = jnp.where(lane_head == h, lf[h * t_new:(h + 1) * t_new], lse)
        outs.append(o)
        lses.append(lse)
        rolled = pltpu.roll(c_ref[0], win - t_new, axis=1)
        if win > LANES:
            new_caches[g][0, :, 0:win - LANES] = rolled[:, 0:win - LANES]
        keep = lax.broadcasted_iota(jnp.int32, (2 * gd, LANES), 1) < LANES - t_new
        new_caches[g][0, :, win - LANES:win] = jnp.where(keep, rolled[:, win - LANES:win], new)
    m = jnp.maximum(jnp.maximum(lses[0], lses[1]), lses[2])
    es = [jnp.exp(l - m) for l in lses]
    inv = 1.0 / (es[0] + es[1] + es[2])
    for g in range(len(SWA_GROUPS)):
        a_ref[0, :, g * gd:(g + 1) * gd] = (outs[g] * es[g] * inv).astype(BF16)


def _swa_sample_bias(rel_bias, t_new):
    t = jnp.arange(t_new)
    biases, new_biases = [], []
    for g, (win, dil) in enumerate(SWA_GROUPS):
        tab = rel_bias[:, g * SWA_HEADS_PER_GROUP:(g + 1) * SWA_HEADS_PER_GROUP].astype(F32)
        back = t[:, None] - jnp.arange(win)[None, :]
        ok = (back <= 0) & (back % dil == 0)
        b = jnp.where(ok[..., None], _bias_lookup(tab, win + back), NEG)
        biases.append(jnp.transpose(b, (2, 0, 1)).reshape(SWA_HEADS_PER_GROUP * t_new, win))
        j = jnp.arange(LANES)[None, :] - (LANES - t_new)
        nd = t[:, None] - j
        nok = (j >= 0) & (nd >= 0) & (nd % dil == 0)
        nb = jnp.where(nok[..., None], _bias_lookup(tab, nd), NEG)
        new_biases.append(jnp.transpose(nb, (2, 0, 1)).reshape(SWA_HEADS_PER_GROUP * t_new, LANES))
    return biases, jnp.stack(new_biases)


def _swa_sample(q, k, v, caches, rel_bias, bt, t_new):
    biases, new_bias = _swa_sample_bias(rel_bias, t_new)
    gd = SWA_GROUP_DIM
    tiles = []
    for g in range(len(SWA_GROUPS)):
        kv = jnp.concatenate([k[:, :, g * gd:(g + 1) * gd], v[:, :, g * gd:(g + 1) * gd]], axis=2)
        tiles.append(jnp.pad(jnp.transpose(kv, (0, 2, 1)), ((0, 0), (0, 0), (LANES - t_new, 0))))
    kn = jnp.stack(tiles, axis=1)
    views = [jnp.transpose(c, (0, 2, 3, 4, 1)).reshape(bt, 2 * gd, c.shape[1]) for c in caches]
    cspecs = [pl.BlockSpec((1, 2 * gd, c.shape[2]), lambda b: (b, 0, 0)) for c in views]
    tok = pl.BlockSpec((1, t_new, SWA_DIM), lambda b: (b, 0, 0))
    res = pl.pallas_call(
        functools.partial(_swa_sample_kernel, t_new=t_new),
        grid=(bt,),
        in_specs=[tok, pl.BlockSpec((1, len(SWA_GROUPS), 2 * gd, LANES), lambda b: (b, 0, 0, 0))] + cspecs
                 + [_resident(b.shape) for b in biases] + [_resident(new_bias.shape)],
        out_specs=[tok] + cspecs,
        out_shape=[jax.ShapeDtypeStruct((bt, t_new, SWA_DIM), BF16)]
                  + [jax.ShapeDtypeStruct(c.shape, F32) for c in views],
        compiler_params=_params(("parallel",)),
        name="swa_sample",
    )(q, kn, *views, *biases, new_bias)
    new_caches = [jnp.transpose(c.reshape(bt, 2, SWA_HEADS_PER_GROUP, HEAD_DIM, c.shape[2]), (0, 4, 1, 2, 3))
                  for c in res[1:]]
    return res[0].reshape(bt * t_new, SWA_DIM), new_caches


def _diff_lambda(lam_ref, lam_init):
    lp = lam_ref[...]
    s1 = jnp.sum(lp[0:1] * lp[1:2], axis=-1, keepdims=True)
    s2 = jnp.sum(lp[2:3] * lp[3:4], axis=-1, keepdims=True)
    return jnp.exp(s1) - jnp.exp(s2) + lam_init


def _diff_prompt_kernel(q_ref, k_ref, v_ref, b_ref, lam_ref, on_ref, o_ref, q6_ref, m_ref, acc_ref,
                        *, tq, tk, lam_init):
    i = pl.program_id(2)
    nsub = tk // LANES
    qsub = tq // LANES
    lane_lo = lax.broadcasted_iota(jnp.int32, (tq, LANES), 1) < HEAD_DIM
    for r in range(DIFF_REP):
        qr = q_ref[0, :, r * LANES:(r + 1) * LANES] * (HEAD_DIM ** -0.5)
        q6_ref[(2 * r) * tq:(2 * r + 1) * tq, :] = jnp.where(lane_lo, qr, 0.0).astype(BF16)
        q6_ref[(2 * r + 1) * tq:(2 * r + 2) * tq, :] = jnp.where(lane_lo, 0.0, qr).astype(BF16)
    m_ref[...] = jnp.full(m_ref.shape, NEG, F32)
    acc_ref[...] = jnp.zeros(acc_ref.shape, F32)
    ones = jnp.ones((tk, LANES), BF16)

    def body(j, carry):
        start = pl.multiple_of(j * tk, tk)
        kb = k_ref[0, pl.ds(start, tk), :].astype(BF16)
        vx = jnp.concatenate([v_ref[0, pl.ds(start, tk), :].astype(BF16), ones], axis=1)
        d0 = i * qsub - j * nsub + (nsub - 1)
        for rm in range(2 * DIFF_REP):
            rows = slice(rm * tq, (rm + 1) * tq)
            bias = jnp.concatenate(
                [jnp.concatenate([b_ref[0, d0 - c + qs, rm // 2] for qs in range(qsub)], axis=0)
                 for c in range(nsub)], axis=1)
            s = _dot_nt(q6_ref[rows, :], kb) + bias
            smax = s[:, 0:LANES]
            for c in range(1, tk // LANES):
                smax = jnp.maximum(smax, s[:, c * LANES:(c + 1) * LANES])
            m_old = m_ref[rows, :]
            m_new = jnp.maximum(m_old, jnp.max(smax, axis=-1, keepdims=True))
            alpha = jnp.exp(m_old - m_new)
            p = jnp.concatenate([jnp.exp(s[:, c * LANES:(c + 1) * LANES] - m_new)
                                 for c in range(tk // LANES)], axis=1).astype(BF16)
            acc_ref[rows, :] = jnp.concatenate([alpha, alpha], axis=1) * acc_ref[rows, :] + _dot(p, vx)
            m_ref[rows, :] = m_new
        return carry

    lax.fori_loop(0, ((i + 1) * tq - 1) // tk + 1, body, 0)
    lam = _diff_lambda(lam_ref, lam_init)
    for r in range(DIFF_REP):
        o0 = acc_ref[(2 * r) * tq:(2 * r + 1) * tq, :]
        o1 = acc_ref[(2 * r + 1) * tq:(2 * r + 2) * tq, :]
        a = o0[:, :DIFF_V_DIM] / o0[:, DIFF_V_DIM:] - lam * (o1[:, :DIFF_V_DIM] / o1[:, DIFF_V_DIM:])
        o_ref[0, :, r * LANES:(r + 1) * LANES] = (_rms(a, on_ref[...]) * (1.0 - lam_init)).astype(BF16)


def _diff_prompt_bias(rel_bias, seq, tk):
    nd = seq // LANES + tk // LANES - 1
    delta = jnp.arange(nd) - (tk // LANES - 1)
    d = delta[:, None, None] * LANES + jnp.arange(LANES)[None, :, None] - jnp.arange(LANES)[None, None, :]
    tab = jnp.where((d >= 0)[..., None], _bias_lookup(rel_bias, d), NEG)
    tab = jnp.transpose(tab, (3, 0, 1, 2)).reshape(DIFF_KV_HEADS, DIFF_REP, nd, LANES, LANES)
    return jnp.transpose(tab, (0, 2, 1, 3, 4))


def _diff_prompt(q, k, v, rel_bias, lam_p, out_norm, lam_init, bt, seq):
    tq, tk = 512, 512
    nb = seq // tq
    bias = _diff_prompt_bias(rel_bias, seq, tk)
    qw = DIFF_REP * 2 * HEAD_DIM
    a = pl.pallas_call(
        functools.partial(_diff_prompt_kernel, tq=tq, tk=tk, lam_init=lam_init),
        grid=(bt, DIFF_KV_HEADS, nb),
        in_specs=[pl.BlockSpec((1, tq, qw), lambda b, g, i: (b, i, g)),
                  pl.BlockSpec((1, seq, LANES), lambda b, g, i: (b, 0, g)),
                  pl.BlockSpec((1, seq, LANES), lambda b, g, i: (b, 0, g)),
                  pl.BlockSpec((1,) + bias.shape[1:], lambda b, g, i: (g, 0, 0, 0, 0)),
                  _resident((4, HEAD_DIM)), _resident((1, DIFF_V_DIM))],
        out_specs=pl.BlockSpec((1, tq, qw), lambda b, g, i: (b, i, g)),
        out_shape=jax.ShapeDtypeStruct((bt, seq, DIFF_Q_DIM), BF16),
        scratch_shapes=[pltpu.VMEM((2 * DIFF_REP * tq, LANES), BF16),
                        pltpu.VMEM((2 * DIFF_REP * tq, LANES), F32),
                        pltpu.VMEM((2 * DIFF_REP * tq, 2 * DIFF_V_DIM), F32)],
        compiler_params=_params(("parallel", "parallel", "arbitrary")),
        name="diff_prompt",
    )(q.reshape(bt, seq, DIFF_Q_DIM), k.reshape(bt, seq, DIFF_K_DIM), v.reshape(bt, seq, DIFF_V_ALL),
      bias, lam_p.astype(F32), out_norm.reshape(1, DIFF_V_DIM).astype(F32))
    return a.reshape(bt * seq, DIFF_Q_DIM)


def _diff_sample_kernel(pt_ref, qb_ref, kn_ref, vn_ref, *refs, t_new, lam_init):
    npg = PAGES_PER_STEP
    k_refs = refs[:npg]
    v_refs = refs[npg:2 * npg]
    b_ref, nb_ref, lam_ref, on_ref, o_ref, m_ref, l_ref, acc_ref, qs_ref = refs[2 * npg:]
    s_idx = pl.program_id(1)
    per_map = DIFF_REP * t_new
    per_g = 2 * per_map

    @pl.when(s_idx == 0)
    def _():
        m_ref[...] = jnp.full(m_ref.shape, NEG, F32)
        l_ref[...] = jnp.zeros(l_ref.shape, F32)
        acc_ref[...] = jnp.zeros(acc_ref.shape, F32)
        qr = qb_ref[0] * (HEAD_DIM ** -0.5)
        own = lax.broadcasted_iota(jnp.int32, qr.shape, 0) // per_map
        for c in range(2 * DIFF_KV_HEADS):
            qs_ref[:, c * HEAD_DIM:(c + 1) * HEAD_DIM] = jnp.where(own == c, qr, 0.0)

    qb = qs_ref[...].astype(BF16)

    def absorb(kt_refs, vv_refs, bias):
        kcat = jnp.concatenate([kt[0].astype(BF16) for kt in kt_refs], axis=1)
        s = _dot(qb, kcat) + bias
        chunks = [s[:, c * LANES:(c + 1) * LANES] for c in range(len(kt_refs))]
        smax = chunks[0]
        for c in chunks[1:]:
            smax = jnp.maximum(smax, c)
        m_old = m_ref[...]
        m_new = jnp.maximum(m_old, jnp.max(smax, axis=-1, keepdims=True))
        alpha = jnp.exp(m_old - m_new)
        ps = [jnp.exp(c - m_new) for c in chunks]
        psum = ps[0]
        for p in ps[1:]:
            psum = psum + p
        l_ref[...] = alpha * l_ref[...] + jnp.sum(psum, axis=-1, keepdims=True)
        pcat = jnp.concatenate(ps, axis=1).astype(BF16)
        for g in range(DIFF_KV_HEADS):
            rows = slice(g * per_g, (g + 1) * per_g)
            vcat = jnp.concatenate([vv[0, pl.ds(g, PAGE_SIZE, stride=DIFF_KV_HEADS), :].astype(BF16)
                                    for vv in vv_refs], axis=0)
            acc_ref[rows, :] = alpha[rows] * acc_ref[rows, :] + _dot(pcat[rows], vcat)
        m_ref[...] = m_new

    absorb(k_refs, v_refs, b_ref[0])

    @pl.when(s_idx == pl.num_programs(1) - 1)
    def _():
        absorb([kn_ref], [vn_ref], nb_ref[...])
        lam = _diff_lambda(lam_ref, lam_init)
        o = acc_ref[...] / l_ref[...]
        for g in range(DIFF_KV_HEADS):
            for r in range(DIFF_REP):
                r0 = g * per_g + r * t_new
                blk = o[r0:r0 + t_new] - lam * o[r0 + per_map:r0 + per_map + t_new]
                col = (g * DIFF_REP + r) * DIFF_V_DIM
                o_ref[0, :, col:col + DIFF_V_DIM] = (_rms(blk, on_ref[...]) * (1.0 - lam_init)).astype(BF16)


def _diff_sample(q, k, v, cache_k, cache_v, page_table, rel_bias, lam_p, out_norm, lam_init, bt, t_new):
    n_pages = page_table.shape[1]
    past = n_pages * PAGE_SIZE
    n_phys = cache_k.shape[0]
    nrow = DIFF_KV_HEADS * 2 * DIFF_REP * t_new
    q6 = q.reshape(bt, t_new, DIFF_KV_HEADS, DIFF_REP, 2, HEAD_DIM)
    qb = jnp.transpose(q6, (0, 2, 4, 3, 1, 5)).reshape(bt, nrow, HEAD_DIM)
    tab = rel_bias.astype(F32)
    tt = jnp.arange(t_new)

    def rows_of(b):
        n = b.shape[1]
        b = jnp.transpose(b, (2, 0, 1)).reshape(DIFF_KV_HEADS, 1, DIFF_REP, t_new, n)
        return jnp.broadcast_to(b, (DIFF_KV_HEADS, 2, DIFF_REP, t_new, n)).reshape(nrow, n)

    dist = past + tt[:, None] - jnp.arange(past)[None, :]
    npg = PAGES_PER_STEP
    bias = rows_of(_bias_lookup(tab, dist)).reshape(nrow, n_pages // npg, npg * PAGE_SIZE)
    bias = jnp.transpose(bias, (1, 0, 2))
    nd = tt[:, None] - jnp.arange(PAGE_SIZE)[None, :]
    nbias = rows_of(jnp.where((nd >= 0)[..., None], _bias_lookup(tab, nd), NEG))
    kn = jnp.transpose(k.reshape(bt, t_new, DIFF_K_DIM), (0, 2, 1))
    kn = jnp.pad(kn, ((0, 0), (0, 0), (0, PAGE_SIZE - t_new)))
    vn = jnp.pad(v.reshape(bt, t_new * DIFF_KV_HEADS, DIFF_V_DIM),
                 ((0, 0), (0, (PAGE_SIZE - t_new) * DIFF_KV_HEADS), (0, 0)))
    ck = jnp.transpose(cache_k, (0, 2, 3, 4, 1)).reshape(n_phys, DIFF_K_DIM, PAGE_SIZE)
    cv = cache_v.reshape(n_phys, PAGE_SIZE * DIFF_KV_HEADS, DIFF_V_DIM)

    def page_spec(pp):
        return pl.BlockSpec((1, DIFF_K_DIM, PAGE_SIZE), lambda b, s, pt: (pt[b, s * npg + pp], 0, 0))

    const = lambda shape: pl.BlockSpec(shape, lambda b, s, pt: (0,) * len(shape))
    grid_spec = pltpu.PrefetchScalarGridSpec(
        num_scalar_prefetch=1,
        grid=(bt, n_pages // npg),
        in_specs=[pl.BlockSpec((1, nrow, HEAD_DIM), lambda b, s, pt: (b, 0, 0)),
                  pl.BlockSpec((1, DIFF_K_DIM, PAGE_SIZE), lambda b, s, pt: (b, 0, 0)),
                  pl.BlockSpec((1, DIFF_K_DIM, PAGE_SIZE), lambda b, s, pt: (b, 0, 0))]
                 + [page_spec(pp) for pp in range(npg)] * 2
                 + [pl.BlockSpec((1, nrow, npg * PAGE_SIZE), lambda b, s, pt: (s, 0, 0)),
                    const((nrow, PAGE_SIZE)), const((4, HEAD_DIM)), const((1, DIFF_V_DIM))],
        out_specs=pl.BlockSpec((1, t_new, DIFF_Q_DIM), lambda b, s, pt: (b, 0, 0)),
        scratch_shapes=[pltpu.VMEM((nrow, LANES), F32), pltpu.VMEM((nrow, LANES), F32),
                        pltpu.VMEM((nrow, DIFF_V_DIM), F32), pltpu.VMEM((nrow, DIFF_K_DIM), F32)],
    )
    a = pl.pallas_call(
        functools.partial(_diff_sample_kernel, t_new=t_new, lam_init=lam_init),
        grid_spec=grid_spec,
        out_shape=jax.ShapeDtypeStruct((bt, t_new, DIFF_Q_DIM), BF16),
        compiler_params=_params(("parallel", "arbitrary")),
        name="diff_sample",
    )(page_table, qb, kn, vn, *([ck] * npg), *([cv] * npg), bias, nbias,
      lam_p.astype(F32), out_norm.reshape(1, DIFF_V_DIM).astype(F32))
    return a.reshape(bt * t_new, DIFF_Q_DIM)


def kernel(x_prompt, x_sample, state_ssm_conv, state_ssm, cache_swa_kv0, cache_swa_kv1, cache_swa_kv2, cache_diff_k, cache_diff_v, page_table, rel_bias, norm_mix, norm_ffn, ffn_w_in, ffn_w_out, ssm_w_in, ssm_conv_w, ssm_conv_b, ssm_dt_bias, ssm_a_log, ssm_d, ssm_norm, ssm_w_out, swa_w_qkv, swa_q_norm, swa_k_norm, swa_w_out, diff_w_qkv, diff_q_norm, diff_k_norm, diff_lambda, diff_out_norm, diff_w_out):
    bp, seq, d = x_prompt.shape
    bs, t_new, _ = x_sample.shape
    mp, ms = bp * seq, bs * t_new
    tm_p, tm_s = 512, ms
    xp = x_prompt.reshape(mp, d)
    xs = x_sample.reshape(ms, d)
    swa_caches = (cache_swa_kv0, cache_swa_kv1, cache_swa_kv2)
    conv_p, conv_s = [], []
    ssm_p = ssm_s = None
    swa_p = tuple([] for _ in SWA_GROUPS)
    swa_s = tuple([] for _ in SWA_GROUPS)
    dk_p, dk_s, dv_p, dv_s = [], [], [], []
    i_ssd = i_swa = i_diff = 0
    w1, w2 = ffn_w_in.astype(BF16), ffn_w_out.astype(BF16)
    for layer in range(DEPTH):
        kind = layer % 3
        g_mix = norm_mix[layer].astype(F32)
        if kind == 0:
            i = i_ssd
            i_ssd += 1
            w_in = ssm_w_in[i].astype(BF16)
            weights = [w_in[:, :D_INNER], w_in[:, D_INNER:D_INNER + CONV_DIM],
                       jnp.pad(w_in[:, D_INNER + CONV_DIM:], ((0, 0), (0, LANES - SSM_HEADS)))]
            wo = ssm_w_out[i].astype(BF16)
            cw, cb = ssm_conv_w[i].astype(F32), ssm_conv_b[i].astype(F32)
            ssd_args = (cw, cb, ssm_dt_bias[i], ssm_a_log[i], ssm_d[i], ssm_norm[i].astype(F32))
            z, xbc, dt = _norm_proj(xp, g_mix, weights, [None] * 3, tm_p)
            xbc3 = xbc.reshape(bp, seq, CONV_DIM)
            y, ns = _ssd_mixer(z.reshape(bp, seq, D_INNER), xbc3, dt.reshape(bp, seq, LANES),
                               jnp.zeros((bp, CONV_W - 1, CONV_DIM), F32),
                               jnp.zeros((1, bp, SSM_HEADS, SSM_HEAD_DIM, D_STATE), F32), 0, *ssd_args,
                               prev_new=ssm_p)
            conv_p.append(xbc3[:, seq - (CONV_W - 1):])
            ssm_p = ns
            ap = y.reshape(mp, D_INNER)
            z, xbc, dt = _norm_proj(xs, g_mix, weights, [None] * 3, tm_s)
            xbc3 = xbc.reshape(bs, t_new, CONV_DIM)
            y, ns = _ssd_mixer(z.reshape(bs, t_new, D_INNER), xbc3, dt.reshape(bs, t_new, LANES),
                               state_ssm_conv[i], state_ssm, i, *ssd_args, prev_new=ssm_s)
            conv_s.append(xbc3[:, t_new - (CONV_W - 1):])
            ssm_s = ns
            as_ = y.reshape(ms, D_INNER)
        elif kind == 1:
            i = i_swa
            i_swa += 1
            w_qkv = swa_w_qkv[i].astype(BF16)
            weights = [w_qkv[:, :SWA_DIM], w_qkv[:, SWA_DIM:2 * SWA_DIM], w_qkv[:, 2 * SWA_DIM:]]
            gains = [jnp.tile(swa_q_norm[i].astype(F32), N_ATTN_HEADS).reshape(1, SWA_DIM),
                     jnp.tile(swa_k_norm[i].astype(F32), N_ATTN_HEADS).reshape(1, SWA_DIM), None]
            wo = swa_w_out[i].astype(BF16)
            q, kt, k, vt, v = _norm_proj(xp, g_mix, weights, gains, tm_p, emit=("b", "tb", "tb"), seq=seq)
            ap = _swa_prompt(q, k, v, rel_bias, bp, seq)
            for g, (win, dil) in enumerate(SWA_GROUPS):
                keep = min(win, seq)
                sl = slice(g * SWA_GROUP_DIM, (g + 1) * SWA_GROUP_DIM)
                kv_t = jnp.stack([kt[:, sl, seq - keep:], vt[:, sl, seq - keep:]], axis=1)
                kv_t = kv_t.reshape(bp, 2, SWA_HEADS_PER_GROUP, HEAD_DIM, keep)
                swa_p[g].append(jnp.transpose(kv_t, (0, 4, 1, 2, 3)))
            q, k, v = _norm_proj(xs, g_mix, weights, gains, tm_s)
            q3, k3, v3 = (t.reshape(bs, t_new, SWA_DIM) for t in (q, k, v))
            as_, new_bufs = _swa_sample(q3, k3, v3, [c[i] for c in swa_caches], rel_bias, bs, t_new)
            for g in range(len(SWA_GROUPS)):
                swa_s[g].append(new_bufs[g])
        else:
            i = i_diff
            i_diff += 1
            lam_init = 0.8 - 0.6 * math.exp(-0.3 * layer)
            w_qkv = diff_w_qkv[i].astype(BF16)
            weights = [w_qkv[:, :DIFF_Q_DIM], w_qkv[:, DIFF_Q_DIM:DIFF_Q_DIM + DIFF_K_DIM],
                       w_qkv[:, DIFF_Q_DIM + DIFF_K_DIM:]]
            gains = [jnp.tile(diff_q_norm[i].astype(F32), DIFF_Q_DIM // HEAD_DIM).reshape(1, DIFF_Q_DIM),
                     jnp.tile(diff_k_norm[i].astype(F32), DIFF_K_DIM // HEAD_DIM).reshape(1, DIFF_K_DIM), None]
            wo = diff_w_out[i].astype(BF16)
            q, k, kt, v, v_rows = _norm_proj(xp, g_mix, weights, gains, tm_p, emit=("n", "nt", "ns"), seq=seq)
            ap = _diff_prompt(q, k, v, rel_bias, diff_lambda[i], diff_out_norm[i], lam_init, bp, seq)
            dk_p.append(jnp.transpose(kt.reshape(bp, DIFF_KV_HEADS, 2, HEAD_DIM, seq), (0, 4, 1, 2, 3)))
            dv_p.append(v_rows.reshape(bp, seq, DIFF_KV_HEADS, DIFF_V_DIM))
            q, k, v = _norm_proj(xs, g_mix, weights, gains, tm_s)
            as_ = _diff_sample(q, k, v, cache_diff_k[i], cache_diff_v[i], page_table, rel_bias, diff_lambda[i],
                               diff_out_norm[i], lam_init, bs, t_new)
            dk_s.append(k.reshape(bs, t_new, DIFF_KV_HEADS, 2, HEAD_DIM))
            dv_s.append(v.reshape(bs, t_new, DIFF_KV_HEADS, DIFF_V_DIM))
        g_ffn = norm_ffn[layer].astype(F32)
        xp = _mix_ffn(xp, ap, wo, g_ffn, w1, w2, layer, tm_p)
        xs = _mix_ffn(xs, as_, wo, g_ffn, w1, w2, layer, tm_s)
    return (xp.reshape(bp, seq, d), xs.reshape(bs, t_new, d),
            jnp.stack(conv_p), jnp.stack(conv_s), ssm_p, ssm_s,
            jnp.stack(swa_p[0]), jnp.stack(swa_s[0]), jnp.stack(swa_p[1]), jnp.stack(swa_s[1]),
            jnp.stack(swa_p[2]), jnp.stack(swa_s[2]),
            jnp.stack(dk_p), jnp.stack(dk_s), jnp.stack(dv_p), jnp.stack(dv_s))
```
